```python
import math
import jax, jax.numpy as jnp
from jax import lax
import numpy as np

D_MODEL = 1024
BATCH = 2
SEQ = 8192
DEPTH = 1

CHUNK = 64
Q_BLOCK = 128
EPS = 1e-6

FOX_HEADS = 8
FOX_HEAD_DIM = 64
FOX_WIDTH = FOX_HEADS * FOX_HEAD_DIM
GDN_HEADS = 4
GDN_HEAD_DIM = 128
GDN_WIDTH = GDN_HEADS * GDN_HEAD_DIM
CONV_WIDTH = 4
MIX_WIDTH = FOX_WIDTH + GDN_WIDTH

IN_SPLITS = (FOX_WIDTH, FOX_WIDTH, FOX_WIDTH, FOX_HEADS,
             GDN_WIDTH, GDN_WIDTH, GDN_WIDTH,
             GDN_HEADS, GDN_HEADS, GDN_WIDTH)
IN_WIDTH = 3 * FOX_WIDTH + FOX_HEADS + 4 * GDN_WIDTH + 2 * GDN_HEADS

MEM_LEN = 256
XA_HEADS = 4
XA_HEAD_DIM = D_MODEL // XA_HEADS

N_EXPERTS = 32
TOP_K = 4
D_EXPERT = D_MODEL
SWIGLU_LIMIT = 7.0
SWIGLU_ALPHA = 1.702

kernel_name = "hybrid_fox_gdn_xattn_moe_block"


def rms_norm(x, gain):
    xf = x.astype(jnp.float32)
    y = xf * lax.rsqrt(jnp.mean(xf * xf, axis=-1, keepdims=True) + EPS)
    return (y * gain.astype(jnp.float32)).astype(x.dtype)


def l2_norm(x):
    xf = x.astype(jnp.float32)
    return xf * lax.rsqrt(jnp.sum(xf * xf, axis=-1, keepdims=True) + EPS)


def forgetting_attention(q, k, v, log_f):
    B, H, T, Dh = q.shape
    cum = jnp.cumsum(log_f, axis=-1)
    scale = Dh ** -0.5
    nb = T // Q_BLOCK
    qb = jnp.moveaxis(q.reshape(B, H, nb, Q_BLOCK, Dh), 2, 0)
    cb = jnp.moveaxis(cum.reshape(B, H, nb, Q_BLOCK), 2, 0)
    pos_k = jnp.arange(T)

    def one_block(args):
        i, q_blk, c_blk = args
        pos_q = i * Q_BLOCK + jnp.arange(Q_BLOCK)
        s = jnp.einsum('bhqd,bhkd->bhqk', q_blk, k).astype(jnp.float32) * scale
        s = s + (c_blk[..., :, None] - cum[..., None, :])
        s = jnp.where(pos_k[None, :] <= pos_q[:, None], s, -jnp.inf)
        p = jax.nn.softmax(s, axis=-1)
        return jnp.einsum('bhqk,bhkd->bhqd', p.astype(v.dtype), v)

    out = lax.map(one_block, (jnp.arange(nb), qb, cb))
    return jnp.moveaxis(out, 0, 2).reshape(B, H, T, Dh)


def causal_depthwise_conv(x, w):
    K = w.shape[-1]
    rhs = jnp.transpose(w)[:, None, :].astype(x.dtype)
    return lax.conv_general_dilated(x, rhs, window_strides=(1,), padding=[(K - 1, 0)],
                                    dimension_numbers=('NWC', 'WIO', 'NWC'),
                                    feature_group_count=x.shape[-1])


def gated_delta_rule(q, k, v, g, beta):
    B, H, T, Dk = q.shape
    Dv = v.shape[-1]
    n = T // CHUNK
    q = q.reshape(B, H, n, CHUNK, Dk)
    k = k.reshape(B, H, n, CHUNK, Dk)
    v = v.reshape(B, H, n, CHUNK, Dv)
    g = g.reshape(B, H, n, CHUNK)
    beta = beta.reshape(B, H, n, CHUNK)
    G = jnp.cumsum(g, axis=-1)
    idx = jnp.arange(CHUNK)
    lower_incl = idx[:, None] >= idx[None, :]
    strict = idx[:, None] > idx[None, :]
    decay = jnp.exp(jnp.where(lower_incl, G[..., :, None] - G[..., None, :], -jnp.inf))
    A = jnp.where(strict, beta[..., :, None] * jnp.einsum('bhncd,bhnsd->bhncs', k, k) * decay, 0.0)
    eye = jnp.eye(CHUNK, dtype=jnp.float32)
    T_inv = lax.linalg.triangular_solve(A + eye, jnp.broadcast_to(eye, A.shape),
                                        left_side=True, lower=True)
    u = T_inv @ (v * beta[..., None])
    w = T_inv @ (k * (beta * jnp.exp(G))[..., None])
    qk = jnp.where(lower_incl, jnp.einsum('bhncd,bhnsd->bhncs', q, k) * decay, 0.0)
    q_dec = q * jnp.exp(G)[..., None]
    k_dec = k * jnp.exp(G[..., -1:] - G)[..., None]
    last = jnp.exp(G[..., -1])

    def step(S, inp):
        qk_c, u_c, w_c, qd_c, kd_c, last_c = inp
        v_new = u_c - w_c @ S
        o = qd_c @ S + qk_c @ v_new
        S = S * last_c[..., None, None] + jnp.swapaxes(kd_c, -1, -2) @ v_new
        return S, o

    xs = tuple(jnp.moveaxis(a, 2, 0) for a in (qk, u, w, q_dec, k_dec, last))
    S0 = jnp.zeros((B, H, Dk, Dv), jnp.float32)
    _, o = lax.scan(step, S0, xs)
    return jnp.moveaxis(o, 0, 2).reshape(B, H, T, Dv)


def gated_deltanet(q, k, v, a, b, z, conv_w, a_log, dt_bias, out_gain):
    B, T, _ = q.shape
    qkv = jax.nn.silu(causal_depthwise_conv(jnp.concatenate([q, k, v], axis=-1), conv_w))
    qkv = qkv.astype(jnp.float32).reshape(B, T, 3, GDN_HEADS, GDN_HEAD_DIM)
    qkv = jnp.transpose(qkv, (2, 0, 3, 1, 4))
    qh = l2_norm(qkv[0]) * (GDN_HEAD_DIM ** -0.5)
    kh = l2_norm(qkv[1])
    vh = qkv[2]
    beta = jnp.transpose(jax.nn.sigmoid(b.astype(jnp.float32)), (0, 2, 1))
    g = -jnp.exp(a_log.astype(jnp.float32)) * jax.nn.softplus(a.astype(jnp.float32) + dt_bias.astype(jnp.float32))
    g = jnp.transpose(g, (0, 2, 1))
    o = jnp.transpose(gated_delta_rule(qh, kh, vh, g, beta), (0, 2, 1, 3))
    zh = z.astype(jnp.float32).reshape(B, T, GDN_HEADS, GDN_HEAD_DIM)
    o = rms_norm(o, out_gain) * jax.nn.silu(zh)
    return o.reshape(B, T, GDN_WIDTH).astype(q.dtype)


def memory_cross_attention(h, m, w_xq, w_xkv, xq_gain, xk_gain, w_xo):
    B, T, D = h.shape
    q = (h @ w_xq).reshape(B, T, XA_HEADS, XA_HEAD_DIM)
    kv = (m @ w_xkv).reshape(B, MEM_LEN, 2, XA_HEADS, XA_HEAD_DIM)
    q = rms_norm(q, xq_gain)
    k = rms_norm(kv[:, :, 0], xk_gain)
    v = kv[:, :, 1]
    s = jnp.einsum('bthd,bmhd->bhtm', q, k).astype(jnp.float32) * (XA_HEAD_DIM ** -0.5)
    p = jax.nn.softmax(s, axis=-1).astype(v.dtype)
    o = jnp.einsum('bhtm,bmhd->bthd', p, v).reshape(B, T, D)
    return o @ w_xo


def moe_ffn(h, w_router, b_router, w_gate_up, b_gate_up, w_down, b_down):
    B, T, D = h.shape
    ht = h.reshape(B * T, D)
    logits = (ht @ w_router).astype(jnp.float32) + b_router.astype(jnp.float32)
    top_val, top_idx = lax.top_k(logits, TOP_K)
    top_w = jax.nn.softmax(top_val, axis=-1)
    combine = jnp.sum(jax.nn.one_hot(top_idx, N_EXPERTS, dtype=jnp.float32) * top_w[..., None], axis=1)
    combine = combine.astype(h.dtype)
    y = jnp.zeros_like(ht)
    for e in range(N_EXPERTS):
        gu = ht @ w_gate_up[e] + b_gate_up[e]
        gate = jnp.minimum(gu[:, :D_EXPERT], SWIGLU_LIMIT)
        up = jnp.clip(gu[:, D_EXPERT:], -SWIGLU_LIMIT, SWIGLU_LIMIT)
        act = (up + 1.0) * (gate * jax.nn.sigmoid(SWIGLU_ALPHA * gate))
        y = y + combine[:, e:e + 1] * (act @ w_down[e] + b_down[e])
    return y.reshape(B, T, D)


def setup_inputs(seed: int = 0) -> dict:
    key = jax.random.key(seed)
    ks = jax.random.split(key, 32)
    L, D, F, E = DEPTH, D_MODEL, D_EXPERT, N_EXPERTS

    def nrm(k, shape, scale):
        return jax.random.normal(k, shape, jnp.float32) * scale

    def gain(k, shape):
        return 1.0 + 0.02 * jax.random.normal(k, shape, jnp.float32)

    dt = jnp.exp(jax.random.uniform(ks[9], (L, GDN_HEADS), jnp.float32, math.log(1e-3), math.log(1e-1)))
    return {
        "x": nrm(ks[0], (BATCH, SEQ, D), 1.0),
        "mem": nrm(ks[1], (BATCH, MEM_LEN, D), 1.0),
        "mix_norm": gain(ks[2], (L, D)),
        "w_in": nrm(ks[3], (L, D, IN_WIDTH), D ** -0.5),
        "b_forget": 3.0 + nrm(ks[4], (L, FOX_HEADS), 1.0),
        "fox_q_gain": gain(ks[5], (L, FOX_HEAD_DIM)),
        "fox_k_gain": gain(ks[6], (L, FOX_HEAD_DIM)),
        "fox_out_gain": gain(ks[7], (L, FOX_HEADS, FOX_HEAD_DIM)),
        "gdn_conv_w": nrm(ks[8], (L, 3 * GDN_WIDTH, CONV_WIDTH), CONV_WIDTH ** -0.5),
        "gdn_a_log": jnp.log(jax.random.uniform(ks[10], (L, GDN_HEADS), jnp.float32, 1.0, 16.0)),
        "gdn_dt_bias": dt + jnp.log(-jnp.expm1(-dt)),
        "gdn_out_gain": gain(ks[11], (L, GDN_HEAD_DIM)),
        "w_out": nrm(ks[12], (L, MIX_WIDTH, D), MIX_WIDTH ** -0.5),
        "xattn_norm": gain(ks[13], (L, D)),
        "mem_norm": gain(ks[14], (L, D)),
        "w_xq": nrm(ks[15], (L, D, D), D ** -0.5),
        "w_xkv": nrm(ks[16], (L, D, 2 * D), D ** -0.5),
        "xq_gain": gain(ks[17], (L, XA_HEAD_DIM)),
        "xk_gain": gain(ks[18], (L, XA_HEAD_DIM)),
        "w_xo": nrm(ks[19], (L, D, D), D ** -0.5),
        "moe_norm": gain(ks[20], (L, D)),
        "w_router": nrm(ks[21], (L, D, E), D ** -0.5),
        "b_router": nrm(ks[22], (L, E), 0.01),
        "w_gate_up": nrm(ks[23], (L, E, D, 2 * F), D ** -0.5),
        "b_gate_up": nrm(ks[24], (L, E, 2 * F), 0.01),
        "w_down": nrm(ks[25], (L, E, F, D), F ** -0.5),
        "b_down": nrm(ks[26], (L, E, D), 0.01),
    }


def reference(x, mem, mix_norm, w_in, b_forget, fox_q_gain, fox_k_gain, fox_out_gain,
              gdn_conv_w, gdn_a_log, gdn_dt_bias, gdn_out_gain, w_out,
              xattn_norm, mem_norm, w_xq, w_xkv, xq_gain, xk_gain, w_xo,
              moe_norm, w_router, b_router, w_gate_up, b_gate_up, w_down, b_down):
    B, T, D = x.shape
    split_at = [int(i) for i in np.cumsum(IN_SPLITS)[:-1]]
    for l in range(DEPTH):
        h = rms_norm(x, mix_norm[l])
        proj = h @ w_in[l]
        fq, fk, fv, ff, gq, gk, gv, ga, gb, gz = jnp.split(proj, split_at, axis=-1)
        fq = rms_norm(fq.reshape(B, T, FOX_HEADS, FOX_HEAD_DIM), fox_q_gain[l])
        fk = rms_norm(fk.reshape(B, T, FOX_HEADS, FOX_HEAD_DIM), fox_k_gain[l])
        fv = fv.reshape(B, T, FOX_HEADS, FOX_HEAD_DIM)
        log_f = jax.nn.log_sigmoid(ff.astype(jnp.float32) + b_forget[l].astype(jnp.float32))
        o_fox = forgetting_attention(jnp.transpose(fq, (0, 2, 1, 3)), jnp.transpose(fk, (0, 2, 1, 3)),
                                     jnp.transpose(fv, (0, 2, 1, 3)), jnp.transpose(log_f, (0, 2, 1)))
        o_fox = rms_norm(jnp.transpose(o_fox, (0, 2, 1, 3)), fox_out_gain[l]).reshape(B, T, FOX_WIDTH)
        o_gdn = gated_deltanet(gq, gk, gv, ga, gb, gz, gdn_conv_w[l], gdn_a_log[l],
                               gdn_dt_bias[l], gdn_out_gain[l])
        x = x + jnp.concatenate([o_fox, o_gdn], axis=-1) @ w_out[l]
        x = x + memory_cross_attention(rms_norm(x, xattn_norm[l]), rms_norm(mem, mem_norm[l]),
                                       w_xq[l], w_xkv[l], xq_gain[l], xk_gain[l], w_xo[l])
        x = x + moe_ffn(rms_norm(x, moe_norm[l]), w_router[l], b_router[l],
                        w_gate_up[l], b_gate_up[l], w_down[l], b_down[l])
    return x
```

```python
import functools

import jax
import jax.numpy as jnp
from jax import lax
from jax.experimental import pallas as pl
from jax.experimental.pallas import tpu as pltpu

F32 = jnp.float32
BF16 = jnp.bfloat16
HIGHEST = lax.Precision.HIGHEST

EPS = 1e-6
FOX_HEADS, FOX_HEAD_DIM = 8, 64
GDN_HEADS, GDN_HEAD_DIM = 4, 128
FOX_WIDTH = FOX_HEADS * FOX_HEAD_DIM
GDN_WIDTH = GDN_HEADS * GDN_HEAD_DIM
CONV_WIDTH = 4
CHUNK = 64
XA_HEADS = 4
N_EXPERTS = 32
TOP_K = 4
SWIGLU_LIMIT = 7.0
SWIGLU_ALPHA = 1.702

LANES = 128
SUBLANES = 8
VMEM_LIMIT = 52 * 1024 * 1024

SM_F0, SM_G0, SM_B0, SM_END = 0, 8, 12, 16

NT_DIMS = (((1,), (1,)), ((), ()))


def _cparams(sem):
    return pltpu.CompilerParams(dimension_semantics=sem, vmem_limit_bytes=VMEM_LIMIT)


def _rms(x, gain):
    return x * lax.rsqrt(jnp.mean(x * x, axis=-1, keepdims=True) + EPS) * gain


def _softplus(t):
    return jnp.maximum(t, 0.0) + jnp.log1p(jnp.exp(-jnp.abs(t)))


def _sigmoid(t):
    return 1.0 / (1.0 + jnp.exp(-t))


def _small_act(v, bias, alog, idx):
    t = v + bias
    tail = jnp.log1p(jnp.exp(-jnp.abs(t)))
    log_f = jnp.minimum(t, 0.0) - tail
    g = -jnp.exp(alog) * (jnp.maximum(t, 0.0) + tail)
    beta = _sigmoid(v)
    return jnp.where(idx < SM_G0, log_f, jnp.where(idx < SM_B0, g, jnp.where(idx < SM_END, beta, 0.0)))


def _inproj_kernel(x_ref, gain_ref, w_ref, ws_ref, wst_ref, bd_ref, qg_ref, kg_ref, prow_ref, pcol_ref,
                   fq_ref, fk_ref, fv_ref, gqkv_ref, gz_ref, sm_ref, smt_ref, carry_r, carry_c,
                   *, tm, tiles_per_seq):
    i = pl.program_id(0)
    hb = _rms(x_ref[...], gain_ref[...]).astype(BF16)

    def proj(lo, hi):
        return jnp.dot(hb, w_ref[:, lo:hi], preferred_element_type=F32)

    def headnorm(p, g):
        ss = jnp.dot(p * p, bd_ref[...], precision=HIGHEST, preferred_element_type=F32)
        return p * lax.rsqrt(ss * (1.0 / FOX_HEAD_DIM) + EPS) * g

    w0 = FOX_WIDTH
    fq_ref[...] = (headnorm(proj(0, w0), qg_ref[...]) * FOX_HEAD_DIM ** -0.5).astype(BF16)
    fk_ref[...] = headnorm(proj(w0, 2 * w0), kg_ref[...]).astype(BF16)
    fv_ref[...] = proj(2 * w0, 3 * w0).astype(BF16)
    g0 = 3 * w0
    gqkv_ref[...] = proj(g0, g0 + 3 * GDN_WIDTH)
    gz_ref[...] = proj(g0 + 3 * GDN_WIDTH, g0 + 4 * GDN_WIDTH)

    @pl.when(i % tiles_per_seq == 0)
    def _():
        carry_r[...] = jnp.zeros_like(carry_r)
        carry_c[...] = jnp.zeros_like(carry_c)

    sm = jnp.dot(hb, ws_ref[...], preferred_element_type=F32)
    smt = lax.dot_general(wst_ref[...], hb, NT_DIMS, preferred_element_type=F32)
    lane = lax.broadcasted_iota(jnp.int32, (1, LANES), 1)
    srow = lax.broadcasted_iota(jnp.int32, (LANES, 1), 0)
    vals = _small_act(sm, prow_ref[0:1, :], prow_ref[1:2, :], lane)
    vals_t = _small_act(smt, pcol_ref[:, 0:1], pcol_ref[:, 1:2], srow)

    r = lax.broadcasted_iota(jnp.int32, (tm, tm), 0)
    c = lax.broadcasted_iota(jnp.int32, (tm, tm), 1)
    cum = jnp.dot((c <= r).astype(F32), vals, precision=HIGHEST, preferred_element_type=F32) + carry_r[...]
    cum_t = jnp.dot(vals_t, (r <= c).astype(F32), precision=HIGHEST, preferred_element_type=F32) + carry_c[...]
    carry_r[...] = cum[tm - 1:tm, :]
    carry_c[...] = cum_t[:, tm - 1:tm]
    sm_ref[...] = jnp.where(lane < SM_G0, cum, vals)
    smt_ref[...] = jnp.where(srow < SM_G0, cum_t, vals_t)


def _inproj(x2d, gain, w_main, w_small, w_small_t, bd, qg, kg, prow, pcol, *, seq, tm):
    n, d = x2d.shape
    wm = w_main.shape[1]
    full = lambda shape: pl.BlockSpec(shape, lambda i: (0,) * len(shape))
    rows = lambda width: pl.BlockSpec((tm, width), lambda i: (i, 0))
    out_shape = (
        jax.ShapeDtypeStruct((n, FOX_WIDTH), BF16),
        jax.ShapeDtypeStruct((n, FOX_WIDTH), BF16),
        jax.ShapeDtypeStruct((n, FOX_WIDTH), BF16),
        jax.ShapeDtypeStruct((n, 3 * GDN_WIDTH), F32),
        jax.ShapeDtypeStruct((n, GDN_WIDTH), F32),
        jax.ShapeDtypeStruct((n, LANES), F32),
        jax.ShapeDtypeStruct((LANES, n), F32),
    )
    return pl.pallas_call(
        functools.partial(_inproj_kernel, tm=tm, tiles_per_seq=seq // tm),
        grid=(n // tm,),
        in_specs=[rows(d), full((1, d)), full((d, wm)), full((d, LANES)), full((LANES, d)),
                  full((FOX_WIDTH, FOX_WIDTH)), full((1, FOX_WIDTH)), full((1, FOX_WIDTH)),
                  full((SUBLANES, LANES)), full((LANES, 2))],
        out_specs=(rows(FOX_WIDTH), rows(FOX_WIDTH), rows(FOX_WIDTH), rows(3 * GDN_WIDTH), rows(GDN_WIDTH),
                   rows(LANES), pl.BlockSpec((LANES, tm), lambda i: (0, i))),
        out_shape=out_shape,
        scratch_shapes=[pltpu.VMEM((1, LANES), F32), pltpu.VMEM((LANES, 1), F32)],
        compiler_params=_cparams(("arbitrary",)),
        name="inproj",
    )(x2d, gain, w_main, w_small, w_small_t, bd, qg, kg, prow, pcol)


def _fox_kernel(q_ref, k_ref, v_ref, cum_ref, gain_ref, o_ref, m_ref, l_ref, acc_ref, *, blk):
    qi = pl.program_id(2)
    lane = lax.broadcasted_iota(jnp.int32, (1, LANES), 1)
    lo = lane < FOX_HEAD_DIM
    q = q_ref[...]
    zero = jnp.zeros_like(q)
    q_heads = (jnp.where(lo, q, zero), jnp.where(lo, zero, q))
    q0 = pl.multiple_of(qi * blk, blk)
    c_ref = [cum_ref[hh:hh + 1, pl.ds(q0, LANES)][:, 0:1] for hh in range(2)]

    m_ref[...] = jnp.full_like(m_ref, -1e30)
    l_ref[...] = jnp.zeros_like(l_ref)
    acc_ref[...] = jnp.zeros_like(acc_ref)

    def step(k0, masked):
        kb = k_ref[pl.ds(k0, blk), :]
        vb = v_ref[pl.ds(k0, blk), :]
        vzero = jnp.zeros_like(vb)
        v_heads = (jnp.where(lo, vb, vzero), jnp.where(lo, vzero, vb))
        pv = None
        alphas = []
        for hh in range(2):
            s = lax.dot_general(q_heads[hh], kb, NT_DIMS, preferred_element_type=F32)
            s = s + (c_ref[hh] - cum_ref[hh:hh + 1, pl.ds(k0, blk)])
            if masked:
                r = lax.broadcasted_iota(jnp.int32, (blk, blk), 0)
                c = lax.broadcasted_iota(jnp.int32, (blk, blk), 1)
                s = jnp.where(c <= r, s, -jnp.inf)
            m_old = m_ref[hh]
            m_new = jnp.maximum(m_old, jnp.max(s, axis=-1, keepdims=True))
            alpha = jnp.exp(m_old - m_new)
            p = jnp.exp(s - m_new)
            l_ref[hh] = alpha * l_ref[hh] + jnp.sum(p, axis=-1, keepdims=True)
            m_ref[hh] = m_new
            alphas.append(alpha)
            o_h = jnp.dot(p.astype(BF16), v_heads[hh], preferred_element_type=F32)
            pv = o_h if pv is None else pv + o_h
        acc_ref[...] = acc_ref[...] * jnp.where(lo, alphas[0], alphas[1]) + pv

    def body(ki, carry):
        step(pl.multiple_of(ki * blk, blk), False)
        return carry

    lax.fori_loop(0, qi, body, 0)
    step(q0, True)

    o = acc_ref[...] / jnp.where(lo, l_ref[0], l_ref[1])
    o2 = o * o
    ss_lo = jnp.sum(jnp.where(lo, o2, 0.0), axis=-1, keepdims=True)
    ss_hi = jnp.sum(jnp.where(lo, 0.0, o2), axis=-1, keepdims=True)
    ms = jnp.where(lo, ss_lo, ss_hi) * (1.0 / FOX_HEAD_DIM)
    o_ref[...] = (o * lax.rsqrt(ms + EPS) * gain_ref[...]).astype(o_ref.dtype)


def _fox_attention(fq, fk, fv, cum, gain_pairs, *, batch, seq, blk):
    npairs = FOX_HEADS // 2
    return pl.pallas_call(
        functools.partial(_fox_kernel, blk=blk),
        grid=(batch, npairs, seq // blk),
        in_specs=[
            pl.BlockSpec((None, blk, LANES), lambda b, j, i: (b, i, j)),
            pl.BlockSpec((None, seq, LANES), lambda b, j, i: (b, 0, j)),
            pl.BlockSpec((None, seq, LANES), lambda b, j, i: (b, 0, j)),
            pl.BlockSpec((None, None, 2, seq), lambda b, j, i: (b, j, 0, 0)),
            pl.BlockSpec((None, 1, LANES), lambda b, j, i: (j, 0, 0)),
        ],
        out_specs=pl.BlockSpec((None, blk, LANES), lambda b, j, i: (b, i, j)),
        out_shape=jax.ShapeDtypeStruct((batch, seq, FOX_WIDTH), BF16),
        scratch_shapes=[pltpu.VMEM((2, blk, 1), F32), pltpu.VMEM((2, blk, 1), F32),
                        pltpu.VMEM((blk, LANES), F32)],
        compiler_params=_cparams(("parallel", "parallel", "arbitrary")),
        name="fox_attention",
    )(fq, fk, fv, cum, gain_pairs)


GDN_BLK = 4 * CHUNK


def _gdn_prep_kernel(x_ref, halo_ref, cw_ref, sm_ref, smt_ref,
                     u_ref, w_ref, qd_ref, qk_ref, kdt_ref, egl_ref, xpad_ref):
    i = pl.program_id(1)
    nb = GDN_BLK
    halo = halo_ref[...]
    xpad_ref[0:SUBLANES, :] = jnp.where(i > 0, halo, jnp.zeros_like(halo))
    xpad_ref[SUBLANES:, :] = x_ref[...]
    y = None
    for j in range(CONV_WIDTH):
        start = SUBLANES - (CONV_WIDTH - 1) + j
        term = cw_ref[j:j + 1, :] * xpad_ref[start:start + nb, :]
        y = term if y is None else y + term
    y = y * _sigmoid(y)

    r = lax.broadcasted_iota(jnp.int32, (nb, nb), 0)
    c = lax.broadcasted_iota(jnp.int32, (nb, nb), 1)
    chunk_shift = CHUNK.bit_length() - 1
    same = jnp.right_shift(r, chunk_shift) == jnp.right_shift(c, chunk_shift)
    incl = same & (c <= r)
    strict = same & (c < r)
    sm = sm_ref[...]
    g_cum = jnp.dot(incl.astype(F32), sm, precision=HIGHEST, preferred_element_type=F32)
    g_tot = jnp.dot(same.astype(F32), sm, precision=HIGHEST, preferred_element_type=F32)
    g_cum_t = jnp.dot(smt_ref[...], (same & (r <= c)).astype(F32), precision=HIGHEST,
                      preferred_element_type=F32)
    eye = (r == c).astype(F32)

    for h in range(GDN_HEADS):
        sl = slice(h * GDN_HEAD_DIM, (h + 1) * GDN_HEAD_DIM)
        q = y[:, h * GDN_HEAD_DIM:(h + 1) * GDN_HEAD_DIM]
        k = y[:, GDN_WIDTH + h * GDN_HEAD_DIM:GDN_WIDTH + (h + 1) * GDN_HEAD_DIM]
        v = y[:, 2 * GDN_WIDTH + h * GDN_HEAD_DIM:2 * GDN_WIDTH + (h + 1) * GDN_HEAD_DIM]
        qn = q * lax.rsqrt(jnp.sum(q * q, axis=-1, keepdims=True) + EPS) * GDN_HEAD_DIM ** -0.5
        kn = k * lax.rsqrt(jnp.sum(k * k, axis=-1, keepdims=True) + EPS)
        gc = g_cum[:, SM_G0 + h:SM_G0 + h + 1]
        gl = g_tot[:, SM_G0 + h:SM_G0 + h + 1]
        gr = g_cum_t[SM_G0 + h:SM_G0 + h + 1, :]
        beta = sm[:, SM_B0 + h:SM_B0 + h + 1]
        decay = jnp.where(incl, jnp.exp(jnp.where(incl, gc - gr, 0.0)), 0.0)
        qb, kb = qn.astype(BF16), kn.astype(BF16)
        kk = lax.dot_general(kb, kb, NT_DIMS, preferred_element_type=F32)
        a = jnp.where(strict, beta * kk * decay, 0.0)
        t_inv = eye - a
        p = a
        for _ in range(5):
            pb = p.astype(BF16)
            p = jnp.dot(pb, pb, preferred_element_type=F32)
            t_inv = t_inv + jnp.dot(t_inv.astype(BF16), p.astype(BF16), preferred_element_type=F32)
        tb = t_inv.astype(BF16)
        eg = jnp.exp(gc)
        u_ref[:, sl] = jnp.dot(tb, (v * beta).astype(BF16), preferred_element_type=F32)
        w_ref[:, sl] = jnp.dot(tb, (kn * (beta * eg)).astype(BF16), preferred_element_type=F32).astype(BF16)
        qk = lax.dot_general(qb, kb, NT_DIMS, preferred_element_type=F32)
        qk_ref[:, h * nb:(h + 1) * nb] = jnp.where(incl, qk * decay, 0.0).astype(BF16)
        qd_ref[:, sl] = (qn * eg).astype(BF16)
        kdt_ref[sl, :] = (kn * jnp.exp(gl - gc)).T.astype(BF16)
        egl_ref[:, sl] = jnp.broadcast_to(jnp.exp(gl), (nb, GDN_HEAD_DIM))


def _gdn_prep(gqkv, conv_w_t, sm, smt, *, batch, seq):
    nb = GDN_BLK
    bps = seq // nb
    hps = nb // SUBLANES
    width = 3 * GDN_WIDTH
    row = lambda w: pl.BlockSpec((None, nb, w), lambda b, i: (b, i, 0))
    return pl.pallas_call(
        _gdn_prep_kernel,
        grid=(batch, bps),
        in_specs=[
            pl.BlockSpec((nb, width), lambda b, i: (b * bps + i, 0)),
            pl.BlockSpec((SUBLANES, width), lambda b, i: (jnp.maximum((b * bps + i) * hps - 1, 0), 0)),
            pl.BlockSpec((CONV_WIDTH, width), lambda b, i: (0, 0)),
            pl.BlockSpec((nb, LANES), lambda b, i: (b * bps + i, 0)),
            pl.BlockSpec((LANES, nb), lambda b, i: (0, b * bps + i)),
        ],
        out_specs=(row(GDN_WIDTH), row(GDN_WIDTH), row(GDN_WIDTH), row(GDN_HEADS * nb),
                   pl.BlockSpec((None, GDN_WIDTH, nb), lambda b, i: (b, 0, i)), row(GDN_WIDTH)),
        out_shape=(
            jax.ShapeDtypeStruct((batch, seq, GDN_WIDTH), F32),
            jax.ShapeDtypeStruct((batch, seq, GDN_WIDTH), BF16),
            jax.ShapeDtypeStruct((batch, seq, GDN_WIDTH), BF16),
            jax.ShapeDtypeStruct((batch, seq, GDN_HEADS * nb), BF16),
            jax.ShapeDtypeStruct((batch, GDN_WIDTH, seq), BF16),
            jax.ShapeDtypeStruct((batch, seq, GDN_WIDTH), F32),
        ),
        scratch_shapes=[pltpu.VMEM((nb + SUBLANES, width), F32)],
        compiler_params=_cparams(("parallel", "parallel")),
        name="gdn_prep",
    )(gqkv, gqkv, conv_w_t, sm, smt)


def _gdn_scan_kernel(u_ref, w_ref, qd_ref, qk_ref, kdt_ref, egl_ref, z_ref, gain_ref, o_ref, s_ref, vz_ref,
                     *, batch):
    nb = GDN_BLK

    @pl.when(pl.program_id(0) == 0)
    def _():
        s_ref[...] = jnp.zeros_like(s_ref)

    vz_ref[...] = jnp.zeros_like(vz_ref)
    for cidx in range(nb // CHUNK):
        rows = slice(cidx * CHUNK, (cidx + 1) * CHUNK)
        for b in range(batch):
            for h in range(GDN_HEADS):
                bh = b * GDN_HEADS + h
                sl = slice(h * GDN_HEAD_DIM, (h + 1) * GDN_HEAD_DIM)
                s_old = s_ref[bh]
                lhs1 = jnp.concatenate([w_ref[b, rows, sl], qd_ref[b, rows, sl]], axis=0)
                r1 = jnp.dot(lhs1, s_old.astype(BF16), preferred_element_type=F32)
                v_new = u_ref[b, rows, sl] - r1[:CHUNK]
                vz_ref[bh, rows, :] = v_new.astype(BF16)
                lhs2 = jnp.concatenate([qk_ref[b, rows, h * nb:(h + 1) * nb], kdt_ref[b, sl, :]], axis=0)
                r2 = jnp.dot(lhs2, vz_ref[bh], preferred_element_type=F32)
                vz_ref[bh, rows, :] = jnp.zeros((CHUNK, GDN_HEAD_DIM), BF16)
                last = egl_ref[b, (cidx + 1) * CHUNK - 1:(cidx + 1) * CHUNK, sl]
                s_ref[bh] = s_old * last + r2[CHUNK:]
                o = r1[CHUNK:] + r2[:CHUNK]
                z = z_ref[b, rows, sl]
                o_ref[b, rows, sl] = (_rms(o, gain_ref[...]) * (z * _sigmoid(z))).astype(o_ref.dtype)


def _gdn_scan(u, w, qd, qk, kdt, egl, z, gain, *, batch, seq):
    nb = GDN_BLK
    row = lambda width: pl.BlockSpec((batch, nb, width), lambda i: (0, i, 0))
    return pl.pallas_call(
        functools.partial(_gdn_scan_kernel, batch=batch),
        grid=(seq // nb,),
        in_specs=[row(GDN_WIDTH), row(GDN_WIDTH), row(GDN_WIDTH), row(GDN_HEADS * nb),
                  pl.BlockSpec((batch, GDN_WIDTH, nb), lambda i: (0, 0, i)), row(GDN_WIDTH), row(GDN_WIDTH),
                  pl.BlockSpec((1, GDN_HEAD_DIM), lambda i: (0, 0))],
        out_specs=row(GDN_WIDTH),
        out_shape=jax.ShapeDtypeStruct((batch, seq, GDN_WIDTH), BF16),
        scratch_shapes=[pltpu.VMEM((batch * GDN_HEADS, GDN_HEAD_DIM, GDN_HEAD_DIM), F32),
                        pltpu.VMEM((batch * GDN_HEADS, nb, GDN_HEAD_DIM), BF16)],
        compiler_params=_cparams(("arbitrary",)),
        name="gdn_scan",
    )(u, w, qd, qk, kdt, egl, z, gain)


def _memkv_kernel(m_ref, gain_ref, w_ref, kg_ref, k_ref, v_ref):
    d = m_ref.shape[-1]
    hd = d // XA_HEADS
    mb = _rms(m_ref[...], gain_ref[...]).astype(BF16)
    kv = jnp.dot(mb, w_ref[...], preferred_element_type=F32)
    for h in range(XA_HEADS):
        sl = slice(h * hd, (h + 1) * hd)
        k_ref[:, sl] = _rms(kv[:, sl], kg_ref[...]).astype(BF16)
    v_ref[...] = kv[:, d:].astype(BF16)


def _memkv(mem2d, gain, w_xkv, xk_gain, *, batch, mem_len):
    d = mem2d.shape[-1]
    full = lambda shape: pl.BlockSpec(shape, lambda b: (0,) * len(shape))
    row = pl.BlockSpec((mem_len, d), lambda b: (b, 0))
    return pl.pallas_call(
        _memkv_kernel,
        grid=(batch,),
        in_specs=[row, full((1, d)), full((d, 2 * d)), full((1, d // XA_HEADS))],
        out_specs=(row, row),
        out_shape=(jax.ShapeDtypeStruct(mem2d.shape, BF16), jax.ShapeDtypeStruct(mem2d.shape, BF16)),
        compiler_params=_cparams(("parallel",)),
        name="mem_kv",
    )(mem2d, gain, w_xkv, xk_gain)


def _mid_kernel(x_ref, of_ref, og_ref, wo_ref, xg_ref, wq_ref, qg_ref, k_ref, v_ref, wxo_ref, mg_ref,
                wr_ref, br_ref,
                x2_ref, h3_ref, mi_ref, mw_ref, cnt_ref, carry_ref, *, tm):
    i = pl.program_id(0)
    d = x_ref.shape[-1]
    hd = d // XA_HEADS
    x1 = (x_ref[...]
          + jnp.dot(of_ref[...], wo_ref[0:FOX_WIDTH, :], preferred_element_type=F32)
          + jnp.dot(og_ref[...], wo_ref[FOX_WIDTH:, :], preferred_element_type=F32))
    h2 = _rms(x1, xg_ref[...]).astype(BF16)
    q = jnp.dot(h2, wq_ref[...], preferred_element_type=F32)
    heads = []
    for h in range(XA_HEADS):
        sl = slice(h * hd, (h + 1) * hd)
        qn = (_rms(q[:, sl], qg_ref[...]) * hd ** -0.5).astype(BF16)
        s = lax.dot_general(qn, k_ref[:, sl], NT_DIMS, preferred_element_type=F32)
        p = jnp.exp(s - jnp.max(s, axis=-1, keepdims=True))
        p = p / jnp.sum(p, axis=-1, keepdims=True)
        heads.append(jnp.dot(p.astype(BF16), v_ref[:, sl], preferred_element_type=F32).astype(BF16))
    x2 = x1 + jnp.dot(jnp.concatenate(heads, axis=-1), wxo_ref[...], preferred_element_type=F32)
    x2_ref[...] = x2
    h3 = _rms(x2, mg_ref[...])
    h3_ref[...] = h3
    logits = jnp.dot(h3, wr_ref[...], precision=HIGHEST, preferred_element_type=F32) + br_ref[...]
    lane = lax.broadcasted_iota(jnp.int32, (tm, LANES), 1)
    work = logits
    vals, idxs = [], []
    onehot = jnp.zeros((tm, LANES), F32)
    for _ in range(TOP_K):
        mx = jnp.max(work, axis=-1, keepdims=True)
        idx = jnp.min(jnp.where(work == mx, lane, LANES), axis=-1, keepdims=True)
        sel = lane == idx
        onehot = jnp.where(sel, 1.0, onehot)
        work = jnp.where(sel, -jnp.inf, work)
        vals.append(mx)
        idxs.append(idx)
    es = [jnp.exp(v - vals[0]) for v in vals]
    denom = es[0] + es[1] + es[2] + es[3]

    @pl.when(i == 0)
    def _():
        carry_ref[...] = jnp.zeros_like(carry_ref)

    r = lax.broadcasted_iota(jnp.int32, (tm, tm), 0)
    c = lax.broadcasted_iota(jnp.int32, (tm, tm), 1)
    before = jnp.dot((c < r).astype(BF16), onehot.astype(BF16), preferred_element_type=F32) + carry_ref[...]
    mi = jnp.zeros((tm, LANES), jnp.int32)
    mw = jnp.zeros((tm, LANES), F32)
    for kk in range(TOP_K):
        rank = jnp.sum(jnp.where(lane == idxs[kk], before, 0.0), axis=-1, keepdims=True).astype(jnp.int32)
        mi = jnp.where(lane == kk, idxs[kk], mi)
        mi = jnp.where(lane == TOP_K + kk, rank, mi)
        mw = jnp.where(lane == kk, es[kk] / denom, mw)
    mi_ref[...] = mi
    mw_ref[...] = mw
    total = carry_ref[...] + jnp.sum(onehot, axis=0, keepdims=True)
    carry_ref[...] = total
    cnt_ref[...] = jnp.broadcast_to(total, cnt_ref.shape).astype(jnp.int32)


def _mid(x2d, o_fox, o_gdn, w_out, xg, w_xq, xq_gain, kx, vx, w_xo, mg, w_r, b_r, *, seq, mem_len, tm):
    n, d = x2d.shape
    full = lambda shape: pl.BlockSpec(shape, lambda i: (0,) * len(shape))
    rows = lambda width: pl.BlockSpec((tm, width), lambda i: (i, 0))
    mem = pl.BlockSpec((mem_len, d), lambda i: (i // (seq // tm), 0))
    return pl.pallas_call(
        functools.partial(_mid_kernel, tm=tm),
        grid=(n // tm,),
        in_specs=[rows(d), rows(FOX_WIDTH), rows(GDN_WIDTH), full((d, d)), full((1, d)), full((d, d)),
                  full((1, d // XA_HEADS)), mem, mem, full((d, d)), full((1, d)), full((d, LANES)),
                  full((1, LANES))],
        out_specs=(rows(d), rows(d), rows(LANES), rows(LANES), full((SUBLANES, LANES))),
        out_shape=(jax.ShapeDtypeStruct((n, d), F32), jax.ShapeDtypeStruct((n, d), F32),
                   jax.ShapeDtypeStruct((n, LANES), jnp.int32), jax.ShapeDtypeStruct((n, LANES), F32),
                   jax.ShapeDtypeStruct((SUBLANES, LANES), jnp.int32)),
        scratch_shapes=[pltpu.VMEM((1, LANES), F32)],
        compiler_params=_cparams(("arbitrary",)),
        name="outproj_xattn_router",
    )(x2d, o_fox, o_gdn, w_out, xg, w_xq, xq_gain, kx, vx, w_xo, mg, w_r, b_r)


MOE_TM = 512
DISPATCH_TOKENS = 128


def _dispatch_kernel(pos_ref, src_ref, dst_ref, sem):
    nt = DISPATCH_TOKENS
    base = pl.program_id(0) * nt

    def issue(t, carry):
        for kk in range(TOP_K):
            pltpu.make_async_copy(src_ref.at[base + t], dst_ref.at[pos_ref[0, t * TOP_K + kk]], sem).start()
        return carry

    lax.fori_loop(0, nt, issue, 0)

    def drain(t, carry):
        for kk in range(TOP_K):
            pltpu.make_async_copy(src_ref.at[base + t], dst_ref.at[pos_ref[0, t * TOP_K + kk]], sem).wait()
        return carry

    lax.fori_loop(0, nt, drain, 0)


def _dispatch(h3_tiles, pos, *, rows_padded):
    n = h3_tiles.shape[0]
    nt = DISPATCH_TOKENS
    return pl.pallas_call(
        _dispatch_kernel,
        grid=(n // nt,),
        in_specs=[pl.BlockSpec((None, 1, nt * TOP_K), lambda i: (i, 0, 0), memory_space=pltpu.SMEM),
                  pl.BlockSpec(memory_space=pl.ANY)],
        out_specs=pl.BlockSpec(memory_space=pl.ANY),
        out_shape=jax.ShapeDtypeStruct((rows_padded,) + h3_tiles.shape[1:], F32),
        scratch_shapes=[pltpu.SemaphoreType.DMA],
        compiler_params=pltpu.CompilerParams(dimension_semantics=("arbitrary",), has_side_effects=True),
        name="moe_dispatch",
    )(pos, h3_tiles)


def _expert_kernel(te_ref, nv_ref, xs_ref, wgu_ref, bgu_ref, wd_ref, bd_ref, y_ref, wgu_bf, wd_bf, acc_ref,
                   *, chunk):
    i = pl.program_id(0)
    tm = xs_ref.shape[0]
    d = wd_bf.shape[1]
    f = wd_bf.shape[0]
    nvalid = nv_ref[i]
    first = jnp.logical_or(i == 0, te_ref[i] != te_ref[jnp.maximum(i - 1, 0)])

    @pl.when(jnp.logical_and(first, nvalid > 0))
    def _():
        wgu_bf[...] = wgu_ref[0].astype(BF16)
        wd_bf[...] = wd_ref[0].astype(BF16)

    @pl.when(nvalid > 0)
    def _():
        row = lax.broadcasted_iota(jnp.int32, (tm, 1), 0)
        x = jnp.concatenate([xs_ref[:, cc, :] for cc in range(SUBLANES)], axis=-1)
        x = jnp.where(row < nvalid, x, 0.0).astype(BF16)
        for j in range(f // chunk):
            cs = slice(j * chunk, (j + 1) * chunk)
            us = slice(f + j * chunk, f + (j + 1) * chunk)
            g = jnp.dot(x, wgu_bf[:, cs], preferred_element_type=F32) + bgu_ref[0, :, cs]
            u = jnp.dot(x, wgu_bf[:, us], preferred_element_type=F32) + bgu_ref[0, :, us]
            gate = jnp.minimum(g, SWIGLU_LIMIT)
            up = jnp.clip(u, -SWIGLU_LIMIT, SWIGLU_LIMIT)
            act = ((up + 1.0) * (gate * _sigmoid(SWIGLU_ALPHA * gate))).astype(BF16)
            part = jnp.dot(act, wd_bf[cs, :], preferred_element_type=F32)
            if j == 0:
                acc_ref[...] = part + bd_ref[0]
            else:
                acc_ref[...] += part
        for cc in range(SUBLANES):
            y_ref[:, cc, :] = acc_ref[:, cc * LANES:(cc + 1) * LANES]

    @pl.when(nvalid <= 0)
    def _():
        y_ref[...] = jnp.zeros_like(y_ref)


def _experts(tile_expert, tile_valid, xs, w_gate_up, b_gate_up, w_down, b_down):
    rows_padded = xs.shape[0]
    e, d, f2 = w_gate_up.shape
    f = f2 // 2
    tm = MOE_TM
    grid_spec = pltpu.PrefetchScalarGridSpec(
        num_scalar_prefetch=2,
        grid=(rows_padded // tm,),
        in_specs=[
            pl.BlockSpec((tm, SUBLANES, LANES), lambda i, te, nv: (i, 0, 0)),
            pl.BlockSpec((1, d, f2), lambda i, te, nv: (te[i], 0, 0)),
            pl.BlockSpec((1, 1, f2), lambda i, te, nv: (te[i], 0, 0)),
            pl.BlockSpec((1, f, d), lambda i, te, nv: (te[i], 0, 0)),
            pl.BlockSpec((1, 1, d), lambda i, te, nv: (te[i], 0, 0)),
        ],
        out_specs=pl.BlockSpec((tm, SUBLANES, LANES), lambda i, te, nv: (i, 0, 0)),
        scratch_shapes=[pltpu.VMEM((d, f2), BF16), pltpu.VMEM((f, d), BF16), pltpu.VMEM((tm, d), F32)],
    )
    return pl.pallas_call(
        functools.partial(_expert_kernel, chunk=512),
        grid_spec=grid_spec,
        out_shape=jax.ShapeDtypeStruct(xs.shape, F32),
        compiler_params=_cparams(("arbitrary",)),
        name="moe_experts",
    )(tile_expert, tile_valid, xs, w_gate_up, b_gate_up.reshape(e, 1, f2), w_down, b_down.reshape(e, 1, d))


COMBINE_TOKENS = 128


def _combine_kernel(pos_ref, y_ref, x2_ref, mw_ref, o_ref, ybuf, sem):
    nt = COMBINE_TOKENS

    def issue(t, carry):
        for kk in range(TOP_K):
            pltpu.make_async_copy(y_ref.at[pos_ref[0, t * TOP_K + kk]], ybuf.at[kk, t], sem).start()
        return carry

    lax.fori_loop(0, nt, issue, 0)

    def drain(t, carry):
        for kk in range(TOP_K):
            pltpu.make_async_copy(y_ref.at[pos_ref[0, t * TOP_K + kk]], ybuf.at[kk, t], sem).wait()
        return carry

    lax.fori_loop(0, nt, drain, 0)

    mw = mw_ref[...]
    for cc in range(SUBLANES):
        cs = slice(cc * LANES, (cc + 1) * LANES)
        acc = x2_ref[:, cs]
        for kk in range(TOP_K):
            acc = acc + mw[:, kk:kk + 1] * ybuf[kk, :, cc, :]
        o_ref[:, cs] = acc


def _combine(pos, y, x2, mw):
    n, d = x2.shape
    nt = COMBINE_TOKENS
    return pl.pallas_call(
        _combine_kernel,
        grid=(n // nt,),
        in_specs=[pl.BlockSpec((None, 1, nt * TOP_K), lambda i: (i, 0, 0), memory_space=pltpu.SMEM),
                  pl.BlockSpec(memory_space=pl.ANY),
                  pl.BlockSpec((nt, d), lambda i: (i, 0)),
                  pl.BlockSpec((nt, LANES), lambda i: (i, 0))],
        out_specs=pl.BlockSpec((nt, d), lambda i: (i, 0)),
        out_shape=jax.ShapeDtypeStruct((n, d), F32),
        scratch_shapes=[pltpu.VMEM((TOP_K, nt, SUBLANES, LANES), F32), pltpu.SemaphoreType.DMA],
        compiler_params=_cparams(("arbitrary",)),
        name="moe_combine",
    )(pos, y, x2, mw)


def _layer(x, mem, mix_norm, w_in, b_forget, fox_q_gain, fox_k_gain, fox_out_gain, gdn_conv_w, gdn_a_log,
           gdn_dt_bias, gdn_out_gain, w_out, xattn_norm, mem_norm, w_xq, w_xkv, xq_gain, xk_gain, w_xo,
           moe_norm, w_router, b_router, w_gate_up, b_gate_up, w_down, b_down):
    batch, seq, d = x.shape
    n = batch * seq
    mem_len = mem.shape[1]
    x2d = x.reshape(n, d)

    o_ff = 3 * FOX_WIDTH
    o_gq = o_ff + FOX_HEADS
    o_ga = o_gq + 3 * GDN_WIDTH
    o_gb = o_ga + GDN_HEADS
    o_gz = o_gb + GDN_HEADS
    w_main = jnp.concatenate([w_in[:, :o_ff], w_in[:, o_gq:o_ga], w_in[:, o_gz:]], axis=1).astype(BF16)
    w_small = jnp.concatenate([w_in[:, o_ff:o_gq], w_in[:, o_ga:o_gz],
                               jnp.zeros((d, LANES - SM_END), F32)], axis=1).astype(BF16)
    prow = jnp.zeros((SUBLANES, LANES), F32)
    prow = prow.at[0, SM_F0:SM_G0].set(b_forget).at[0, SM_G0:SM_B0].set(gdn_dt_bias)
    prow = prow.at[1, SM_G0:SM_B0].set(gdn_a_log)
    pcol = prow[0:2].T
    head_id = jnp.arange(FOX_WIDTH) // FOX_HEAD_DIM
    bd = (head_id[:, None] == head_id[None, :]).astype(F32)
    row1 = lambda v: v.reshape(1, -1)

    tm = min(256, seq)
    fq, fk, fv, gqkv, gz, sm, smt = _inproj(
        x2d, row1(mix_norm), w_main, w_small, w_small.T, bd,
        row1(jnp.tile(fox_q_gain, FOX_HEADS)), row1(jnp.tile(fox_k_gain, FOX_HEADS)), prow, pcol,
        seq=seq, tm=tm)

    cum = smt[SM_F0:SM_G0].reshape(FOX_HEADS // 2, 2, batch, seq).transpose(2, 0, 1, 3)
    b3 = lambda a: a.reshape(batch, seq, a.shape[-1])
    o_fox = _fox_attention(b3(fq), b3(fk), b3(fv), cum, fox_out_gain.reshape(FOX_HEADS // 2, 1, LANES),
                           batch=batch, seq=seq, blk=min(512, seq))

    u, w, qd, qk, kdt, egl = _gdn_prep(gqkv, gdn_conv_w.T, sm, smt, batch=batch, seq=seq)
    o_gdn = _gdn_scan(u, w, qd, qk, kdt, egl, b3(gz), row1(gdn_out_gain), batch=batch, seq=seq)

    kx, vx = _memkv(mem.reshape(batch * mem_len, d), row1(mem_norm), w_xkv.astype(BF16), row1(xk_gain),
                    batch=batch, mem_len=mem_len)
    w_r = jnp.concatenate([w_router, jnp.zeros((d, LANES - N_EXPERTS), F32)], axis=1)
    b_r = jnp.concatenate([b_router, jnp.full((LANES - N_EXPERTS,), -jnp.inf, F32)]).reshape(1, LANES)
    x2, h3, mi, mw, cnt = _mid(
        x2d, o_fox.reshape(n, FOX_WIDTH), o_gdn.reshape(n, GDN_WIDTH), w_out.astype(BF16), row1(xattn_norm),
        w_xq.astype(BF16), row1(xq_gain), kx, vx, w_xo.astype(BF16), row1(moe_norm), w_r, b_r,
        seq=seq, mem_len=mem_len, tm=min(256, seq))

    counts = cnt[0, :N_EXPERTS]
    padded = ((counts + MOE_TM - 1) // MOE_TM) * MOE_TM
    ends = jnp.cumsum(padded)
    starts = ends - padded
    eidx = mi[:, :TOP_K]
    pos = starts[eidx] + mi[:, TOP_K:2 * TOP_K]
    rows_padded = n * TOP_K + N_EXPERTS * MOE_TM
    n_tiles = rows_padded // MOE_TM
    tile_start = jnp.arange(n_tiles, dtype=jnp.int32) * MOE_TM
    tile_expert = jnp.minimum(jnp.searchsorted(ends, tile_start, side="right"), N_EXPERTS - 1).astype(jnp.int32)
    tile_valid = jnp.clip(starts[tile_expert] + counts[tile_expert] - tile_start, 0, MOE_TM).astype(jnp.int32)
    last_used = jnp.max(jnp.where(tile_valid > 0, tile_expert, 0))
    tile_expert = jnp.where(tile_valid > 0, tile_expert, last_used).astype(jnp.int32)

    pos_tiles = pos.reshape(n // DISPATCH_TOKENS, 1, DISPATCH_TOKENS * TOP_K).astype(jnp.int32)
    xs = _dispatch(h3.reshape(n, SUBLANES, LANES), pos_tiles, rows_padded=rows_padded)
    y = _experts(tile_expert, tile_valid, xs, w_gate_up, b_gate_up, w_down, b_down)
    out = _combine(pos_tiles, y, x2, mw)
    return out.reshape(batch, seq, d)


def kernel(x, mem, mix_norm, w_in, b_forget, fox_q_gain, fox_k_gain, fox_out_gain, gdn_conv_w, gdn_a_log,
           gdn_dt_bias, gdn_out_gain, w_out, xattn_norm, mem_norm, w_xq, w_xkv, xq_gain, xk_gain, w_xo,
           moe_norm, w_router, b_router, w_gate_up, b_gate_up, w_down, b_down):
    depth = mix_norm.shape[0]
    for l in range(depth):
        x = _layer(x, mem, mix_norm[l], w_in[l], b_forget[l], fox_q_gain[l], fox_k_gain[l], fox_out_gain[l],
                   gdn_conv_w[l], gdn_a_log[l], gdn_dt_bias[l], gdn_out_gain[l], w_out[l], xattn_norm[l],
                   mem_norm[l], w_xq[l], w_xkv[l], xq_gain[l], xk_gain[l], w_xo[l], moe_norm[l], w_router[l],
                   b_router[l], w_gate_up[l], b_gate_up[l], w_down[l], b_down[l])
    return x
```

```python
import functools

import jax
import jax.numpy as jnp
from jax import lax
from jax.experimental import pallas as pl
from jax.experimental.pallas import tpu as pltpu

F32 = jnp.float32
BF16 = jnp.bfloat16
HIGHEST = lax.Precision.HIGHEST

EPS = 1e-6
FOX_HEADS, FOX_HEAD_DIM = 8, 64
GDN_HEADS, GDN_HEAD_DIM = 4, 128
FOX_WIDTH = FOX_HEADS * FOX_HEAD_DIM
GDN_WIDTH = GDN_HEADS * GDN_HEAD_DIM
CONV_WIDTH = 4
CHUNK = 64
XA_HEADS = 4
N_EXPERTS = 32
TOP_K = 4
SWIGLU_LIMIT = 7.0
SWIGLU_ALPHA = 1.702
LOG2E = 1.4426950408889634

LANES = 128
SUBLANES = 8
VMEM_LIMIT = 52 * 1024 * 1024

SM_F0, SM_G0, SM_B0, SM_END = 0, 8, 12, 16

NT_DIMS = (((1,), (1,)), ((), ()))


def _cparams(sem):
    return pltpu.CompilerParams(dimension_semantics=sem, vmem_limit_bytes=VMEM_LIMIT)


def _rms(x, gain):
    return x * lax.rsqrt(jnp.mean(x * x, axis=-1, keepdims=True) + EPS) * gain


def _softplus(t):
    return jnp.maximum(t, 0.0) + jnp.log1p(jnp.exp(-jnp.abs(t)))


def _sigmoid(t):
    return 1.0 / (1.0 + jnp.exp(-t))


def _small_act(v, bias, alog, idx):
    t = v + bias
    tail = jnp.log1p(jnp.exp(-jnp.abs(t)))
    log_f = jnp.minimum(t, 0.0) - tail
    g = -jnp.exp(alog) * (jnp.maximum(t, 0.0) + tail)
    beta = _sigmoid(v)
    return jnp.where(idx < SM_G0, log_f, jnp.where(idx < SM_B0, g, jnp.where(idx < SM_END, beta, 0.0)))


def _inproj_kernel(x_ref, gain_ref, w_ref, ws_ref, wst_ref, bd_ref, qg_ref, kg_ref, prow_ref, pcol_ref,
                   fq_ref, fk_ref, fv_ref, gqkv_ref, gz_ref, sm_ref, smt_ref, carry_r, carry_c,
                   *, tm, tiles_per_seq):
    i = pl.program_id(0)
    hb = _rms(x_ref[...], gain_ref[...]).astype(BF16)

    def proj(lo, hi):
        return jnp.dot(hb, w_ref[:, lo:hi], preferred_element_type=F32)

    def headnorm(p, g):
        ss = jnp.dot(p * p, bd_ref[...], precision=HIGHEST, preferred_element_type=F32)
        return p * lax.rsqrt(ss * (1.0 / FOX_HEAD_DIM) + EPS) * g

    w0 = FOX_WIDTH
    fq_ref[...] = (headnorm(proj(0, w0), qg_ref[...]) * (FOX_HEAD_DIM ** -0.5 * LOG2E)).astype(BF16)
    fk_ref[...] = headnorm(proj(w0, 2 * w0), kg_ref[...]).astype(BF16)
    fv_ref[...] = proj(2 * w0, 3 * w0).astype(BF16)
    g0 = 3 * w0
    gqkv_ref[...] = proj(g0, g0 + 3 * GDN_WIDTH)
    gz_ref[...] = proj(g0 + 3 * GDN_WIDTH, g0 + 4 * GDN_WIDTH)

    @pl.when(i % tiles_per_seq == 0)
    def _():
        carry_r[...] = jnp.zeros_like(carry_r)
        carry_c[...] = jnp.zeros_like(carry_c)

    sm = jnp.dot(hb, ws_ref[...], preferred_element_type=F32)
    smt = lax.dot_general(wst_ref[...], hb, NT_DIMS, preferred_element_type=F32)
    lane = lax.broadcasted_iota(jnp.int32, (1, LANES), 1)
    srow = lax.broadcasted_iota(jnp.int32, (LANES, 1), 0)
    vals = _small_act(sm, prow_ref[0:1, :], prow_ref[1:2, :], lane)
    vals_t = _small_act(smt, pcol_ref[:, 0:1], pcol_ref[:, 1:2], srow)

    r = lax.broadcasted_iota(jnp.int32, (tm, tm), 0)
    c = lax.broadcasted_iota(jnp.int32, (tm, tm), 1)
    cum = jnp.dot((c <= r).astype(F32), vals, precision=HIGHEST, preferred_element_type=F32) + carry_r[...]
    cum_t = jnp.dot(vals_t, (r <= c).astype(F32), precision=HIGHEST, preferred_element_type=F32) + carry_c[...]
    carry_r[...] = cum[tm - 1:tm, :]
    carry_c[...] = cum_t[:, tm - 1:tm]
    sm_ref[...] = jnp.where(lane < SM_G0, cum, vals)
    smt_ref[...] = jnp.where(srow < SM_G0, cum_t, vals_t)


def _inproj(x2d, gain, w_main, w_small, w_small_t, bd, qg, kg, prow, pcol, *, seq, tm):
    n, d = x2d.shape
    wm = w_main.shape[1]
    full = lambda shape: pl.BlockSpec(shape, lambda i: (0,) * len(shape))
    rows = lambda width: pl.BlockSpec((tm, width), lambda i: (i, 0))
    out_shape = (
        jax.ShapeDtypeStruct((n, FOX_WIDTH), BF16),
        jax.ShapeDtypeStruct((n, FOX_WIDTH), BF16),
        jax.ShapeDtypeStruct((n, FOX_WIDTH), BF16),
        jax.ShapeDtypeStruct((n, 3 * GDN_WIDTH), F32),
        jax.ShapeDtypeStruct((n, GDN_WIDTH), F32),
        jax.ShapeDtypeStruct((n, LANES), F32),
        jax.ShapeDtypeStruct((LANES, n), F32),
    )
    return pl.pallas_call(
        functools.partial(_inproj_kernel, tm=tm, tiles_per_seq=seq // tm),
        grid=(n // tm,),
        in_specs=[rows(d), full((1, d)), full((d, wm)), full((d, LANES)), full((LANES, d)),
                  full((FOX_WIDTH, FOX_WIDTH)), full((1, FOX_WIDTH)), full((1, FOX_WIDTH)),
                  full((SUBLANES, LANES)), full((LANES, 2))],
        out_specs=(rows(FOX_WIDTH), rows(FOX_WIDTH), rows(FOX_WIDTH), rows(3 * GDN_WIDTH), rows(GDN_WIDTH),
                   rows(LANES), pl.BlockSpec((LANES, tm), lambda i: (0, i))),
        out_shape=out_shape,
        scratch_shapes=[pltpu.VMEM((1, LANES), F32), pltpu.VMEM((LANES, 1), F32)],
        compiler_params=_cparams(("arbitrary",)),
        name="inproj",
    )(x2d, gain, w_main, w_small, w_small_t, bd, qg, kg, prow, pcol)


def _fox_kernel(q_ref, k_ref, v_ref, cum_ref, gain_ref, o_ref, m0_ref, m1_ref, acc0_ref, acc1_ref, *, blk):
    qi = pl.program_id(2)
    lane = lax.broadcasted_iota(jnp.int32, (1, LANES), 1)
    lo = lane < FOX_HEAD_DIM
    q = q_ref[...]
    zero = jnp.zeros_like(q)
    q_heads = (jnp.where(lo, q, zero), jnp.where(lo, zero, q))
    q0 = pl.multiple_of(qi * blk, blk)
    c_ref = [cum_ref[hh:hh + 1, pl.ds(q0, LANES)][:, 0:1] for hh in range(2)]
    head_lanes = (lo, jnp.logical_not(lo))
    sum_lane = (FOX_HEAD_DIM, 0)
    ones_col = [jnp.where(lane == sum_lane[hh], 1.0, 0.0).astype(BF16) for hh in range(2)]
    m_refs = (m0_ref, m1_ref)
    acc_refs = (acc0_ref, acc1_ref)
    for hh in range(2):
        m_refs[hh][...] = jnp.full_like(m_refs[hh], -1e30)
        acc_refs[hh][...] = jnp.zeros_like(acc_refs[hh])

    def step(k0, masked):
        kb = k_ref[pl.ds(k0, blk), :]
        vb = v_ref[pl.ds(k0, blk), :]
        for hh in range(2):
            s = lax.dot_general(q_heads[hh], kb, NT_DIMS, preferred_element_type=F32)
            s = s + (c_ref[hh] - cum_ref[hh:hh + 1, pl.ds(k0, blk)]) * LOG2E
            if masked:
                r = lax.broadcasted_iota(jnp.int32, (blk, blk), 0)
                c = lax.broadcasted_iota(jnp.int32, (blk, blk), 1)
                s = jnp.where(c <= r, s, -jnp.inf)
            m_old = m_refs[hh][...]
            m_new = jnp.maximum(m_old, jnp.max(s, axis=-1, keepdims=True))
            alpha = jnp.exp2(m_old - m_new)
            p = jnp.exp2(s - jnp.concatenate([m_new] * (blk // LANES), axis=1))
            v_h = jnp.where(head_lanes[hh], vb, ones_col[hh])
            pv = jnp.dot(p.astype(BF16), v_h, preferred_element_type=F32)
            acc_refs[hh][...] = acc_refs[hh][...] * alpha + pv
            m_refs[hh][...] = m_new

    def body(ki, carry):
        step(pl.multiple_of(ki * blk, blk), False)
        return carry

    lax.fori_loop(0, qi, body, 0)
    step(q0, True)

    acc0, acc1 = acc0_ref[...], acc1_ref[...]
    l0 = acc0[:, sum_lane[0]:sum_lane[0] + 1]
    l1 = acc1[:, sum_lane[1]:sum_lane[1] + 1]
    o = jnp.where(lo, acc0 / l0, acc1 / l1)
    o2 = o * o
    ss_lo = jnp.sum(jnp.where(lo, o2, 0.0), axis=-1, keepdims=True)
    ss_hi = jnp.sum(jnp.where(lo, 0.0, o2), axis=-1, keepdims=True)
    ms = jnp.where(lo, ss_lo, ss_hi) * (1.0 / FOX_HEAD_DIM)
    o_ref[...] = (o * lax.rsqrt(ms + EPS) * gain_ref[...]).astype(o_ref.dtype)


def _fox_attention(fq, fk, fv, cum, gain_pairs, *, batch, seq, blk):
    npairs = FOX_HEADS // 2
    return pl.pallas_call(
        functools.partial(_fox_kernel, blk=blk),
        grid=(batch, npairs, seq // blk),
        in_specs=[
            pl.BlockSpec((None, blk, LANES), lambda b, j, i: (b, i, j)),
            pl.BlockSpec((None, seq, LANES), lambda b, j, i: (b, 0, j)),
            pl.BlockSpec((None, seq, LANES), lambda b, j, i: (b, 0, j)),
            pl.BlockSpec((None, None, 2, seq), lambda b, j, i: (b, j, 0, 0)),
            pl.BlockSpec((None, 1, LANES), lambda b, j, i: (j, 0, 0)),
        ],
        out_specs=pl.BlockSpec((None, blk, LANES), lambda b, j, i: (b, i, j)),
        out_shape=jax.ShapeDtypeStruct((batch, seq, FOX_WIDTH), BF16),
        scratch_shapes=[pltpu.VMEM((blk, LANES), F32) for _ in range(4)],
        compiler_params=_cparams(("parallel", "parallel", "arbitrary")),
        name="fox_attention",
    )(fq, fk, fv, cum, gain_pairs)


GDN_BLK = 4 * CHUNK


def _gdn_prep_kernel(x_ref, halo_ref, cw_ref, sm_ref, smt_ref,
                     u_ref, w_ref, qd_ref, qk_ref, kdt_ref, egl_ref, xpad_ref):
    i = pl.program_id(1)
    nb = GDN_BLK
    halo = halo_ref[...]
    xpad_ref[0:SUBLANES, :] = jnp.where(i > 0, halo, jnp.zeros_like(halo))
    xpad_ref[SUBLANES:, :] = x_ref[...]
    y = None
    for j in range(CONV_WIDTH):
        start = SUBLANES - (CONV_WIDTH - 1) + j
        term = cw_ref[j:j + 1, :] * xpad_ref[start:start + nb, :]
        y = term if y is None else y + term
    y = y * _sigmoid(y)

    r = lax.broadcasted_iota(jnp.int32, (nb, nb), 0)
    c = lax.broadcasted_iota(jnp.int32, (nb, nb), 1)
    chunk_shift = CHUNK.bit_length() - 1
    same = jnp.right_shift(r, chunk_shift) == jnp.right_shift(c, chunk_shift)
    incl = same & (c <= r)
    strict = same & (c < r)
    sm = sm_ref[...]
    g_cum = jnp.dot(incl.astype(F32), sm, precision=HIGHEST, preferred_element_type=F32)
    g_tot = jnp.dot(same.astype(F32), sm, precision=HIGHEST, preferred_element_type=F32)
    g_cum_t = jnp.dot(smt_ref[...], (same & (r <= c)).astype(F32), precision=HIGHEST,
                      preferred_element_type=F32)
    eye = (r == c).astype(F32)

    for h in range(GDN_HEADS):
        sl = slice(h * GDN_HEAD_DIM, (h + 1) * GDN_HEAD_DIM)
        q = y[:, h * GDN_HEAD_DIM:(h + 1) * GDN_HEAD_DIM]
        k = y[:, GDN_WIDTH + h * GDN_HEAD_DIM:GDN_WIDTH + (h + 1) * GDN_HEAD_DIM]
        v = y[:, 2 * GDN_WIDTH + h * GDN_HEAD_DIM:2 * GDN_WIDTH + (h + 1) * GDN_HEAD_DIM]
        qn = q * lax.rsqrt(jnp.sum(q * q, axis=-1, keepdims=True) + EPS) * GDN_HEAD_DIM ** -0.5
        kn = k * lax.rsqrt(jnp.sum(k * k, axis=-1, keepdims=True) + EPS)
        gc = g_cum[:, SM_G0 + h:SM_G0 + h + 1]
        gl = g_tot[:, SM_G0 + h:SM_G0 + h + 1]
        gr = g_cum_t[SM_G0 + h:SM_G0 + h + 1, :]
        beta = sm[:, SM_B0 + h:SM_B0 + h + 1]
        decay = jnp.where(incl, jnp.exp(jnp.where(incl, gc - gr, 0.0)), 0.0)
        qb, kb = qn.astype(BF16), kn.astype(BF16)
        kk = lax.dot_general(kb, kb, NT_DIMS, preferred_element_type=F32)
        a = jnp.where(strict, beta * kk * decay, 0.0)
        t_inv = eye - a
        p = a
        for _ in range(5):
            pb = p.astype(BF16)
            p = jnp.dot(pb, pb, preferred_element_type=F32)
            t_inv = t_inv + jnp.dot(t_inv.astype(BF16), p.astype(BF16), preferred_element_type=F32)
        tb = t_inv.astype(BF16)
        eg = jnp.exp(gc)
        u_ref[:, sl] = jnp.dot(tb, (v * beta).astype(BF16), preferred_element_type=F32)
        w_ref[:, sl] = jnp.dot(tb, (kn * (beta * eg)).astype(BF16), preferred_element_type=F32).astype(BF16)
        qk = lax.dot_general(qb, kb, NT_DIMS, preferred_element_type=F32)
        qk_ref[:, h * nb:(h + 1) * nb] = jnp.where(incl, qk * decay, 0.0).astype(BF16)
        qd_ref[:, sl] = (qn * eg).astype(BF16)
        kdt_ref[sl, :] = (kn * jnp.exp(gl - gc)).T.astype(BF16)
        egl_ref[:, sl] = jnp.broadcast_to(jnp.exp(gl), (nb, GDN_HEAD_DIM))


def _gdn_prep(gqkv, conv_w_t, sm, smt, *, batch, seq):
    nb = GDN_BLK
    bps = seq // nb
    hps = nb // SUBLANES
    width = 3 * GDN_WIDTH
    row = lambda w: pl.BlockSpec((None, nb, w), lambda b, i: (b, i, 0))
    return pl.pallas_call(
        _gdn_prep_kernel,
        grid=(batch, bps),
        in_specs=[
            pl.BlockSpec((nb, width), lambda b, i: (b * bps + i, 0)),
            pl.BlockSpec((SUBLANES, width), lambda b, i: (jnp.maximum((b * bps + i) * hps - 1, 0), 0)),
            pl.BlockSpec((CONV_WIDTH, width), lambda b, i: (0, 0)),
            pl.BlockSpec((nb, LANES), lambda b, i: (b * bps + i, 0)),
            pl.BlockSpec((LANES, nb), lambda b, i: (0, b * bps + i)),
        ],
        out_specs=(row(GDN_WIDTH), row(GDN_WIDTH), row(GDN_WIDTH), row(GDN_HEADS * nb),
                   pl.BlockSpec((None, GDN_WIDTH, nb), lambda b, i: (b, 0, i)), row(GDN_WIDTH)),
        out_shape=(
            jax.ShapeDtypeStruct((batch, seq, GDN_WIDTH), F32),
            jax.ShapeDtypeStruct((batch, seq, GDN_WIDTH), BF16),
            jax.ShapeDtypeStruct((batch, seq, GDN_WIDTH), BF16),
            jax.ShapeDtypeStruct((batch, seq, GDN_HEADS * nb), BF16),
            jax.ShapeDtypeStruct((batch, GDN_WIDTH, seq), BF16),
            jax.ShapeDtypeStruct((batch, seq, GDN_WIDTH), F32),
        ),
        scratch_shapes=[pltpu.VMEM((nb + SUBLANES, width), F32)],
        compiler_params=_cparams(("parallel", "parallel")),
        name="gdn_prep",
    )(gqkv, gqkv, conv_w_t, sm, smt)


def _gdn_scan_kernel(u_ref, w_ref, qd_ref, qk_ref, kdt_ref, egl_ref, z_ref, gain_ref, o_ref, s_ref, vz_ref,
                     *, batch):
    nb = GDN_BLK

    @pl.when(pl.program_id(0) == 0)
    def _():
        s_ref[...] = jnp.zeros_like(s_ref)

    vz_ref[...] = jnp.zeros_like(vz_ref)
    for cidx in range(nb // CHUNK):
        rows = slice(cidx * CHUNK, (cidx + 1) * CHUNK)
        for b in range(batch):
            for h in range(GDN_HEADS):
                bh = b * GDN_HEADS + h
                sl = slice(h * GDN_HEAD_DIM, (h + 1) * GDN_HEAD_DIM)
                s_old = s_ref[bh]
                lhs1 = jnp.concatenate([w_ref[b, rows, sl], qd_ref[b, rows, sl]], axis=0)
                r1 = jnp.dot(lhs1, s_old.astype(BF16), preferred_element_type=F32)
                v_new = u_ref[b, rows, sl] - r1[:CHUNK]
                vz_ref[bh, rows, :] = v_new.astype(BF16)
                lhs2 = jnp.concatenate([qk_ref[b, rows, h * nb:(h + 1) * nb], kdt_ref[b, sl, :]], axis=0)
                r2 = jnp.dot(lhs2, vz_ref[bh], preferred_element_type=F32)
                vz_ref[bh, rows, :] = jnp.zeros((CHUNK, GDN_HEAD_DIM), BF16)
                last = egl_ref[b, (cidx + 1) * CHUNK - 1:(cidx + 1) * CHUNK, sl]
                s_ref[bh] = s_old * last + r2[CHUNK:]
                o = r1[CHUNK:] + r2[:CHUNK]
                z = z_ref[b, rows, sl]
                o_ref[b, rows, sl] = (_rms(o, gain_ref[...]) * (z * _sigmoid(z))).astype(o_ref.dtype)


def _gdn_scan(u, w, qd, qk, kdt, egl, z, gain, *, batch, seq):
    nb = GDN_BLK
    row = lambda width: pl.BlockSpec((batch, nb, width), lambda i: (0, i, 0))
    return pl.pallas_call(
        functools.partial(_gdn_scan_kernel, batch=batch),
        grid=(seq // nb,),
        in_specs=[row(GDN_WIDTH), row(GDN_WIDTH), row(GDN_WIDTH), row(GDN_HEADS * nb),
                  pl.BlockSpec((batch, GDN_WIDTH, nb), lambda i: (0, 0, i)), row(GDN_WIDTH), row(GDN_WIDTH),
                  pl.BlockSpec((1, GDN_HEAD_DIM), lambda i: (0, 0))],
        out_specs=row(GDN_WIDTH),
        out_shape=jax.ShapeDtypeStruct((batch, seq, GDN_WIDTH), BF16),
        scratch_shapes=[pltpu.VMEM((batch * GDN_HEADS, GDN_HEAD_DIM, GDN_HEAD_DIM), F32),
                        pltpu.VMEM((batch * GDN_HEADS, nb, GDN_HEAD_DIM), BF16)],
        compiler_params=_cparams(("arbitrary",)),
        name="gdn_scan",
    )(u, w, qd, qk, kdt, egl, z, gain)


def _memkv_kernel(m_ref, gain_ref, w_ref, kg_ref, k_ref, v_ref):
    d = m_ref.shape[-1]
    hd = d // XA_HEADS
    mb = _rms(m_ref[...], gain_ref[...]).astype(BF16)
    kv = jnp.dot(mb, w_ref[...], preferred_element_type=F32)
    for h in range(XA_HEADS):
        sl = slice(h * hd, (h + 1) * hd)
        k_ref[:, sl] = _rms(kv[:, sl], kg_ref[...]).astype(BF16)
    v_ref[...] = kv[:, d:].astype(BF16)


def _memkv(mem2d, gain, w_xkv, xk_gain, *, batch, mem_len):
    d = mem2d.shape[-1]
    full = lambda shape: pl.BlockSpec(shape, lambda b: (0,) * len(shape))
    row = pl.BlockSpec((mem_len, d), lambda b: (b, 0))
    return pl.pallas_call(
        _memkv_kernel,
        grid=(batch,),
        in_specs=[row, full((1, d)), full((d, 2 * d)), full((1, d // XA_HEADS))],
        out_specs=(row, row),
        out_shape=(jax.ShapeDtypeStruct(mem2d.shape, BF16), jax.ShapeDtypeStruct(mem2d.shape, BF16)),
        compiler_params=_cparams(("parallel",)),
        name="mem_kv",
    )(mem2d, gain, w_xkv, xk_gain)


def _mid_kernel(x_ref, of_ref, og_ref, wo_ref, xg_ref, wq_ref, qg_ref, k_ref, v_ref, wxo_ref, mg_ref,
                wr_ref, br_ref,
                x2_ref, h3_ref, mi_ref, mw_ref, cnt_ref, carry_ref, *, tm):
    i = pl.program_id(0)
    d = x_ref.shape[-1]
    hd = d // XA_HEADS
    x1 = (x_ref[...]
          + jnp.dot(of_ref[...], wo_ref[0:FOX_WIDTH, :], preferred_element_type=F32)
          + jnp.dot(og_ref[...], wo_ref[FOX_WIDTH:, :], preferred_element_type=F32))
    h2 = _rms(x1, xg_ref[...]).astype(BF16)
    q = jnp.dot(h2, wq_ref[...], preferred_element_type=F32)
    heads = []
    for h in range(XA_HEADS):
        sl = slice(h * hd, (h + 1) * hd)
        qn = (_rms(q[:, sl], qg_ref[...]) * hd ** -0.5).astype(BF16)
        s = lax.dot_general(qn, k_ref[:, sl], NT_DIMS, preferred_element_type=F32)
        p = jnp.exp(s - jnp.max(s, axis=-1, keepdims=True))
        p = p / jnp.sum(p, axis=-1, keepdims=True)
        heads.append(jnp.dot(p.astype(BF16), v_ref[:, sl], preferred_element_type=F32).astype(BF16))
    x2 = x1 + jnp.dot(jnp.concatenate(heads, axis=-1), wxo_ref[...], preferred_element_type=F32)
    x2_ref[...] = x2
    h3 = _rms(x2, mg_ref[...])
    h3_ref[...] = h3
    logits = jnp.dot(h3, wr_ref[...], precision=HIGHEST, preferred_element_type=F32) + br_ref[...]
    lane = lax.broadcasted_iota(jnp.int32, (tm, LANES), 1)
    work = logits
    vals, idxs = [], []
    onehot = jnp.zeros((tm, LANES), F32)
    for _ in range(TOP_K):
        mx = jnp.max(work, axis=-1, keepdims=True)
        idx = jnp.min(jnp.where(work == mx, lane, LANES), axis=-1, keepdims=True)
        sel = lane == idx
        onehot = jnp.where(sel, 1.0, onehot)
        work = jnp.where(sel, -jnp.inf, work)
        vals.append(mx)
        idxs.append(idx)
    es = [jnp.exp(v - vals[0]) for v in vals]
    denom = es[0] + es[1] + es[2] + es[3]

    @pl.when(i == 0)
    def _():
        carry_ref[...] = jnp.zeros_like(carry_ref)

    r = lax.broadcasted_iota(jnp.int32, (tm, tm), 0)
    c = lax.broadcasted_iota(jnp.int32, (tm, tm), 1)
    before = jnp.dot((c < r).astype(BF16), onehot.astype(BF16), preferred_element_type=F32) + carry_ref[...]
    mi = jnp.zeros((tm, LANES), jnp.int32)
    mw = jnp.zeros((tm, LANES), F32)
    for kk in range(TOP_K):
        rank = jnp.sum(jnp.where(lane == idxs[kk], before, 0.0), axis=-1, keepdims=True).astype(jnp.int32)
        mi = jnp.where(lane == kk, idxs[kk], mi)
        mi = jnp.where(lane == TOP_K + kk, rank, mi)
        mw = jnp.where(lane == kk, es[kk] / denom, mw)
    mi_ref[...] = mi
    mw_ref[...] = mw
    total = carry_ref[...] + jnp.sum(onehot, axis=0, keepdims=True)
    carry_ref[...] = total
    cnt_ref[...] = jnp.broadcast_to(total, cnt_ref.shape).astype(jnp.int32)


def _mid(x2d, o_fox, o_gdn, w_out, xg, w_xq, xq_gain, kx, vx, w_xo, mg, w_r, b_r, *, seq, mem_len, tm):
    n, d = x2d.shape
    full = lambda shape: pl.BlockSpec(shape, lambda i: (0,) * len(shape))
    rows = lambda width: pl.BlockSpec((tm, width), lambda i: (i, 0))
    mem = pl.BlockSpec((mem_len, d), lambda i: (i // (seq // tm), 0))
    return pl.pallas_call(
        functools.partial(_mid_kernel, tm=tm),
        grid=(n // tm,),
        in_specs=[rows(d), rows(FOX_WIDTH), rows(GDN_WIDTH), full((d, d)), full((1, d)), full((d, d)),
                  full((1, d // XA_HEADS)), mem, mem, full((d, d)), full((1, d)), full((d, LANES)),
                  full((1, LANES))],
        out_specs=(rows(d), rows(d), rows(LANES), rows(LANES), full((SUBLANES, LANES))),
        out_shape=(jax.ShapeDtypeStruct((n, d), F32), jax.ShapeDtypeStruct((n, d), F32),
                   jax.ShapeDtypeStruct((n, LANES), jnp.int32), jax.ShapeDtypeStruct((n, LANES), F32),
                   jax.ShapeDtypeStruct((SUBLANES, LANES), jnp.int32)),
        scratch_shapes=[pltpu.VMEM((1, LANES), F32)],
        compiler_params=_cparams(("arbitrary",)),
        name="outproj_xattn_router",
    )(x2d, o_fox, o_gdn, w_out, xg, w_xq, xq_gain, kx, vx, w_xo, mg, w_r, b_r)


MOE_TM = 512
DISPATCH_TOKENS = 128


def _dispatch_kernel(pos_ref, src_ref, dst_ref, sem):
    nt = DISPATCH_TOKENS

    def issue(t, carry):
        for kk in range(TOP_K):
            pltpu.make_async_copy(src_ref.at[t], dst_ref.at[pos_ref[0, t * TOP_K + kk]], sem).start()
        return carry

    lax.fori_loop(0, nt, issue, 0)

    def drain(t, carry):
        for kk in range(TOP_K):
            pltpu.make_async_copy(src_ref.at[t], dst_ref.at[pos_ref[0, t * TOP_K + kk]], sem).wait()
        return carry

    lax.fori_loop(0, nt, drain, 0)


def _dispatch(h3_tiles, pos, *, rows_padded):
    n = h3_tiles.shape[0]
    nt = DISPATCH_TOKENS
    return pl.pallas_call(
        _dispatch_kernel,
        grid=(n // nt,),
        in_specs=[pl.BlockSpec((None, 1, nt * TOP_K), lambda i: (i, 0, 0), memory_space=pltpu.SMEM),
                  pl.BlockSpec((nt, SUBLANES, LANES), lambda i: (i, 0, 0))],
        out_specs=pl.BlockSpec(memory_space=pl.ANY),
        out_shape=jax.ShapeDtypeStruct((rows_padded,) + h3_tiles.shape[1:], F32),
        scratch_shapes=[pltpu.SemaphoreType.DMA],
        compiler_params=pltpu.CompilerParams(dimension_semantics=("arbitrary",), has_side_effects=True),
        name="moe_dispatch",
    )(pos, h3_tiles)


def _expert_kernel(te_ref, nv_ref, xs_ref, wgu_ref, bgu_ref, wd_ref, bd_ref, y_ref, wgu_bf, wd_bf, acc_ref,
                   *, chunk):
    i = pl.program_id(0)
    tm = xs_ref.shape[0]
    d = wd_bf.shape[1]
    f = wd_bf.shape[0]
    nvalid = nv_ref[i]
    first = jnp.logical_or(i == 0, te_ref[i] != te_ref[jnp.maximum(i - 1, 0)])

    @pl.when(jnp.logical_and(first, nvalid > 0))
    def _():
        wgu_bf[...] = wgu_ref[0].astype(BF16)
        wd_bf[...] = wd_ref[0].astype(BF16)

    @pl.when(nvalid > 0)
    def _():
        row = lax.broadcasted_iota(jnp.int32, (tm, 1), 0)
        x = jnp.concatenate([xs_ref[:, cc, :] for cc in range(SUBLANES)], axis=-1)
        x = jnp.where(row < nvalid, x, 0.0).astype(BF16)
        for j in range(f // chunk):
            cs = slice(j * chunk, (j + 1) * chunk)
            us = slice(f + j * chunk, f + (j + 1) * chunk)
            g = jnp.dot(x, wgu_bf[:, cs], preferred_element_type=F32) + bgu_ref[0, :, cs]
            u = jnp.dot(x, wgu_bf[:, us], preferred_element_type=F32) + bgu_ref[0, :, us]
            gate = jnp.minimum(g, SWIGLU_LIMIT)
            up = jnp.clip(u, -SWIGLU_LIMIT, SWIGLU_LIMIT)
            act = ((up + 1.0) * (gate * _sigmoid(SWIGLU_ALPHA * gate))).astype(BF16)
            part = jnp.dot(act, wd_bf[cs, :], preferred_element_type=F32)
            if j == 0:
                acc_ref[...] = part + bd_ref[0]
            else:
                acc_ref[...] += part
        for cc in range(SUBLANES):
            y_ref[:, cc, :] = acc_ref[:, cc * LANES:(cc + 1) * LANES]

    @pl.when(nvalid <= 0)
    def _():
        y_ref[...] = jnp.zeros_like(y_ref)


def _experts(tile_expert, tile_valid, xs, w_gate_up, b_gate_up, w_down, b_down):
    rows_padded = xs.shape[0]
    e, d, f2 = w_gate_up.shape
    f = f2 // 2
    tm = MOE_TM
    grid_spec = pltpu.PrefetchScalarGridSpec(
        num_scalar_prefetch=2,
        grid=(rows_padded // tm,),
        in_specs=[
            pl.BlockSpec((tm, SUBLANES, LANES), lambda i, te, nv: (i, 0, 0)),
            pl.BlockSpec((1, d, f2), lambda i, te, nv: (te[i], 0, 0)),
            pl.BlockSpec((1, 1, f2), lambda i, te, nv: (te[i], 0, 0)),
            pl.BlockSpec((1, f, d), lambda i, te, nv: (te[i], 0, 0)),
            pl.BlockSpec((1, 1, d), lambda i, te, nv: (te[i], 0, 0)),
        ],
        out_specs=pl.BlockSpec((tm, SUBLANES, LANES), lambda i, te, nv: (i, 0, 0)),
        scratch_shapes=[pltpu.VMEM((d, f2), BF16), pltpu.VMEM((f, d), BF16), pltpu.VMEM((tm, d), F32)],
    )
    return pl.pallas_call(
        functools.partial(_expert_kernel, chunk=512),
        grid_spec=grid_spec,
        out_shape=jax.ShapeDtypeStruct(xs.shape, F32),
        compiler_params=_cparams(("arbitrary",)),
        name="moe_experts",
    )(tile_expert, tile_valid, xs, w_gate_up, b_gate_up.reshape(e, 1, f2), w_down, b_down.reshape(e, 1, d))


COMBINE_TOKENS = 128


def _combine_kernel(pos_ref, y_ref, x2_ref, mw_ref, o_ref, ybuf, sem):
    nt = COMBINE_TOKENS

    def issue(t, carry):
        for kk in range(TOP_K):
            pltpu.make_async_copy(y_ref.at[pos_ref[0, t * TOP_K + kk]], ybuf.at[kk, t], sem).start()
        return carry

    lax.fori_loop(0, nt, issue, 0)

    def drain(t, carry):
        for kk in range(TOP_K):
            pltpu.make_async_copy(y_ref.at[pos_ref[0, t * TOP_K + kk]], ybuf.at[kk, t], sem).wait()
        return carry

    lax.fori_loop(0, nt, drain, 0)

    mw = mw_ref[...]
    for cc in range(SUBLANES):
        cs = slice(cc * LANES, (cc + 1) * LANES)
        acc = x2_ref[:, cs]
        for kk in range(TOP_K):
            acc = acc + mw[:, kk:kk + 1] * ybuf[kk, :, cc, :]
        o_ref[:, cs] = acc


def _combine(pos, y, x2, mw):
    n, d = x2.shape
    nt = COMBINE_TOKENS
    return pl.pallas_call(
        _combine_kernel,
        grid=(n // nt,),
        in_specs=[pl.BlockSpec((None, 1, nt * TOP_K), lambda i: (i, 0, 0), memory_space=pltpu.SMEM),
                  pl.BlockSpec(memory_space=pl.ANY),
                  pl.BlockSpec((nt, d), lambda i: (i, 0)),
                  pl.BlockSpec((nt, LANES), lambda i: (i, 0))],
        out_specs=pl.BlockSpec((nt, d), lambda i: (i, 0)),
        out_shape=jax.ShapeDtypeStruct((n, d), F32),
        scratch_shapes=[pltpu.VMEM((TOP_K, nt, SUBLANES, LANES), F32), pltpu.SemaphoreType.DMA],
        compiler_params=_cparams(("arbitrary",)),
        name="moe_combine",
    )(pos, y, x2, mw)


def _layer(x, mem, mix_norm, w_in, b_forget, fox_q_gain, fox_k_gain, fox_out_gain, gdn_conv_w, gdn_a_log,
           gdn_dt_bias, gdn_out_gain, w_out, xattn_norm, mem_norm, w_xq, w_xkv, xq_gain, xk_gain, w_xo,
           moe_norm, w_router, b_router, w_gate_up, b_gate_up, w_down, b_down):
    batch, seq, d = x.shape
    n = batch * seq
    mem_len = mem.shape[1]
    x2d = x.reshape(n, d)

    o_ff = 3 * FOX_WIDTH
    o_gq = o_ff + FOX_HEADS
    o_ga = o_gq + 3 * GDN_WIDTH
    o_gb = o_ga + GDN_HEADS
    o_gz = o_gb + GDN_HEADS
    w_main = jnp.concatenate([w_in[:, :o_ff], w_in[:, o_gq:o_ga], w_in[:, o_gz:]], axis=1).astype(BF16)
    w_small = jnp.concatenate([w_in[:, o_ff:o_gq], w_in[:, o_ga:o_gz],
                               jnp.zeros((d, LANES - SM_END), F32)], axis=1).astype(BF16)
    prow = jnp.zeros((SUBLANES, LANES), F32)
    prow = prow.at[0, SM_F0:SM_G0].set(b_forget).at[0, SM_G0:SM_B0].set(gdn_dt_bias)
    prow = prow.at[1, SM_G0:SM_B0].set(gdn_a_log)
    pcol = prow[0:2].T
    head_id = jnp.arange(FOX_WIDTH) // FOX_HEAD_DIM
    bd = (head_id[:, None] == head_id[None, :]).astype(F32)
    row1 = lambda v: v.reshape(1, -1)

    tm = min(256, seq)
    fq, fk, fv, gqkv, gz, sm, smt = _inproj(
        x2d, row1(mix_norm), w_main, w_small, w_small.T, bd,
        row1(jnp.tile(fox_q_gain, FOX_HEADS)), row1(jnp.tile(fox_k_gain, FOX_HEADS)), prow, pcol,
        seq=seq, tm=tm)

    cum = smt[SM_F0:SM_G0].reshape(FOX_HEADS // 2, 2, batch, seq).transpose(2, 0, 1, 3)
    b3 = lambda a: a.reshape(batch, seq, a.shape[-1])
    o_fox = _fox_attention(b3(fq), b3(fk), b3(fv), cum, fox_out_gain.reshape(FOX_HEADS // 2, 1, LANES),
                           batch=batch, seq=seq, blk=min(512, seq))

    u, w, qd, qk, kdt, egl = _gdn_prep(gqkv, gdn_conv_w.T, sm, smt, batch=batch, seq=seq)
    o_gdn = _gdn_scan(u, w, qd, qk, kdt, egl, b3(gz), row1(gdn_out_gain), batch=batch, seq=seq)

    kx, vx = _memkv(mem.reshape(batch * mem_len, d), row1(mem_norm), w_xkv.astype(BF16), row1(xk_gain),
                    batch=batch, mem_len=mem_len)
    w_r = jnp.concatenate([w_router, jnp.zeros((d, LANES - N_EXPERTS), F32)], axis=1)
    b_r = jnp.concatenate([b_router, jnp.full((LANES - N_EXPERTS,), -jnp.inf, F32)]).reshape(1, LANES)
    x2, h3, mi, mw, cnt = _mid(
        x2d, o_fox.reshape(n, FOX_WIDTH), o_gdn.reshape(n, GDN_WIDTH), w_out.astype(BF16), row1(xattn_norm),
        w_xq.astype(BF16), row1(xq_gain), kx, vx, w_xo.astype(BF16), row1(moe_norm), w_r, b_r,
        seq=seq, mem_len=mem_len, tm=min(256, seq))

    counts = cnt[0, :N_EXPERTS]
    padded = ((counts + MOE_TM - 1) // MOE_TM) * MOE_TM
    ends = jnp.cumsum(padded)
    starts = ends - padded
    eidx = mi[:, :TOP_K]
    pos = starts[eidx] + mi[:, TOP_K:2 * TOP_K]
    rows_padded = n * TOP_K + N_EXPERTS * MOE_TM
    n_tiles = rows_padded // MOE_TM
    tile_start = jnp.arange(n_tiles, dtype=jnp.int32) * MOE_TM
    tile_expert = jnp.sum((ends[None, :] <= tile_start[:, None]).astype(jnp.int32), axis=1)
    tile_expert = jnp.minimum(tile_expert, N_EXPERTS - 1)
    tile_valid = jnp.clip(starts[tile_expert] + counts[tile_expert] - tile_start, 0, MOE_TM).astype(jnp.int32)
    last_used = jnp.max(jnp.where(tile_valid > 0, tile_expert, 0))
    tile_expert = jnp.where(tile_valid > 0, tile_expert, last_used).astype(jnp.int32)

    pos_tiles = pos.reshape(n // DISPATCH_TOKENS, 1, DISPATCH_TOKENS * TOP_K).astype(jnp.int32)
    xs = _dispatch(h3.reshape(n, SUBLANES, LANES), pos_tiles, rows_padded=rows_padded)
    y = _experts(tile_expert, tile_valid, xs, w_gate_up, b_gate_up, w_down, b_down)
    out = _combine(pos_tiles, y, x2, mw)
    return out.reshape(batch, seq, d)


def kernel(x, mem, mix_norm, w_in, b_forget, fox_q_gain, fox_k_gain, fox_out_gain, gdn_conv_w, gdn_a_log,
           gdn_dt_bias, gdn_out_gain, w_out, xattn_norm, mem_norm, w_xq, w_xkv, xq_gain, xk_gain, w_xo,
           moe_norm, w_router, b_router, w_gate_up, b_gate_up, w_down, b_down):
    depth = mix_norm.shape[0]
    for l in range(depth):
        x = _layer(x, mem, mix_norm[l], w_in[l], b_forget[l], fox_q_gain[l], fox_k_gain[l], fox_out_gain[l],
                   gdn_conv_w[l], gdn_a_log[l], gdn_dt_bias[l], gdn_out_gain[l], w_out[l], xattn_norm[l],
                   mem_norm[l], w_xq[l], w_xkv[l], xq_gain[l], xk_gain[l], w_xo[l], moe_norm[l], w_router[l],
                   b_router[l], w_gate_up[l], b_gate_up[l], w_down[l], b_down[l])
    return x
```

```python
import functools

import jax
import jax.numpy as jnp
from jax import lax
from jax.experimental import pallas as pl
from jax.experimental.pallas import tpu as pltpu

F32 = jnp.float32
BF16 = jnp.bfloat16
HIGHEST = lax.Precision.HIGHEST

EPS = 1e-6
FOX_HEADS, FOX_HEAD_DIM = 8, 64
GDN_HEADS, GDN_HEAD_DIM = 4, 128
FOX_WIDTH = FOX_HEADS * FOX_HEAD_DIM
GDN_WIDTH = GDN_HEADS * GDN_HEAD_DIM
CONV_WIDTH = 4
CHUNK = 64
XA_HEADS = 4
N_EXPERTS = 32
TOP_K = 4
SWIGLU_LIMIT = 7.0
SWIGLU_ALPHA = 1.702
LOG2E = 1.4426950408889634

LANES = 128
SUBLANES = 8
VMEM_LIMIT = 52 * 1024 * 1024

SM_F0, SM_G0, SM_B0, SM_END = 0, 8, 12, 16

NT_DIMS = (((1,), (1,)), ((), ()))


def _cparams(sem):
    return pltpu.CompilerParams(dimension_semantics=sem, vmem_limit_bytes=VMEM_LIMIT)


def _rms(x, gain):
    return x * lax.rsqrt(jnp.mean(x * x, axis=-1, keepdims=True) + EPS) * gain


def _split_bf16(x):
    hi = x.astype(BF16)
    return hi, (x - hi.astype(F32)).astype(BF16)


def _dot_mask_left(mask, x):
    hi, lo = _split_bf16(x)
    mb = mask.astype(BF16)
    return jnp.dot(mb, hi, preferred_element_type=F32) + jnp.dot(mb, lo, preferred_element_type=F32)


def _dot_mask_right(x, mask):
    hi, lo = _split_bf16(x)
    mb = mask.astype(BF16)
    return jnp.dot(hi, mb, preferred_element_type=F32) + jnp.dot(lo, mb, preferred_element_type=F32)


def _softplus(t):
    return jnp.maximum(t, 0.0) + jnp.log1p(jnp.exp(-jnp.abs(t)))


def _sigmoid(t):
    return 1.0 / (1.0 + jnp.exp(-t))


def _small_act(v, bias, alog, idx):
    t = v + bias
    tail = jnp.log1p(jnp.exp(-jnp.abs(t)))
    log_f = jnp.minimum(t, 0.0) - tail
    g = -jnp.exp(alog) * (jnp.maximum(t, 0.0) + tail)
    beta = _sigmoid(v)
    return jnp.where(idx < SM_G0, log_f, jnp.where(idx < SM_B0, g, jnp.where(idx < SM_END, beta, 0.0)))


def _inproj_kernel(x_ref, gain_ref, w_ref, ws_ref, wst_ref, bd_ref, qg_ref, kg_ref, prow_ref, pcol_ref,
                   fq_ref, fk_ref, fv_ref, gqkv_ref, gz_ref, sm_ref, smt_ref, carry_r, carry_c,
                   *, tm, tiles_per_seq):
    i = pl.program_id(0)
    hb = _rms(x_ref[...], gain_ref[...]).astype(BF16)

    def proj(lo, hi):
        return jnp.dot(hb, w_ref[:, lo:hi], preferred_element_type=F32)

    def headnorm(p, g):
        ss = _dot_mask_right(p * p, bd_ref[...])
        return p * lax.rsqrt(ss * (1.0 / FOX_HEAD_DIM) + EPS) * g

    w0 = FOX_WIDTH
    fq_ref[...] = (headnorm(proj(0, w0), qg_ref[...]) * (FOX_HEAD_DIM ** -0.5 * LOG2E)).astype(BF16)
    fk_ref[...] = headnorm(proj(w0, 2 * w0), kg_ref[...]).astype(BF16)
    fv_ref[...] = proj(2 * w0, 3 * w0).astype(BF16)
    g0 = 3 * w0
    gqkv_ref[...] = proj(g0, g0 + 3 * GDN_WIDTH)
    gz_ref[...] = proj(g0 + 3 * GDN_WIDTH, g0 + 4 * GDN_WIDTH)

    @pl.when(i % tiles_per_seq == 0)
    def _():
        carry_r[...] = jnp.zeros_like(carry_r)
        carry_c[...] = jnp.zeros_like(carry_c)

    sm = jnp.dot(hb, ws_ref[...], preferred_element_type=F32)
    smt = lax.dot_general(wst_ref[...], hb, NT_DIMS, preferred_element_type=F32)
    lane = lax.broadcasted_iota(jnp.int32, (1, LANES), 1)
    srow = lax.broadcasted_iota(jnp.int32, (LANES, 1), 0)
    vals = _small_act(sm, prow_ref[0:1, :], prow_ref[1:2, :], lane)
    vals_t = _small_act(smt, pcol_ref[:, 0:1], pcol_ref[:, 1:2], srow)

    r = lax.broadcasted_iota(jnp.int32, (tm, tm), 0)
    c = lax.broadcasted_iota(jnp.int32, (tm, tm), 1)
    cum = _dot_mask_left(c <= r, vals) + carry_r[...]
    cum_t = _dot_mask_right(vals_t, r <= c) + carry_c[...]
    carry_r[...] = cum[tm - 1:tm, :]
    carry_c[...] = cum_t[:, tm - 1:tm]
    sm_ref[...] = jnp.where(lane < SM_G0, cum, vals)
    smt_ref[...] = jnp.where(srow < SM_G0, cum_t, vals_t)


def _inproj(x2d, gain, w_main, w_small, w_small_t, bd, qg, kg, prow, pcol, *, seq, tm):
    n, d = x2d.shape
    wm = w_main.shape[1]
    full = lambda shape: pl.BlockSpec(shape, lambda i: (0,) * len(shape))
    rows = lambda width: pl.BlockSpec((tm, width), lambda i: (i, 0))
    out_shape = (
        jax.ShapeDtypeStruct((n, FOX_WIDTH), BF16),
        jax.ShapeDtypeStruct((n, FOX_WIDTH), BF16),
        jax.ShapeDtypeStruct((n, FOX_WIDTH), BF16),
        jax.ShapeDtypeStruct((n, 3 * GDN_WIDTH), F32),
        jax.ShapeDtypeStruct((n, GDN_WIDTH), F32),
        jax.ShapeDtypeStruct((n, LANES), F32),
        jax.ShapeDtypeStruct((LANES, n), F32),
    )
    return pl.pallas_call(
        functools.partial(_inproj_kernel, tm=tm, tiles_per_seq=seq // tm),
        grid=(n // tm,),
        in_specs=[rows(d), full((1, d)), full((d, wm)), full((d, LANES)), full((LANES, d)),
                  full((FOX_WIDTH, FOX_WIDTH)), full((1, FOX_WIDTH)), full((1, FOX_WIDTH)),
                  full((SUBLANES, LANES)), full((LANES, 2))],
        out_specs=(rows(FOX_WIDTH), rows(FOX_WIDTH), rows(FOX_WIDTH), rows(3 * GDN_WIDTH), rows(GDN_WIDTH),
                   rows(LANES), pl.BlockSpec((LANES, tm), lambda i: (0, i))),
        out_shape=out_shape,
        scratch_shapes=[pltpu.VMEM((1, LANES), F32), pltpu.VMEM((LANES, 1), F32)],
        compiler_params=_cparams(("arbitrary",)),
        name="inproj",
    )(x2d, gain, w_main, w_small, w_small_t, bd, qg, kg, prow, pcol)


def _fox_kernel(q_ref, k_ref, v_ref, cum_ref, gain_ref, o_ref, m0_ref, m1_ref, acc0_ref, acc1_ref, *, blk):
    qi = pl.program_id(2)
    lane = lax.broadcasted_iota(jnp.int32, (1, LANES), 1)
    lo = lane < FOX_HEAD_DIM
    q = q_ref[...]
    zero = jnp.zeros_like(q)
    q_heads = (jnp.where(lo, q, zero), jnp.where(lo, zero, q))
    q0 = pl.multiple_of(qi * blk, blk)
    c_ref = [cum_ref[hh:hh + 1, pl.ds(q0, LANES)][:, 0:1] for hh in range(2)]
    head_lanes = (lo, jnp.logical_not(lo))
    sum_lane = (FOX_HEAD_DIM, 0)
    ones_col = [jnp.where(lane == sum_lane[hh], 1.0, 0.0).astype(BF16) for hh in range(2)]
    m_refs = (m0_ref, m1_ref)
    acc_refs = (acc0_ref, acc1_ref)
    for hh in range(2):
        m_refs[hh][...] = jnp.full_like(m_refs[hh], -1e30)
        acc_refs[hh][...] = jnp.zeros_like(acc_refs[hh])

    def step(k0, masked):
        kb = k_ref[pl.ds(k0, blk), :]
        vb = v_ref[pl.ds(k0, blk), :]
        scores = [lax.dot_general(q_heads[hh], kb, NT_DIMS, preferred_element_type=F32) for hh in range(2)]
        for hh in range(2):
            s = scores[hh] + (c_ref[hh] - cum_ref[hh:hh + 1, pl.ds(k0, blk)]) * LOG2E
            if masked:
                r = lax.broadcasted_iota(jnp.int32, (blk, blk), 0)
                c = lax.broadcasted_iota(jnp.int32, (blk, blk), 1)
                s = jnp.where(c <= r, s, -jnp.inf)
            m_old = m_refs[hh][...]
            m_new = jnp.maximum(m_old, jnp.max(s, axis=-1, keepdims=True))
            alpha = jnp.exp2(m_old - m_new)
            p = jnp.exp2(s - jnp.concatenate([m_new] * (blk // LANES), axis=1))
            v_h = jnp.where(head_lanes[hh], vb, ones_col[hh])
            pv = jnp.dot(p.astype(BF16), v_h, preferred_element_type=F32)
            acc_refs[hh][...] = acc_refs[hh][...] * alpha + pv
            m_refs[hh][...] = m_new

    def body(kp, carry):
        step(pl.multiple_of(kp * (2 * blk), blk), False)
        step(pl.multiple_of(kp * (2 * blk) + blk, blk), False)
        return carry

    lax.fori_loop(0, qi // 2, body, 0)

    @pl.when(qi % 2 == 1)
    def _():
        step(pl.multiple_of((qi - 1) * blk, blk), False)

    step(q0, True)

    acc0, acc1 = acc0_ref[...], acc1_ref[...]
    l0 = acc0[:, sum_lane[0]:sum_lane[0] + 1]
    l1 = acc1[:, sum_lane[1]:sum_lane[1] + 1]
    o = jnp.where(lo, acc0 / l0, acc1 / l1)
    o2 = o * o
    ss_lo = jnp.sum(jnp.where(lo, o2, 0.0), axis=-1, keepdims=True)
    ss_hi = jnp.sum(jnp.where(lo, 0.0, o2), axis=-1, keepdims=True)
    ms = jnp.where(lo, ss_lo, ss_hi) * (1.0 / FOX_HEAD_DIM)
    o_ref[...] = (o * lax.rsqrt(ms + EPS) * gain_ref[...]).astype(o_ref.dtype)


def _fox_attention(fq, fk, fv, cum, gain_pairs, *, batch, seq, blk):
    npairs = FOX_HEADS // 2
    return pl.pallas_call(
        functools.partial(_fox_kernel, blk=blk),
        grid=(batch, npairs, seq // blk),
        in_specs=[
            pl.BlockSpec((None, blk, LANES), lambda b, j, i: (b, i, j)),
            pl.BlockSpec((None, seq, LANES), lambda b, j, i: (b, 0, j)),
            pl.BlockSpec((None, seq, LANES), lambda b, j, i: (b, 0, j)),
            pl.BlockSpec((None, None, 2, seq), lambda b, j, i: (b, j, 0, 0)),
            pl.BlockSpec((None, 1, LANES), lambda b, j, i: (j, 0, 0)),
        ],
        out_specs=pl.BlockSpec((None, blk, LANES), lambda b, j, i: (b, i, j)),
        out_shape=jax.ShapeDtypeStruct((batch, seq, FOX_WIDTH), BF16),
        scratch_shapes=[pltpu.VMEM((blk, LANES), F32) for _ in range(4)],
        compiler_params=_cparams(("parallel", "parallel", "arbitrary")),
        name="fox_attention",
    )(fq, fk, fv, cum, gain_pairs)


GDN_BLK = 4 * CHUNK


def _gdn_prep_kernel(x_ref, halo_ref, cw_ref, sm_ref, smt_ref,
                     u_ref, w_ref, qd_ref, qk_ref, kdt_ref, egl_ref, xpad_ref):
    i = pl.program_id(1)
    nb = GDN_BLK
    halo = halo_ref[...]
    xpad_ref[0:SUBLANES, :] = jnp.where(i > 0, halo, jnp.zeros_like(halo))
    xpad_ref[SUBLANES:, :] = x_ref[...]
    y = None
    for j in range(CONV_WIDTH):
        start = SUBLANES - (CONV_WIDTH - 1) + j
        term = cw_ref[j:j + 1, :] * xpad_ref[start:start + nb, :]
        y = term if y is None else y + term
    y = y * _sigmoid(y)

    r = lax.broadcasted_iota(jnp.int32, (nb, nb), 0)
    c = lax.broadcasted_iota(jnp.int32, (nb, nb), 1)
    chunk_shift = CHUNK.bit_length() - 1
    same = jnp.right_shift(r, chunk_shift) == jnp.right_shift(c, chunk_shift)
    incl = same & (c <= r)
    strict = same & (c < r)
    sm = sm_ref[...]
    g_cum = _dot_mask_left(incl, sm)
    g_tot = _dot_mask_left(same, sm)
    g_cum_t = _dot_mask_right(smt_ref[...], same & (r <= c))
    eye = (r == c).astype(F32)

    powers, t_invs, rhs = [], [], []
    for h in range(GDN_HEADS):
        sl = slice(h * GDN_HEAD_DIM, (h + 1) * GDN_HEAD_DIM)
        q = y[:, h * GDN_HEAD_DIM:(h + 1) * GDN_HEAD_DIM]
        k = y[:, GDN_WIDTH + h * GDN_HEAD_DIM:GDN_WIDTH + (h + 1) * GDN_HEAD_DIM]
        v = y[:, 2 * GDN_WIDTH + h * GDN_HEAD_DIM:2 * GDN_WIDTH + (h + 1) * GDN_HEAD_DIM]
        qn = q * lax.rsqrt(jnp.sum(q * q, axis=-1, keepdims=True) + EPS) * GDN_HEAD_DIM ** -0.5
        kn = k * lax.rsqrt(jnp.sum(k * k, axis=-1, keepdims=True) + EPS)
        gc = g_cum[:, SM_G0 + h:SM_G0 + h + 1]
        gl = g_tot[:, SM_G0 + h:SM_G0 + h + 1]
        gr = g_cum_t[SM_G0 + h:SM_G0 + h + 1, :]
        beta = sm[:, SM_B0 + h:SM_B0 + h + 1]
        decay = jnp.where(incl, jnp.exp(jnp.where(incl, gc - gr, 0.0)), 0.0)
        qb, kb = qn.astype(BF16), kn.astype(BF16)
        kk = lax.dot_general(kb, kb, NT_DIMS, preferred_element_type=F32)
        a = jnp.where(strict, beta * kk * decay, 0.0)
        eg = jnp.exp(gc)
        powers.append(a)
        t_invs.append(eye - a)
        rhs.append(jnp.concatenate([(v * beta).astype(BF16), (kn * (beta * eg)).astype(BF16)], axis=1))
        qk = lax.dot_general(qb, kb, NT_DIMS, preferred_element_type=F32)
        qk_ref[:, h * nb:(h + 1) * nb] = jnp.where(incl, qk * decay, 0.0).astype(BF16)
        qd_ref[:, sl] = (qn * eg).astype(BF16)
        kdt_ref[sl, :] = (kn * jnp.exp(gl - gc)).T.astype(BF16)
        egl_ref[:, sl] = jnp.broadcast_to(jnp.exp(gl), (nb, GDN_HEAD_DIM))

    for _ in range(5):
        for h in range(GDN_HEADS):
            pb = powers[h].astype(BF16)
            powers[h] = jnp.dot(pb, pb, preferred_element_type=F32)
        for h in range(GDN_HEADS):
            t_invs[h] = t_invs[h] + jnp.dot(t_invs[h].astype(BF16), powers[h].astype(BF16),
                                            preferred_element_type=F32)
    for h in range(GDN_HEADS):
        sl = slice(h * GDN_HEAD_DIM, (h + 1) * GDN_HEAD_DIM)
        uw = jnp.dot(t_invs[h].astype(BF16), rhs[h], preferred_element_type=F32)
        u_ref[:, sl] = uw[:, :GDN_HEAD_DIM]
        w_ref[:, sl] = uw[:, GDN_HEAD_DIM:].astype(BF16)


def _gdn_prep(gqkv, conv_w_t, sm, smt, *, batch, seq):
    nb = GDN_BLK
    bps = seq // nb
    hps = nb // SUBLANES
    width = 3 * GDN_WIDTH
    row = lambda w: pl.BlockSpec((None, nb, w), lambda b, i: (b, i, 0))
    return pl.pallas_call(
        _gdn_prep_kernel,
        grid=(batch, bps),
        in_specs=[
            pl.BlockSpec((nb, width), lambda b, i: (b * bps + i, 0)),
            pl.BlockSpec((SUBLANES, width), lambda b, i: (jnp.maximum((b * bps + i) * hps - 1, 0), 0)),
            pl.BlockSpec((CONV_WIDTH, width), lambda b, i: (0, 0)),
            pl.BlockSpec((nb, LANES), lambda b, i: (b * bps + i, 0)),
            pl.BlockSpec((LANES, nb), lambda b, i: (0, b * bps + i)),
        ],
        out_specs=(row(GDN_WIDTH), row(GDN_WIDTH), row(GDN_WIDTH), row(GDN_HEADS * nb),
                   pl.BlockSpec((None, GDN_WIDTH, nb), lambda b, i: (b, 0, i)), row(GDN_WIDTH)),
        out_shape=(
            jax.ShapeDtypeStruct((batch, seq, GDN_WIDTH), F32),
            jax.ShapeDtypeStruct((batch, seq, GDN_WIDTH), BF16),
            jax.ShapeDtypeStruct((batch, seq, GDN_WIDTH), BF16),
            jax.ShapeDtypeStruct((batch, seq, GDN_HEADS * nb), BF16),
            jax.ShapeDtypeStruct((batch, GDN_WIDTH, seq), BF16),
            jax.ShapeDtypeStruct((batch, seq, GDN_WIDTH), F32),
        ),
        scratch_shapes=[pltpu.VMEM((nb + SUBLANES, width), F32)],
        compiler_params=_cparams(("parallel", "parallel")),
        name="gdn_prep",
    )(gqkv, gqkv, conv_w_t, sm, smt)


def _gdn_scan_kernel(u_ref, w_ref, qd_ref, qk_ref, kdt_ref, egl_ref, z_ref, gain_ref, o_ref, s_ref, vz_ref,
                     *, batch):
    nb = GDN_BLK

    @pl.when(pl.program_id(0) == 0)
    def _():
        s_ref[...] = jnp.zeros_like(s_ref)

    vz_ref[...] = jnp.zeros_like(vz_ref)
    for cidx in range(nb // CHUNK):
        rows = slice(cidx * CHUNK, (cidx + 1) * CHUNK)
        for b in range(batch):
            for h in range(GDN_HEADS):
                bh = b * GDN_HEADS + h
                sl = slice(h * GDN_HEAD_DIM, (h + 1) * GDN_HEAD_DIM)
                s_old = s_ref[bh]
                lhs1 = jnp.concatenate([w_ref[b, rows, sl], qd_ref[b, rows, sl]], axis=0)
                r1 = jnp.dot(lhs1, s_old.astype(BF16), preferred_element_type=F32)
                v_new = u_ref[b, rows, sl] - r1[:CHUNK]
                vz_ref[bh, rows, :] = v_new.astype(BF16)
                lhs2 = jnp.concatenate([qk_ref[b, rows, h * nb:(h + 1) * nb], kdt_ref[b, sl, :]], axis=0)
                r2 = jnp.dot(lhs2, vz_ref[bh], preferred_element_type=F32)
                vz_ref[bh, rows, :] = jnp.zeros((CHUNK, GDN_HEAD_DIM), BF16)
                last = egl_ref[b, (cidx + 1) * CHUNK - 1:(cidx + 1) * CHUNK, sl]
                s_ref[bh] = s_old * last + r2[CHUNK:]
                o = r1[CHUNK:] + r2[:CHUNK]
                z = z_ref[b, rows, sl]
                o_ref[b, rows, sl] = (_rms(o, gain_ref[...]) * (z * _sigmoid(z))).astype(o_ref.dtype)


def _gdn_scan(u, w, qd, qk, kdt, egl, z, gain, *, batch, seq):
    nb = GDN_BLK
    row = lambda width: pl.BlockSpec((batch, nb, width), lambda i: (0, i, 0))
    return pl.pallas_call(
        functools.partial(_gdn_scan_kernel, batch=batch),
        grid=(seq // nb,),
        in_specs=[row(GDN_WIDTH), row(GDN_WIDTH), row(GDN_WIDTH), row(GDN_HEADS * nb),
                  pl.BlockSpec((batch, GDN_WIDTH, nb), lambda i: (0, 0, i)), row(GDN_WIDTH), row(GDN_WIDTH),
                  pl.BlockSpec((1, GDN_HEAD_DIM), lambda i: (0, 0))],
        out_specs=row(GDN_WIDTH),
        out_shape=jax.ShapeDtypeStruct((batch, seq, GDN_WIDTH), BF16),
        scratch_shapes=[pltpu.VMEM((batch * GDN_HEADS, GDN_HEAD_DIM, GDN_HEAD_DIM), F32),
                        pltpu.VMEM((batch * GDN_HEADS, nb, GDN_HEAD_DIM), BF16)],
        compiler_params=_cparams(("arbitrary",)),
        name="gdn_scan",
    )(u, w, qd, qk, kdt, egl, z, gain)


def _memkv_kernel(m_ref, gain_ref, w_ref, kg_ref, k_ref, v_ref):
    d = m_ref.shape[-1]
    hd = d // XA_HEADS
    mb = _rms(m_ref[...], gain_ref[...]).astype(BF16)
    kv = jnp.dot(mb, w_ref[...], preferred_element_type=F32)
    for h in range(XA_HEADS):
        sl = slice(h * hd, (h + 1) * hd)
        k_ref[:, sl] = _rms(kv[:, sl], kg_ref[...]).astype(BF16)
    v_ref[...] = kv[:, d:].astype(BF16)


def _memkv(mem2d, gain, w_xkv, xk_gain, *, batch, mem_len):
    d = mem2d.shape[-1]
    full = lambda shape: pl.BlockSpec(shape, lambda b: (0,) * len(shape))
    row = pl.BlockSpec((mem_len, d), lambda b: (b, 0))
    return pl.pallas_call(
        _memkv_kernel,
        grid=(batch,),
        in_specs=[row, full((1, d)), full((d, 2 * d)), full((1, d // XA_HEADS))],
        out_specs=(row, row),
        out_shape=(jax.ShapeDtypeStruct(mem2d.shape, BF16), jax.ShapeDtypeStruct(mem2d.shape, BF16)),
        compiler_params=_cparams(("parallel",)),
        name="mem_kv",
    )(mem2d, gain, w_xkv, xk_gain)


def _mid_kernel(x_ref, of_ref, og_ref, wo_ref, xg_ref, wq_ref, qg_ref, k_ref, v_ref, wxo_ref, mg_ref,
                wr_ref, br_ref,
                x2_ref, h3_ref, mi_ref, mw_ref, cnt_ref, carry_ref, *, tm):
    i = pl.program_id(0)
    d = x_ref.shape[-1]
    hd = d // XA_HEADS
    x1 = (x_ref[...]
          + jnp.dot(of_ref[...], wo_ref[0:FOX_WIDTH, :], preferred_element_type=F32)
          + jnp.dot(og_ref[...], wo_ref[FOX_WIDTH:, :], preferred_element_type=F32))
    h2 = _rms(x1, xg_ref[...]).astype(BF16)
    q = jnp.dot(h2, wq_ref[...], preferred_element_type=F32)
    heads = []
    for h in range(XA_HEADS):
        sl = slice(h * hd, (h + 1) * hd)
        qn = (_rms(q[:, sl], qg_ref[...]) * hd ** -0.5).astype(BF16)
        s = lax.dot_general(qn, k_ref[:, sl], NT_DIMS, preferred_element_type=F32)
        p = jnp.exp(s - jnp.max(s, axis=-1, keepdims=True))
        p = p / jnp.sum(p, axis=-1, keepdims=True)
        heads.append(jnp.dot(p.astype(BF16), v_ref[:, sl], preferred_element_type=F32).astype(BF16))
    x2 = x1 + jnp.dot(jnp.concatenate(heads, axis=-1), wxo_ref[...], preferred_element_type=F32)
    x2_ref[...] = x2
    h3 = _rms(x2, mg_ref[...])
    for cc in range(SUBLANES):
        h3_ref[pl.ds(cc, tm, stride=SUBLANES), :] = h3[:, cc * LANES:(cc + 1) * LANES]
    h_hi, h_lo = _split_bf16(h3)
    logits = (jnp.dot(h_hi, wr_ref[0], preferred_element_type=F32)
              + jnp.dot(h_lo, wr_ref[0], preferred_element_type=F32)
              + jnp.dot(h_hi, wr_ref[1], preferred_element_type=F32)) + br_ref[...]
    lane = lax.broadcasted_iota(jnp.int32, (tm, LANES), 1)
    work = logits
    vals, idxs = [], []
    onehot = jnp.zeros((tm, LANES), F32)
    for _ in range(TOP_K):
        mx = jnp.max(work, axis=-1, keepdims=True)
        idx = jnp.min(jnp.where(work == mx, lane, LANES), axis=-1, keepdims=True)
        sel = lane == idx
        onehot = jnp.where(sel, 1.0, onehot)
        work = jnp.where(sel, -jnp.inf, work)
        vals.append(mx)
        idxs.append(idx)
    es = [jnp.exp(v - vals[0]) for v in vals]
    denom = es[0] + es[1] + es[2] + es[3]

    @pl.when(i == 0)
    def _():
        carry_ref[...] = jnp.zeros_like(carry_ref)

    r = lax.broadcasted_iota(jnp.int32, (tm, tm), 0)
    c = lax.broadcasted_iota(jnp.int32, (tm, tm), 1)
    before = jnp.dot((c < r).astype(BF16), onehot.astype(BF16), preferred_element_type=F32) + carry_ref[...]
    mi = jnp.zeros((tm, LANES), jnp.int32)
    mw = jnp.zeros((tm, LANES), F32)
    for kk in range(TOP_K):
        rank = jnp.sum(jnp.where(lane == idxs[kk], before, 0.0), axis=-1, keepdims=True).astype(jnp.int32)
        mi = jnp.where(lane == kk, idxs[kk], mi)
        mi = jnp.where(lane == TOP_K + kk, rank, mi)
        mw = jnp.where(lane == kk, es[kk] / denom, mw)
    mi_ref[...] = mi
    mw_ref[...] = mw
    total = carry_ref[...] + jnp.sum(onehot, axis=0, keepdims=True)
    carry_ref[...] = total
    cnt_ref[...] = jnp.broadcast_to(total, cnt_ref.shape).astype(jnp.int32)


def _mid(x2d, o_fox, o_gdn, w_out, xg, w_xq, xq_gain, kx, vx, w_xo, mg, w_r, b_r, *, seq, mem_len, tm):
    n, d = x2d.shape
    full = lambda shape: pl.BlockSpec(shape, lambda i: (0,) * len(shape))
    rows = lambda width: pl.BlockSpec((tm, width), lambda i: (i, 0))
    mem = pl.BlockSpec((mem_len, d), lambda i: (i // (seq // tm), 0))
    return pl.pallas_call(
        functools.partial(_mid_kernel, tm=tm),
        grid=(n // tm,),
        in_specs=[rows(d), rows(FOX_WIDTH), rows(GDN_WIDTH), full((d, d)), full((1, d)), full((d, d)),
                  full((1, d // XA_HEADS)), mem, mem, full((d, d)), full((1, d)), full((2, d, LANES)),
                  full((1, LANES))],
        out_specs=(rows(d), pl.BlockSpec((tm * d // LANES, LANES), lambda i: (i, 0)), rows(LANES), rows(LANES),
                   full((SUBLANES, LANES))),
        out_shape=(jax.ShapeDtypeStruct((n, d), F32), jax.ShapeDtypeStruct((n * d // LANES, LANES), F32),
                   jax.ShapeDtypeStruct((n, LANES), jnp.int32), jax.ShapeDtypeStruct((n, LANES), F32),
                   jax.ShapeDtypeStruct((SUBLANES, LANES), jnp.int32)),
        scratch_shapes=[pltpu.VMEM((1, LANES), F32)],
        compiler_params=_cparams(("arbitrary",)),
        name="outproj_xattn_router",
    )(x2d, o_fox, o_gdn, w_out, xg, w_xq, xq_gain, kx, vx, w_xo, mg, w_r, b_r)


MOE_TM = 512
DISPATCH_TOKENS = 128


def _dispatch_kernel(pos_ref, src_ref, dst_ref, sem):
    nt = DISPATCH_TOKENS

    def copy(t, kk):
        src = src_ref.at[pl.ds(pl.multiple_of(t * SUBLANES, SUBLANES), SUBLANES), :]
        row = pl.multiple_of(pos_ref[0, t * TOP_K + kk], SUBLANES)
        return pltpu.make_async_copy(src, dst_ref.at[pl.ds(row, SUBLANES), :], sem)

    def issue(t, carry):
        for kk in range(TOP_K):
            copy(t, kk).start(priority=kk % 2)
        return carry

    lax.fori_loop(0, nt, issue, 0, unroll=4)

    def drain(t, carry):
        for kk in range(TOP_K):
            copy(t, kk).wait()
        return carry

    lax.fori_loop(0, nt, drain, 0, unroll=4)


def _dispatch(h3_tiles, pos, *, rows_padded):
    n = h3_tiles.shape[0] // SUBLANES
    nt = DISPATCH_TOKENS
    return pl.pallas_call(
        _dispatch_kernel,
        grid=(n // nt,),
        in_specs=[pl.BlockSpec((None, 1, nt * TOP_K), lambda i: (i, 0, 0), memory_space=pltpu.SMEM),
                  pl.BlockSpec((nt * SUBLANES, LANES), lambda i: (i, 0))],
        out_specs=pl.BlockSpec(memory_space=pl.ANY),
        out_shape=jax.ShapeDtypeStruct((rows_padded * SUBLANES, LANES), F32),
        scratch_shapes=[pltpu.SemaphoreType.DMA],
        compiler_params=pltpu.CompilerParams(dimension_semantics=("arbitrary",), has_side_effects=True),
        name="moe_dispatch",
    )(pos, h3_tiles)


def _expert_kernel(te_ref, nv_ref, xs_ref, wgu_ref, bgu_ref, wd_ref, bd_ref, y_ref, wgu_bf, wd_bf, acc_ref,
                   *, chunk):
    i = pl.program_id(0)
    tm = xs_ref.shape[0] // SUBLANES
    d = wd_bf.shape[1]
    f = wd_bf.shape[0]
    nvalid = nv_ref[i]
    first = jnp.logical_or(i == 0, te_ref[i] != te_ref[jnp.maximum(i - 1, 0)])

    @pl.when(jnp.logical_and(first, nvalid > 0))
    def _():
        wgu_bf[...] = wgu_ref[0].astype(BF16)
        wd_bf[...] = wd_ref[0].astype(BF16)

    @pl.when(nvalid > 0)
    def _():
        row = lax.broadcasted_iota(jnp.int32, (tm, 1), 0)
        x = jnp.concatenate([xs_ref[pl.ds(cc, tm, stride=SUBLANES), :] for cc in range(SUBLANES)], axis=-1)
        x = jnp.where(row < nvalid, x, 0.0).astype(BF16)
        for j in range(f // chunk):
            cs = slice(j * chunk, (j + 1) * chunk)
            us = slice(f + j * chunk, f + (j + 1) * chunk)
            g = jnp.dot(x, wgu_bf[:, cs], preferred_element_type=F32) + bgu_ref[0, :, cs]
            u = jnp.dot(x, wgu_bf[:, us], preferred_element_type=F32) + bgu_ref[0, :, us]
            gate = jnp.minimum(g, SWIGLU_LIMIT)
            up = jnp.clip(u, -SWIGLU_LIMIT, SWIGLU_LIMIT)
            act = ((up + 1.0) * (gate * _sigmoid(SWIGLU_ALPHA * gate))).astype(BF16)
            part = jnp.dot(act, wd_bf[cs, :], preferred_element_type=F32)
            if j == 0:
                acc_ref[...] = part + bd_ref[0]
            else:
                acc_ref[...] += part
        for cc in range(SUBLANES):
            y_ref[pl.ds(cc, tm, stride=SUBLANES), :] = acc_ref[:, cc * LANES:(cc + 1) * LANES]

    @pl.when(nvalid <= 0)
    def _():
        y_ref[...] = jnp.zeros_like(y_ref)


def _experts(tile_expert, tile_valid, xs, w_gate_up, b_gate_up, w_down, b_down):
    rows_padded = xs.shape[0] // SUBLANES
    e, d, f2 = w_gate_up.shape
    f = f2 // 2
    tm = MOE_TM
    grid_spec = pltpu.PrefetchScalarGridSpec(
        num_scalar_prefetch=2,
        grid=(rows_padded // tm,),
        in_specs=[
            pl.BlockSpec((tm * SUBLANES, LANES), lambda i, te, nv: (i, 0)),
            pl.BlockSpec((1, d, f2), lambda i, te, nv: (te[i], 0, 0)),
            pl.BlockSpec((1, 1, f2), lambda i, te, nv: (te[i], 0, 0)),
            pl.BlockSpec((1, f, d), lambda i, te, nv: (te[i], 0, 0)),
            pl.BlockSpec((1, 1, d), lambda i, te, nv: (te[i], 0, 0)),
        ],
        out_specs=pl.BlockSpec((tm * SUBLANES, LANES), lambda i, te, nv: (i, 0)),
        scratch_shapes=[pltpu.VMEM((d, f2), BF16), pltpu.VMEM((f, d), BF16), pltpu.VMEM((tm, d), F32)],
    )
    return pl.pallas_call(
        functools.partial(_expert_kernel, chunk=512),
        grid_spec=grid_spec,
        out_shape=jax.ShapeDtypeStruct(xs.shape, F32),
        compiler_params=_cparams(("arbitrary",)),
        name="moe_experts",
    )(tile_expert, tile_valid, xs, w_gate_up, b_gate_up.reshape(e, 1, f2), w_down, b_down.reshape(e, 1, d))


COMBINE_TOKENS = 128


def _combine_kernel(pos_ref, y_ref, x2_ref, mw_ref, o_ref, ybuf, sem):
    nt = COMBINE_TOKENS

    def copy(t, kk):
        row = pl.multiple_of(pos_ref[0, t * TOP_K + kk], SUBLANES)
        dst = ybuf.at[kk, pl.ds(pl.multiple_of(t * SUBLANES, SUBLANES), SUBLANES), :]
        return pltpu.make_async_copy(y_ref.at[pl.ds(row, SUBLANES), :], dst, sem)

    def issue(t, carry):
        for kk in range(TOP_K):
            copy(t, kk).start(priority=kk % 2)
        return carry

    lax.fori_loop(0, nt, issue, 0, unroll=4)

    def drain(t, carry):
        for kk in range(TOP_K):
            copy(t, kk).wait()
        return carry

    lax.fori_loop(0, nt, drain, 0, unroll=4)

    mw = mw_ref[...]
    for cc in range(SUBLANES):
        cs = slice(cc * LANES, (cc + 1) * LANES)
        acc = x2_ref[:, cs]
        for kk in range(TOP_K):
            acc = acc + mw[:, kk:kk + 1] * ybuf.at[kk][pl.ds(cc, nt, stride=SUBLANES), :]
        o_ref[:, cs] = acc


def _combine(pos, y, x2, mw):
    n, d = x2.shape
    nt = COMBINE_TOKENS
    return pl.pallas_call(
        _combine_kernel,
        grid=(n // nt,),
        in_specs=[pl.BlockSpec((None, 1, nt * TOP_K), lambda i: (i, 0, 0), memory_space=pltpu.SMEM),
                  pl.BlockSpec(memory_space=pl.ANY),
                  pl.BlockSpec((nt, d), lambda i: (i, 0)),
                  pl.BlockSpec((nt, LANES), lambda i: (i, 0))],
        out_specs=pl.BlockSpec((nt, d), lambda i: (i, 0)),
        out_shape=jax.ShapeDtypeStruct((n, d), F32),
        scratch_shapes=[pltpu.VMEM((TOP_K, nt * SUBLANES, LANES), F32), pltpu.SemaphoreType.DMA],
        compiler_params=_cparams(("arbitrary",)),
        name="moe_combine",
    )(pos, y, x2, mw)


def _layer(x, mem, mix_norm, w_in, b_forget, fox_q_gain, fox_k_gain, fox_out_gain, gdn_conv_w, gdn_a_log,
           gdn_dt_bias, gdn_out_gain, w_out, xattn_norm, mem_norm, w_xq, w_xkv, xq_gain, xk_gain, w_xo,
           moe_norm, w_router, b_router, w_gate_up, b_gate_up, w_down, b_down):
    batch, seq, d = x.shape
    n = batch * seq
    mem_len = mem.shape[1]
    x2d = x.reshape(n, d)

    o_ff = 3 * FOX_WIDTH
    o_gq = o_ff + FOX_HEADS
    o_ga = o_gq + 3 * GDN_WIDTH
    o_gb = o_ga + GDN_HEADS
    o_gz = o_gb + GDN_HEADS
    w_main = jnp.concatenate([w_in[:, :o_ff], w_in[:, o_gq:o_ga], w_in[:, o_gz:]], axis=1).astype(BF16)
    w_small = jnp.concatenate([w_in[:, o_ff:o_gq], w_in[:, o_ga:o_gz],
                               jnp.zeros((d, LANES - SM_END), F32)], axis=1).astype(BF16)
    prow = jnp.zeros((SUBLANES, LANES), F32)
    prow = prow.at[0, SM_F0:SM_G0].set(b_forget).at[0, SM_G0:SM_B0].set(gdn_dt_bias)
    prow = prow.at[1, SM_G0:SM_B0].set(gdn_a_log)
    pcol = prow[0:2].T
    head_id = jnp.arange(FOX_WIDTH) // FOX_HEAD_DIM
    bd = (head_id[:, None] == head_id[None, :]).astype(F32)
    row1 = lambda v: v.reshape(1, -1)

    tm = min(512, seq)
    fq, fk, fv, gqkv, gz, sm, smt = _inproj(
        x2d, row1(mix_norm), w_main, w_small, w_small.T, bd,
        row1(jnp.tile(fox_q_gain, FOX_HEADS)), row1(jnp.tile(fox_k_gain, FOX_HEADS)), prow, pcol,
        seq=seq, tm=tm)

    cum = smt[SM_F0:SM_G0].reshape(FOX_HEADS // 2, 2, batch, seq).transpose(2, 0, 1, 3)
    b3 = lambda a: a.reshape(batch, seq, a.shape[-1])
    o_fox = _fox_attention(b3(fq), b3(fk), b3(fv), cum, fox_out_gain.reshape(FOX_HEADS // 2, 1, LANES),
                           batch=batch, seq=seq, blk=min(512, seq))

    u, w, qd, qk, kdt, egl = _gdn_prep(gqkv, gdn_conv_w.T, sm, smt, batch=batch, seq=seq)
    o_gdn = _gdn_scan(u, w, qd, qk, kdt, egl, b3(gz), row1(gdn_out_gain), batch=batch, seq=seq)

    kx, vx = _memkv(mem.reshape(batch * mem_len, d), row1(mem_norm), w_xkv.astype(BF16), row1(xk_gain),
                    batch=batch, mem_len=mem_len)
    w_r = jnp.concatenate([w_router, jnp.zeros((d, LANES - N_EXPERTS), F32)], axis=1)
    w_r = jnp.stack(_split_bf16(w_r))
    b_r = jnp.concatenate([b_router, jnp.full((LANES - N_EXPERTS,), -jnp.inf, F32)]).reshape(1, LANES)
    x2, h3, mi, mw, cnt = _mid(
        x2d, o_fox.reshape(n, FOX_WIDTH), o_gdn.reshape(n, GDN_WIDTH), w_out.astype(BF16), row1(xattn_norm),
        w_xq.astype(BF16), row1(xq_gain), kx, vx, w_xo.astype(BF16), row1(moe_norm), w_r, b_r,
        seq=seq, mem_len=mem_len, tm=min(512, seq))

    counts = cnt[0, :N_EXPERTS]
    padded = ((counts + MOE_TM - 1) // MOE_TM) * MOE_TM
    ends = jnp.cumsum(padded)
    starts = ends - padded
    eidx = mi[:, :TOP_K]
    pos = starts[eidx] + mi[:, TOP_K:2 * TOP_K]
    rows_padded = n * TOP_K + N_EXPERTS * MOE_TM
    n_tiles = rows_padded // MOE_TM
    tile_start = jnp.arange(n_tiles, dtype=jnp.int32) * MOE_TM
    tile_expert = jnp.sum((ends[None, :] <= tile_start[:, None]).astype(jnp.int32), axis=1)
    tile_expert = jnp.minimum(tile_expert, N_EXPERTS - 1)
    tile_valid = jnp.clip(starts[tile_expert] + counts[tile_expert] - tile_start, 0, MOE_TM).astype(jnp.int32)
    last_used = jnp.max(jnp.where(tile_valid > 0, tile_expert, 0))
    tile_expert = jnp.where(tile_valid > 0, tile_expert, last_used).astype(jnp.int32)

    pos_tiles = (pos * SUBLANES).reshape(n // DISPATCH_TOKENS, 1, DISPATCH_TOKENS * TOP_K).astype(jnp.int32)
    xs = _dispatch(h3, pos_tiles, rows_padded=rows_padded)
    y = _experts(tile_expert, tile_valid, xs, w_gate_up, b_gate_up, w_down, b_down)
    out = _combine(pos_tiles, y, x2, mw)
    return out.reshape(batch, seq, d)


def kernel(x, mem, mix_norm, w_in, b_forget, fox_q_gain, fox_k_gain, fox_out_gain, gdn_conv_w, gdn_a_log,
           gdn_dt_bias, gdn_out_gain, w_out, xattn_norm, mem_norm, w_xq, w_xkv, xq_gain, xk_gain, w_xo,
           moe_norm, w_router, b_router, w_gate_up, b_gate_up, w_down, b_down):
    depth = mix_norm.shape[0]
    for l in range(depth):
        x = _layer(x, mem, mix_norm[l], w_in[l], b_forget[l], fox_q_gain[l], fox_k_gain[l], fox_out_gain[l],
                   gdn_conv_w[l], gdn_a_log[l], gdn_dt_bias[l], gdn_out_gain[l], w_out[l], xattn_norm[l],
                   mem_norm[l], w_xq[l], w_xkv[l], xq_gain[l], xk_gain[l], w_xo[l], moe_norm[l], w_router[l],
                   b_router[l], w_gate_up[l], b_gate_up[l], w_down[l], b_down[l])
    return x
```

```python
import functools

import jax
import jax.numpy as jnp
from jax import lax
from jax.experimental import pallas as pl
from jax.experimental.pallas import tpu as pltpu

F32 = jnp.float32
BF16 = jnp.bfloat16

EPS = 1e-6
FOX_HEADS, FOX_HEAD_DIM = 8, 64
GDN_HEADS, GDN_HEAD_DIM = 4, 128
FOX_WIDTH = FOX_HEADS * FOX_HEAD_DIM
GDN_WIDTH = GDN_HEADS * GDN_HEAD_DIM
CONV_WIDTH = 4
CHUNK = 64
XA_HEADS = 4
N_EXPERTS = 32
TOP_K = 4
SWIGLU_LIMIT = 7.0
SWIGLU_ALPHA = 1.702
LOG2E = 1.4426950408889634

LANES = 128
SUBLANES = 8
VMEM_LIMIT = 52 * 1024 * 1024

SM_F0, SM_G0, SM_B0, SM_END = 0, 8, 12, 16

NT_DIMS = (((1,), (1,)), ((), ()))


def _cparams(sem):
    return pltpu.CompilerParams(dimension_semantics=sem, vmem_limit_bytes=VMEM_LIMIT)


def _rms(x, gain):
    return x * lax.rsqrt(jnp.mean(x * x, axis=-1, keepdims=True) + EPS) * gain


def _split_bf16(x):
    hi = x.astype(BF16)
    return hi, (x - hi.astype(F32)).astype(BF16)


def _dot_mask_left(mask, x):
    hi, lo = _split_bf16(x)
    mb = mask.astype(BF16)
    return jnp.dot(mb, hi, preferred_element_type=F32) + jnp.dot(mb, lo, preferred_element_type=F32)


def _dot_mask_right(x, mask):
    hi, lo = _split_bf16(x)
    mb = mask.astype(BF16)
    return jnp.dot(hi, mb, preferred_element_type=F32) + jnp.dot(lo, mb, preferred_element_type=F32)


def _softplus(t):
    return jnp.maximum(t, 0.0) + jnp.log1p(jnp.exp(-jnp.abs(t)))


def _sigmoid(t):
    return 1.0 / (1.0 + jnp.exp(-t))


def _small_act(v, bias, alog, idx):
    t = v + bias
    tail = jnp.log1p(jnp.exp(-jnp.abs(t)))
    log_f = jnp.minimum(t, 0.0) - tail
    g = -jnp.exp(alog) * (jnp.maximum(t, 0.0) + tail)
    beta = _sigmoid(v)
    return jnp.where(idx < SM_G0, log_f, jnp.where(idx < SM_B0, g, jnp.where(idx < SM_END, beta, 0.0)))


def _inproj_kernel(x_ref, gain_ref, w_ref, ws_ref, wst_ref, bd_ref, qg_ref, kg_ref, prow_ref, pcol_ref,
                   fq_ref, fk_ref, fv_ref, gqkv_ref, gz_ref, sm_ref, smt_ref, carry_r, carry_c,
                   *, tm, tiles_per_seq):
    i = pl.program_id(0)
    hb = _rms(x_ref[...], gain_ref[...]).astype(BF16)

    def proj(lo, hi):
        return jnp.dot(hb, w_ref[:, lo:hi], preferred_element_type=F32)

    def headnorm(p, g):
        ss = _dot_mask_right(p * p, bd_ref[...])
        return p * lax.rsqrt(ss * (1.0 / FOX_HEAD_DIM) + EPS) * g

    w0 = FOX_WIDTH
    fq_ref[...] = (headnorm(proj(0, w0), qg_ref[...]) * (FOX_HEAD_DIM ** -0.5 * LOG2E)).astype(BF16)
    fk_ref[...] = headnorm(proj(w0, 2 * w0), kg_ref[...]).astype(BF16)
    fv_ref[...] = proj(2 * w0, 3 * w0).astype(BF16)
    g0 = 3 * w0
    gqkv_ref[...] = proj(g0, g0 + 3 * GDN_WIDTH)
    gz_ref[...] = proj(g0 + 3 * GDN_WIDTH, g0 + 4 * GDN_WIDTH)

    @pl.when(i % tiles_per_seq == 0)
    def _():
        carry_r[...] = jnp.zeros_like(carry_r)
        carry_c[...] = jnp.zeros_like(carry_c)

    sm = jnp.dot(hb, ws_ref[...], preferred_element_type=F32)
    smt = lax.dot_general(wst_ref[...], hb, NT_DIMS, preferred_element_type=F32)
    lane = lax.broadcasted_iota(jnp.int32, (1, LANES), 1)
    srow = lax.broadcasted_iota(jnp.int32, (LANES, 1), 0)
    vals = _small_act(sm, prow_ref[0:1, :], prow_ref[1:2, :], lane)
    vals_t = _small_act(smt, pcol_ref[:, 0:1], pcol_ref[:, 1:2], srow)

    r = lax.broadcasted_iota(jnp.int32, (tm, tm), 0)
    c = lax.broadcasted_iota(jnp.int32, (tm, tm), 1)
    cum = _dot_mask_left(c <= r, vals) + carry_r[...]
    cum_t = _dot_mask_right(vals_t, r <= c) + carry_c[...]
    carry_r[...] = cum[tm - 1:tm, :]
    carry_c[...] = cum_t[:, tm - 1:tm]
    sm_ref[...] = jnp.where(lane < SM_G0, cum, vals)
    smt_ref[...] = jnp.where(srow < SM_G0, cum_t, vals_t)


def _inproj(x2d, gain, w_main, w_small, w_small_t, bd, qg, kg, prow, pcol, *, seq, tm):
    n, d = x2d.shape
    wm = w_main.shape[1]
    full = lambda shape: pl.BlockSpec(shape, lambda i: (0,) * len(shape))
    rows = lambda width: pl.BlockSpec((tm, width), lambda i: (i, 0))
    out_shape = (
        jax.ShapeDtypeStruct((n, FOX_WIDTH), BF16),
        jax.ShapeDtypeStruct((n, FOX_WIDTH), BF16),
        jax.ShapeDtypeStruct((n, FOX_WIDTH), BF16),
        jax.ShapeDtypeStruct((n, 3 * GDN_WIDTH), F32),
        jax.ShapeDtypeStruct((n, GDN_WIDTH), F32),
        jax.ShapeDtypeStruct((n, LANES), F32),
        jax.ShapeDtypeStruct((LANES, n), F32),
    )
    return pl.pallas_call(
        functools.partial(_inproj_kernel, tm=tm, tiles_per_seq=seq // tm),
        grid=(n // tm,),
        in_specs=[rows(d), full((1, d)), full((d, wm)), full((d, LANES)), full((LANES, d)),
                  full((FOX_WIDTH, FOX_WIDTH)), full((1, FOX_WIDTH)), full((1, FOX_WIDTH)),
                  full((SUBLANES, LANES)), full((LANES, 2))],
        out_specs=(rows(FOX_WIDTH), rows(FOX_WIDTH), rows(FOX_WIDTH), rows(3 * GDN_WIDTH), rows(GDN_WIDTH),
                   rows(LANES), pl.BlockSpec((LANES, tm), lambda i: (0, i))),
        out_shape=out_shape,
        scratch_shapes=[pltpu.VMEM((1, LANES), F32), pltpu.VMEM((LANES, 1), F32)],
        compiler_params=_cparams(("arbitrary",)),
        name="inproj",
    )(x2d, gain, w_main, w_small, w_small_t, bd, qg, kg, prow, pcol)


def _fox_kernel(q_ref, k_ref, v_ref, cum_ref, gain_ref, o_ref, m0_ref, m1_ref, acc0_ref, acc1_ref, *, blk):
    qi = pl.program_id(2)
    lane = lax.broadcasted_iota(jnp.int32, (1, LANES), 1)
    lo = lane < FOX_HEAD_DIM
    q = q_ref[...]
    zero = jnp.zeros_like(q)
    q_heads = (jnp.where(lo, q, zero), jnp.where(lo, zero, q))
    q0 = pl.multiple_of(qi * blk, blk)
    c_ref = [cum_ref[hh:hh + 1, pl.ds(q0, LANES)][:, 0:1] for hh in range(2)]
    head_lanes = (lo, jnp.logical_not(lo))
    sum_lane = (FOX_HEAD_DIM, 0)
    ones_col = [jnp.where(lane == sum_lane[hh], 1.0, 0.0).astype(BF16) for hh in range(2)]
    m_refs = (m0_ref, m1_ref)
    acc_refs = (acc0_ref, acc1_ref)
    for hh in range(2):
        m_refs[hh][...] = jnp.full_like(m_refs[hh], -1e30)
        acc_refs[hh][...] = jnp.zeros_like(acc_refs[hh])

    def step(k0, masked):
        kb = k_ref[pl.ds(k0, blk), :]
        vb = v_ref[pl.ds(k0, blk), :]
        scores = [lax.dot_general(q_heads[hh], kb, NT_DIMS, preferred_element_type=F32) for hh in range(2)]
        for hh in range(2):
            s = scores[hh] + (c_ref[hh] - cum_ref[hh:hh + 1, pl.ds(k0, blk)]) * LOG2E
            if masked:
                r = lax.broadcasted_iota(jnp.int32, (blk, blk), 0)
                c = lax.broadcasted_iota(jnp.int32, (blk, blk), 1)
                s = jnp.where(c <= r, s, -jnp.inf)
            m_old = m_refs[hh][...]
            m_new = jnp.maximum(m_old, jnp.max(s, axis=-1, keepdims=True))
            alpha = jnp.exp2(m_old - m_new)
            p = jnp.exp2(s - jnp.concatenate([m_new] * (blk // LANES), axis=1))
            v_h = jnp.where(head_lanes[hh], vb, ones_col[hh])
            pv = jnp.dot(p.astype(BF16), v_h, preferred_element_type=F32)
            acc_refs[hh][...] = acc_refs[hh][...] * alpha + pv
            m_refs[hh][...] = m_new

    def body(kp, carry):
        step(pl.multiple_of(kp * (2 * blk), blk), False)
        step(pl.multiple_of(kp * (2 * blk) + blk, blk), False)
        return carry

    lax.fori_loop(0, qi // 2, body, 0)

    @pl.when(qi % 2 == 1)
    def _():
        step(pl.multiple_of((qi - 1) * blk, blk), False)

    step(q0, True)

    acc0, acc1 = acc0_ref[...], acc1_ref[...]
    l0 = acc0[:, sum_lane[0]:sum_lane[0] + 1]
    l1 = acc1[:, sum_lane[1]:sum_lane[1] + 1]
    o = jnp.where(lo, acc0 * (1.0 / l0), acc1 * (1.0 / l1))
    o2 = o * o
    ss_lo = jnp.sum(jnp.where(lo, o2, 0.0), axis=-1, keepdims=True)
    ss_hi = jnp.sum(jnp.where(lo, 0.0, o2), axis=-1, keepdims=True)
    ms = jnp.where(lo, ss_lo, ss_hi) * (1.0 / FOX_HEAD_DIM)
    o_ref[...] = (o * lax.rsqrt(ms + EPS) * gain_ref[...]).astype(o_ref.dtype)


def _fox_attention(fq, fk, fv, cum, gain_pairs, *, batch, seq, blk):
    npairs = FOX_HEADS // 2
    return pl.pallas_call(
        functools.partial(_fox_kernel, blk=blk),
        grid=(batch, npairs, seq // blk),
        in_specs=[
            pl.BlockSpec((None, blk, LANES), lambda b, j, i: (b, i, j)),
            pl.BlockSpec((None, seq, LANES), lambda b, j, i: (b, 0, j)),
            pl.BlockSpec((None, seq, LANES), lambda b, j, i: (b, 0, j)),
            pl.BlockSpec((None, None, 2, seq), lambda b, j, i: (b, j, 0, 0)),
            pl.BlockSpec((None, 1, LANES), lambda b, j, i: (j, 0, 0)),
        ],
        out_specs=pl.BlockSpec((None, blk, LANES), lambda b, j, i: (b, i, j)),
        out_shape=jax.ShapeDtypeStruct((batch, seq, FOX_WIDTH), BF16),
        scratch_shapes=[pltpu.VMEM((blk, LANES), F32) for _ in range(4)],
        compiler_params=_cparams(("parallel", "parallel", "arbitrary")),
        name="fox_attention",
    )(fq, fk, fv, cum, gain_pairs)


GDN_BLK = 4 * CHUNK


def _gdn_prep_kernel(x_ref, halo_ref, cw_ref, sm_ref, smt_ref,
                     u_ref, w_ref, qd_ref, qk_ref, kdt_ref, egl_ref, xpad_ref):
    i = pl.program_id(1)
    nb = GDN_BLK
    halo = halo_ref[...]
    xpad_ref[0:SUBLANES, :] = jnp.where(i > 0, halo, jnp.zeros_like(halo))
    xpad_ref[SUBLANES:, :] = x_ref[...]
    y = None
    for j in range(CONV_WIDTH):
        start = SUBLANES - (CONV_WIDTH - 1) + j
        term = cw_ref[j:j + 1, :] * xpad_ref[start:start + nb, :]
        y = term if y is None else y + term
    y = y * _sigmoid(y)

    r = lax.broadcasted_iota(jnp.int32, (nb, nb), 0)
    c = lax.broadcasted_iota(jnp.int32, (nb, nb), 1)
    chunk_shift = CHUNK.bit_length() - 1
    same = jnp.right_shift(r, chunk_shift) == jnp.right_shift(c, chunk_shift)
    incl = same & (c <= r)
    strict = same & (c < r)
    sm = sm_ref[...]
    g_cum = _dot_mask_left(incl, sm)
    g_tot = _dot_mask_left(same, sm)
    g_cum_t = _dot_mask_right(smt_ref[...], same & (r <= c))
    eye = (r == c).astype(F32)

    powers, t_invs, rhs = [], [], []
    for h in range(GDN_HEADS):
        sl = slice(h * GDN_HEAD_DIM, (h + 1) * GDN_HEAD_DIM)
        q = y[:, h * GDN_HEAD_DIM:(h + 1) * GDN_HEAD_DIM]
        k = y[:, GDN_WIDTH + h * GDN_HEAD_DIM:GDN_WIDTH + (h + 1) * GDN_HEAD_DIM]
        v = y[:, 2 * GDN_WIDTH + h * GDN_HEAD_DIM:2 * GDN_WIDTH + (h + 1) * GDN_HEAD_DIM]
        qn = q * lax.rsqrt(jnp.sum(q * q, axis=-1, keepdims=True) + EPS) * GDN_HEAD_DIM ** -0.5
        kn = k * lax.rsqrt(jnp.sum(k * k, axis=-1, keepdims=True) + EPS)
        gc = g_cum[:, SM_G0 + h:SM_G0 + h + 1]
        gl = g_tot[:, SM_G0 + h:SM_G0 + h + 1]
        gr = g_cum_t[SM_G0 + h:SM_G0 + h + 1, :]
        beta = sm[:, SM_B0 + h:SM_B0 + h + 1]
        decay = jnp.where(incl, jnp.exp(jnp.where(incl, gc - gr, 0.0)), 0.0)
        qb, kb = qn.astype(BF16), kn.astype(BF16)
        kk = lax.dot_general(kb, kb, NT_DIMS, preferred_element_type=F32)
        a = jnp.where(strict, beta * kk * decay, 0.0)
        eg = jnp.exp(gc)
        powers.append(a)
        t_invs.append(eye - a)
        rhs.append(jnp.concatenate([(v * beta).astype(BF16), (kn * (beta * eg)).astype(BF16)], axis=1))
        qk = lax.dot_general(qb, kb, NT_DIMS, preferred_element_type=F32)
        qk_ref[:, h * nb:(h + 1) * nb] = jnp.where(incl, qk * decay, 0.0).astype(BF16)
        qd_ref[:, sl] = (qn * eg).astype(BF16)
        kdt_ref[sl, :] = (kn * jnp.exp(gl - gc)).T.astype(BF16)
        egl_ref[:, sl] = jnp.broadcast_to(jnp.exp(gl), (nb, GDN_HEAD_DIM))

    for _ in range(5):
        for h in range(GDN_HEADS):
            pb = powers[h].astype(BF16)
            powers[h] = jnp.dot(pb, pb, preferred_element_type=F32)
        for h in range(GDN_HEADS):
            t_invs[h] = t_invs[h] + jnp.dot(t_invs[h].astype(BF16), powers[h].astype(BF16),
                                            preferred_element_type=F32)
    for h in range(GDN_HEADS):
        sl = slice(h * GDN_HEAD_DIM, (h + 1) * GDN_HEAD_DIM)
        uw = jnp.dot(t_invs[h].astype(BF16), rhs[h], preferred_element_type=F32)
        u_ref[:, sl] = uw[:, :GDN_HEAD_DIM]
        w_ref[:, sl] = uw[:, GDN_HEAD_DIM:].astype(BF16)


def _gdn_prep(gqkv, conv_w_t, sm, smt, *, batch, seq):
    nb = GDN_BLK
    bps = seq // nb
    hps = nb // SUBLANES
    width = 3 * GDN_WIDTH
    row = lambda w: pl.BlockSpec((None, nb, w), lambda b, i: (b, i, 0))
    return pl.pallas_call(
        _gdn_prep_kernel,
        grid=(batch, bps),
        in_specs=[
            pl.BlockSpec((nb, width), lambda b, i: (b * bps + i, 0)),
            pl.BlockSpec((SUBLANES, width), lambda b, i: (jnp.maximum((b * bps + i) * hps - 1, 0), 0)),
            pl.BlockSpec((CONV_WIDTH, width), lambda b, i: (0, 0)),
            pl.BlockSpec((nb, LANES), lambda b, i: (b * bps + i, 0)),
            pl.BlockSpec((LANES, nb), lambda b, i: (0, b * bps + i)),
        ],
        out_specs=(row(GDN_WIDTH), row(GDN_WIDTH), row(GDN_WIDTH), row(GDN_HEADS * nb),
                   pl.BlockSpec((None, GDN_WIDTH, nb), lambda b, i: (b, 0, i)), row(GDN_WIDTH)),
        out_shape=(
            jax.ShapeDtypeStruct((batch, seq, GDN_WIDTH), F32),
            jax.ShapeDtypeStruct((batch, seq, GDN_WIDTH), BF16),
            jax.ShapeDtypeStruct((batch, seq, GDN_WIDTH), BF16),
            jax.ShapeDtypeStruct((batch, seq, GDN_HEADS * nb), BF16),
            jax.ShapeDtypeStruct((batch, GDN_WIDTH, seq), BF16),
            jax.ShapeDtypeStruct((batch, seq, GDN_WIDTH), F32),
        ),
        scratch_shapes=[pltpu.VMEM((nb + SUBLANES, width), F32)],
        compiler_params=_cparams(("parallel", "parallel")),
        name="gdn_prep",
    )(gqkv, gqkv, conv_w_t, sm, smt)


def _gdn_scan_kernel(u_ref, w_ref, qd_ref, qk_ref, kdt_ref, egl_ref, z_ref, gain_ref, o_ref, s_ref, vz_ref,
                     *, batch):
    nb = GDN_BLK

    @pl.when(pl.program_id(0) == 0)
    def _():
        s_ref[...] = jnp.zeros_like(s_ref)

    vz_ref[...] = jnp.zeros_like(vz_ref)
    for cidx in range(nb // CHUNK):
        rows = slice(cidx * CHUNK, (cidx + 1) * CHUNK)
        for b in range(batch):
            for h in range(GDN_HEADS):
                bh = b * GDN_HEADS + h
                sl = slice(h * GDN_HEAD_DIM, (h + 1) * GDN_HEAD_DIM)
                s_old = s_ref[bh]
                lhs1 = jnp.concatenate([w_ref[b, rows, sl], qd_ref[b, rows, sl]], axis=0)
                r1 = jnp.dot(lhs1, s_old.astype(BF16), preferred_element_type=F32)
                v_new = u_ref[b, rows, sl] - r1[:CHUNK]
                vz_ref[bh, rows, :] = v_new.astype(BF16)
                lhs2 = jnp.concatenate([qk_ref[b, rows, h * nb:(h + 1) * nb], kdt_ref[b, sl, :]], axis=0)
                r2 = jnp.dot(lhs2, vz_ref[bh], preferred_element_type=F32)
                vz_ref[bh, rows, :] = jnp.zeros((CHUNK, GDN_HEAD_DIM), BF16)
                last = egl_ref[b, (cidx + 1) * CHUNK - 1:(cidx + 1) * CHUNK, sl]
                s_ref[bh] = s_old * last + r2[CHUNK:]
                o = r1[CHUNK:] + r2[:CHUNK]
                z = z_ref[b, rows, sl]
                o_ref[b, rows, sl] = (_rms(o, gain_ref[...]) * (z * _sigmoid(z))).astype(o_ref.dtype)


def _gdn_scan(u, w, qd, qk, kdt, egl, z, gain, *, batch, seq):
    nb = GDN_BLK
    row = lambda width: pl.BlockSpec((batch, nb, width), lambda i: (0, i, 0))
    return pl.pallas_call(
        functools.partial(_gdn_scan_kernel, batch=batch),
        grid=(seq // nb,),
        in_specs=[row(GDN_WIDTH), row(GDN_WIDTH), row(GDN_WIDTH), row(GDN_HEADS * nb),
                  pl.BlockSpec((batch, GDN_WIDTH, nb), lambda i: (0, 0, i)), row(GDN_WIDTH), row(GDN_WIDTH),
                  pl.BlockSpec((1, GDN_HEAD_DIM), lambda i: (0, 0))],
        out_specs=row(GDN_WIDTH),
        out_shape=jax.ShapeDtypeStruct((batch, seq, GDN_WIDTH), BF16),
        scratch_shapes=[pltpu.VMEM((batch * GDN_HEADS, GDN_HEAD_DIM, GDN_HEAD_DIM), F32),
                        pltpu.VMEM((batch * GDN_HEADS, nb, GDN_HEAD_DIM), BF16)],
        compiler_params=_cparams(("arbitrary",)),
        name="gdn_scan",
    )(u, w, qd, qk, kdt, egl, z, gain)


def _memkv_kernel(m_ref, gain_ref, w_ref, kg_ref, k_ref, v_ref):
    d = m_ref.shape[-1]
    hd = d // XA_HEADS
    mb = _rms(m_ref[...], gain_ref[...]).astype(BF16)
    kv = jnp.dot(mb, w_ref[...], preferred_element_type=F32)
    for h in range(XA_HEADS):
        sl = slice(h * hd, (h + 1) * hd)
        k_ref[:, sl] = _rms(kv[:, sl], kg_ref[...]).astype(BF16)
    v_ref[...] = kv[:, d:].astype(BF16)


def _memkv(mem2d, gain, w_xkv, xk_gain, *, batch, mem_len):
    d = mem2d.shape[-1]
    full = lambda shape: pl.BlockSpec(shape, lambda b: (0,) * len(shape))
    row = pl.BlockSpec((mem_len, d), lambda b: (b, 0))
    return pl.pallas_call(
        _memkv_kernel,
        grid=(batch,),
        in_specs=[row, full((1, d)), full((d, 2 * d)), full((1, d // XA_HEADS))],
        out_specs=(row, row),
        out_shape=(jax.ShapeDtypeStruct(mem2d.shape, BF16), jax.ShapeDtypeStruct(mem2d.shape, BF16)),
        compiler_params=_cparams(("parallel",)),
        name="mem_kv",
    )(mem2d, gain, w_xkv, xk_gain)


def _mid_kernel(x_ref, of_ref, og_ref, wo_ref, xg_ref, wq_ref, qg_ref, k_ref, v_ref, wxo_ref, mg_ref,
                wr_ref, br_ref,
                x2_ref, h3_ref, mi_ref, mw_ref, cnt_ref, carry_ref, *, tm):
    i = pl.program_id(0)
    d = x_ref.shape[-1]
    hd = d // XA_HEADS
    x1 = (x_ref[...]
          + jnp.dot(of_ref[...], wo_ref[0:FOX_WIDTH, :], preferred_element_type=F32)
          + jnp.dot(og_ref[...], wo_ref[FOX_WIDTH:, :], preferred_element_type=F32))
    h2 = _rms(x1, xg_ref[...]).astype(BF16)
    q = jnp.dot(h2, wq_ref[...], preferred_element_type=F32)
    heads = []
    for h in range(XA_HEADS):
        sl = slice(h * hd, (h + 1) * hd)
        qn = (_rms(q[:, sl], qg_ref[...]) * hd ** -0.5).astype(BF16)
        s = lax.dot_general(qn, k_ref[:, sl], NT_DIMS, preferred_element_type=F32)
        p = jnp.exp(s - jnp.max(s, axis=-1, keepdims=True))
        p = p * (1.0 / jnp.sum(p, axis=-1, keepdims=True))
        heads.append(jnp.dot(p.astype(BF16), v_ref[:, sl], preferred_element_type=F32).astype(BF16))
    x2 = x1 + jnp.dot(jnp.concatenate(heads, axis=-1), wxo_ref[...], preferred_element_type=F32)
    x2_ref[...] = x2
    h3 = _rms(x2, mg_ref[...])
    for cc in range(SUBLANES):
        h3_ref[pl.ds(cc, tm, stride=SUBLANES), :] = h3[:, cc * LANES:(cc + 1) * LANES]
    h_hi, h_lo = _split_bf16(h3)
    logits = (jnp.dot(h_hi, wr_ref[0], preferred_element_type=F32)
              + jnp.dot(h_lo, wr_ref[0], preferred_element_type=F32)
              + jnp.dot(h_hi, wr_ref[1], preferred_element_type=F32)) + br_ref[...]
    lane = lax.broadcasted_iota(jnp.int32, (tm, LANES), 1)
    work = logits
    vals, idxs = [], []
    onehot = jnp.zeros((tm, LANES), F32)
    for _ in range(TOP_K):
        mx = jnp.max(work, axis=-1, keepdims=True)
        idx = jnp.min(jnp.where(work == mx, lane, LANES), axis=-1, keepdims=True)
        sel = lane == idx
        onehot = jnp.where(sel, 1.0, onehot)
        work = jnp.where(sel, -jnp.inf, work)
        vals.append(mx)
        idxs.append(idx)
    es = [jnp.exp(v - vals[0]) for v in vals]
    inv_denom = 1.0 / (es[0] + es[1] + es[2] + es[3])

    @pl.when(i == 0)
    def _():
        carry_ref[...] = jnp.zeros_like(carry_ref)

    r = lax.broadcasted_iota(jnp.int32, (tm, tm), 0)
    c = lax.broadcasted_iota(jnp.int32, (tm, tm), 1)
    before = jnp.dot((c < r).astype(BF16), onehot.astype(BF16), preferred_element_type=F32) + carry_ref[...]
    mi = jnp.zeros((tm, LANES), F32)
    mw = jnp.zeros((tm, LANES), F32)
    for kk in range(TOP_K):
        rank = jnp.sum(jnp.where(lane == idxs[kk], before, 0.0), axis=-1, keepdims=True)
        mi = jnp.where(lane == kk, idxs[kk].astype(F32), mi)
        mi = jnp.where(lane == TOP_K + kk, rank, mi)
        mw = jnp.where(lane == kk, es[kk] * inv_denom, mw)
    mi_ref[...] = mi.T[0:2 * TOP_K, :].astype(jnp.int32)
    mw_ref[...] = mw
    total = carry_ref[...] + jnp.sum(onehot, axis=0, keepdims=True)
    carry_ref[...] = total
    cnt_ref[...] = jnp.broadcast_to(total, cnt_ref.shape).astype(jnp.int32)


def _mid(x2d, o_fox, o_gdn, w_out, xg, w_xq, xq_gain, kx, vx, w_xo, mg, w_r, b_r, *, seq, mem_len, tm):
    n, d = x2d.shape
    full = lambda shape: pl.BlockSpec(shape, lambda i: (0,) * len(shape))
    rows = lambda width: pl.BlockSpec((tm, width), lambda i: (i, 0))
    mem = pl.BlockSpec((mem_len, d), lambda i: (i // (seq // tm), 0))
    return pl.pallas_call(
        functools.partial(_mid_kernel, tm=tm),
        grid=(n // tm,),
        in_specs=[rows(d), rows(FOX_WIDTH), rows(GDN_WIDTH), full((d, d)), full((1, d)), full((d, d)),
                  full((1, d // XA_HEADS)), mem, mem, full((d, d)), full((1, d)), full((2, d, LANES)),
                  full((1, LANES))],
        out_specs=(rows(d), pl.BlockSpec((tm * d // LANES, LANES), lambda i: (i, 0)),
                   pl.BlockSpec((2 * TOP_K, tm), lambda i: (0, i)), rows(LANES), full((SUBLANES, LANES))),
        out_shape=(jax.ShapeDtypeStruct((n, d), F32), jax.ShapeDtypeStruct((n * d // LANES, LANES), F32),
                   jax.ShapeDtypeStruct((2 * TOP_K, n), jnp.int32), jax.ShapeDtypeStruct((n, LANES), F32),
                   jax.ShapeDtypeStruct((SUBLANES, LANES), jnp.int32)),
        scratch_shapes=[pltpu.VMEM((1, LANES), F32)],
        compiler_params=_cparams(("arbitrary",)),
        name="outproj_xattn_router",
    )(x2d, o_fox, o_gdn, w_out, xg, w_xq, xq_gain, kx, vx, w_xo, mg, w_r, b_r)


MOE_TM = 512
DISPATCH_TOKENS = 512


def _dispatch_kernel(pos_ref, src_ref, dst_ref, sem):
    nt = DISPATCH_TOKENS

    def copy(t, kk):
        src = src_ref.at[pl.ds(pl.multiple_of(t * SUBLANES, SUBLANES), SUBLANES), :]
        row = pl.multiple_of(pos_ref[0, kk * nt + t], SUBLANES)
        return pltpu.make_async_copy(src, dst_ref.at[pl.ds(row, SUBLANES), :], sem)

    def issue(t, carry):
        for kk in range(TOP_K):
            copy(t, kk).start(priority=kk % 2)
        return carry

    lax.fori_loop(0, nt, issue, 0, unroll=4)

    def drain(t, carry):
        for kk in range(TOP_K):
            copy(t, kk).wait()
        return carry

    lax.fori_loop(0, nt, drain, 0, unroll=4)


def _dispatch(h3_tiles, pos, *, rows_padded):
    n = h3_tiles.shape[0] // SUBLANES
    nt = DISPATCH_TOKENS
    return pl.pallas_call(
        _dispatch_kernel,
        grid=(n // nt,),
        in_specs=[pl.BlockSpec((None, 1, nt * TOP_K), lambda i: (i, 0, 0), memory_space=pltpu.SMEM),
                  pl.BlockSpec((nt * SUBLANES, LANES), lambda i: (i, 0))],
        out_specs=pl.BlockSpec(memory_space=pl.ANY),
        out_shape=jax.ShapeDtypeStruct((rows_padded * SUBLANES, LANES), F32),
        scratch_shapes=[pltpu.SemaphoreType.DMA],
        compiler_params=pltpu.CompilerParams(dimension_semantics=("arbitrary",), has_side_effects=True),
        name="moe_dispatch",
    )(pos, h3_tiles)


def _expert_kernel(te_ref, nv_ref, xs_ref, wgu_ref, bgu_ref, wd_ref, bd_ref, y_ref, wgu_bf, wd_bf, acc_ref,
                   *, chunk):
    i = pl.program_id(0)
    tm = xs_ref.shape[0] // SUBLANES
    d = wd_bf.shape[1]
    f = wd_bf.shape[0]
    nvalid = nv_ref[i]
    first = jnp.logical_or(i == 0, te_ref[i] != te_ref[jnp.maximum(i - 1, 0)])

    @pl.when(jnp.logical_and(first, nvalid > 0))
    def _():
        wgu_bf[...] = wgu_ref[0].astype(BF16)
        wd_bf[...] = wd_ref[0].astype(BF16)

    @pl.when(nvalid > 0)
    def _():
        row = lax.broadcasted_iota(jnp.int32, (tm, 1), 0)
        x = jnp.concatenate([xs_ref[pl.ds(cc, tm, stride=SUBLANES), :] for cc in range(SUBLANES)], axis=-1)
        x = jnp.where(row < nvalid, x, 0.0).astype(BF16)
        for j in range(f // chunk):
            cs = slice(j * chunk, (j + 1) * chunk)
            us = slice(f + j * chunk, f + (j + 1) * chunk)
            g = jnp.dot(x, wgu_bf[:, cs], preferred_element_type=F32) + bgu_ref[0, :, cs]
            u = jnp.dot(x, wgu_bf[:, us], preferred_element_type=F32) + bgu_ref[0, :, us]
            gate = jnp.minimum(g, SWIGLU_LIMIT)
            up = jnp.clip(u, -SWIGLU_LIMIT, SWIGLU_LIMIT)
            act = ((up + 1.0) * (gate * _sigmoid(SWIGLU_ALPHA * gate))).astype(BF16)
            part = jnp.dot(act, wd_bf[cs, :], preferred_element_type=F32)
            if j == 0:
                acc_ref[...] = part + bd_ref[0]
            else:
                acc_ref[...] += part
        for cc in range(SUBLANES):
            y_ref[pl.ds(cc, tm, stride=SUBLANES), :] = acc_ref[:, cc * LANES:(cc + 1) * LANES]

    @pl.when(nvalid <= 0)
    def _():
        y_ref[...] = jnp.zeros_like(y_ref)


def _experts(tile_expert, tile_valid, xs, w_gate_up, b_gate_up, w_down, b_down):
    rows_padded = xs.shape[0] // SUBLANES
    e, d, f2 = w_gate_up.shape
    f = f2 // 2
    tm = MOE_TM
    grid_spec = pltpu.PrefetchScalarGridSpec(
        num_scalar_prefetch=2,
        grid=(rows_padded // tm,),
        in_specs=[
            pl.BlockSpec((tm * SUBLANES, LANES), lambda i, te, nv: (i, 0)),
            pl.BlockSpec((1, d, f2), lambda i, te, nv: (te[i], 0, 0)),
            pl.BlockSpec((1, 1, f2), lambda i, te, nv: (te[i], 0, 0)),
            pl.BlockSpec((1, f, d), lambda i, te, nv: (te[i], 0, 0)),
            pl.BlockSpec((1, 1, d), lambda i, te, nv: (te[i], 0, 0)),
        ],
        out_specs=pl.BlockSpec((tm * SUBLANES, LANES), lambda i, te, nv: (i, 0)),
        scratch_shapes=[pltpu.VMEM((d, f2), BF16), pltpu.VMEM((f, d), BF16), pltpu.VMEM((tm, d), F32)],
    )
    return pl.pallas_call(
        functools.partial(_expert_kernel, chunk=512),
        grid_spec=grid_spec,
        out_shape=jax.ShapeDtypeStruct(xs.shape, F32),
        compiler_params=_cparams(("arbitrary",)),
        name="moe_experts",
    )(tile_expert, tile_valid, xs, w_gate_up, b_gate_up.reshape(e, 1, f2), w_down, b_down.reshape(e, 1, d))


COMBINE_TOKENS = 256


def _combine_kernel(pos_ref, y_ref, x2_ref, mw_ref, o_ref, ybuf, sems):
    nt = COMBINE_TOKENS
    half = nt // 2

    def copy(t, kk, hf):
        row = pl.multiple_of(pos_ref[0, kk * nt + t], SUBLANES)
        dst = ybuf.at[kk, pl.ds(pl.multiple_of(t * SUBLANES, SUBLANES), SUBLANES), :]
        return pltpu.make_async_copy(y_ref.at[pl.ds(row, SUBLANES), :], dst, sems.at[hf])

    def issue_half(hf):
        def issue(t, carry):
            for kk in range(TOP_K):
                copy(t, kk, hf).start(priority=kk % 2)
            return carry

        lax.fori_loop(hf * half, (hf + 1) * half, issue, 0, unroll=4)

    def drain_half(hf):
        def drain(t, carry):
            for kk in range(TOP_K):
                copy(t, kk, hf).wait()
            return carry

        lax.fori_loop(hf * half, (hf + 1) * half, drain, 0, unroll=4)

    def combine_half(hf):
        rows = slice(hf * half, (hf + 1) * half)
        mw = mw_ref[rows, :]
        for cc in range(SUBLANES):
            cs = slice(cc * LANES, (cc + 1) * LANES)
            acc = x2_ref[rows, cs]
            for kk in range(TOP_K):
                ys = ybuf.at[kk][pl.ds(hf * half * SUBLANES + cc, half, stride=SUBLANES), :]
                acc = acc + mw[:, kk:kk + 1] * ys
            o_ref[rows, cs] = acc

    issue_half(0)
    issue_half(1)
    drain_half(0)
    combine_half(0)
    drain_half(1)
    combine_half(1)


def _combine(pos, y, x2, mw):
    n, d = x2.shape
    nt = COMBINE_TOKENS
    return pl.pallas_call(
        _combine_kernel,
        grid=(n // nt,),
        in_specs=[pl.BlockSpec((None, 1, nt * TOP_K), lambda i: (i, 0, 0), memory_space=pltpu.SMEM),
                  pl.BlockSpec(memory_space=pl.ANY),
                  pl.BlockSpec((nt, d), lambda i: (i, 0)),
                  pl.BlockSpec((nt, LANES), lambda i: (i, 0))],
        out_specs=pl.BlockSpec((nt, d), lambda i: (i, 0)),
        out_shape=jax.ShapeDtypeStruct((n, d), F32),
        scratch_shapes=[pltpu.VMEM((TOP_K, nt * SUBLANES, LANES), F32), pltpu.SemaphoreType.DMA((2,))],
        compiler_params=_cparams(("arbitrary",)),
        name="moe_combine",
    )(pos, y, x2, mw)


def _layer(x, mem, mix_norm, w_in, b_forget, fox_q_gain, fox_k_gain, fox_out_gain, gdn_conv_w, gdn_a_log,
           gdn_dt_bias, gdn_out_gain, w_out, xattn_norm, mem_norm, w_xq, w_xkv, xq_gain, xk_gain, w_xo,
           moe_norm, w_router, b_router, w_gate_up, b_gate_up, w_down, b_down):
    batch, seq, d = x.shape
    n = batch * seq
    mem_len = mem.shape[1]
    x2d = x.reshape(n, d)

    o_ff = 3 * FOX_WIDTH
    o_gq = o_ff + FOX_HEADS
    o_ga = o_gq + 3 * GDN_WIDTH
    o_gb = o_ga + GDN_HEADS
    o_gz = o_gb + GDN_HEADS
    w_main = jnp.concatenate([w_in[:, :o_ff], w_in[:, o_gq:o_ga], w_in[:, o_gz:]], axis=1).astype(BF16)
    w_small = jnp.concatenate([w_in[:, o_ff:o_gq], w_in[:, o_ga:o_gz],
                               jnp.zeros((d, LANES - SM_END), F32)], axis=1).astype(BF16)
    prow = jnp.zeros((SUBLANES, LANES), F32)
    prow = prow.at[0, SM_F0:SM_G0].set(b_forget).at[0, SM_G0:SM_B0].set(gdn_dt_bias)
    prow = prow.at[1, SM_G0:SM_B0].set(gdn_a_log)
    pcol = prow[0:2].T
    head_id = jnp.arange(FOX_WIDTH) // FOX_HEAD_DIM
    bd = (head_id[:, None] == head_id[None, :]).astype(BF16)
    row1 = lambda v: v.reshape(1, -1)

    tm = min(512, seq)
    fq, fk, fv, gqkv, gz, sm, smt = _inproj(
        x2d, row1(mix_norm), w_main, w_small, w_small.T, bd,
        row1(jnp.tile(fox_q_gain, FOX_HEADS)), row1(jnp.tile(fox_k_gain, FOX_HEADS)), prow, pcol,
        seq=seq, tm=tm)

    cum = smt[SM_F0:SM_G0].reshape(FOX_HEADS // 2, 2, batch, seq).transpose(2, 0, 1, 3)
    b3 = lambda a: a.reshape(batch, seq, a.shape[-1])
    o_fox = _fox_attention(b3(fq), b3(fk), b3(fv), cum, fox_out_gain.reshape(FOX_HEADS // 2, 1, LANES),
                           batch=batch, seq=seq, blk=min(512, seq))

    u, w, qd, qk, kdt, egl = _gdn_prep(gqkv, gdn_conv_w.T, sm, smt, batch=batch, seq=seq)
    o_gdn = _gdn_scan(u, w, qd, qk, kdt, egl, b3(gz), row1(gdn_out_gain), batch=batch, seq=seq)

    kx, vx = _memkv(mem.reshape(batch * mem_len, d), row1(mem_norm), w_xkv.astype(BF16), row1(xk_gain),
                    batch=batch, mem_len=mem_len)
    w_r = jnp.concatenate([w_router, jnp.zeros((d, LANES - N_EXPERTS), F32)], axis=1)
    w_r = jnp.stack(_split_bf16(w_r))
    b_r = jnp.concatenate([b_router, jnp.full((LANES - N_EXPERTS,), -jnp.inf, F32)]).reshape(1, LANES)
    x2, h3, mi, mw, cnt = _mid(
        x2d, o_fox.reshape(n, FOX_WIDTH), o_gdn.reshape(n, GDN_WIDTH), w_out.astype(BF16), row1(xattn_norm),
        w_xq.astype(BF16), row1(xq_gain), kx, vx, w_xo.astype(BF16), row1(moe_norm), w_r, b_r,
        seq=seq, mem_len=mem_len, tm=min(512, seq))

    counts = cnt[0, :N_EXPERTS]
    padded = ((counts + MOE_TM - 1) // MOE_TM) * MOE_TM
    ends = jnp.cumsum(padded)
    starts = ends - padded
    rows_padded = n * TOP_K + N_EXPERTS * MOE_TM
    n_tiles = rows_padded // MOE_TM
    tile_start = jnp.arange(n_tiles, dtype=jnp.int32) * MOE_TM
    tile_expert = jnp.sum((ends[None, :] <= tile_start[:, None]).astype(jnp.int32), axis=1)
    tile_expert = jnp.minimum(tile_expert, N_EXPERTS - 1)
    tile_valid = jnp.clip(starts[tile_expert] + counts[tile_expert] - tile_start, 0, MOE_TM).astype(jnp.int32)
    last_used = jnp.max(jnp.where(tile_valid > 0, tile_expert, 0))
    tile_expert = jnp.where(tile_valid > 0, tile_expert, last_used).astype(jnp.int32)

    pos = ((starts[mi[:TOP_K]] + mi[TOP_K:]) * SUBLANES).astype(jnp.int32)

    def step_tables(nt):
        return pos.reshape(TOP_K, n // nt, nt).transpose(1, 0, 2).reshape(n // nt, 1, TOP_K * nt)

    xs = _dispatch(h3, step_tables(DISPATCH_TOKENS), rows_padded=rows_padded)
    y = _experts(tile_expert, tile_valid, xs, w_gate_up, b_gate_up, w_down, b_down)
    out = _combine(step_tables(COMBINE_TOKENS), y, x2, mw)
    return out.reshape(batch, seq, d)


def kernel(x, mem, mix_norm, w_in, b_forget, fox_q_gain, fox_k_gain, fox_out_gain, gdn_conv_w, gdn_a_log,
           gdn_dt_bias, gdn_out_gain, w_out, xattn_norm, mem_norm, w_xq, w_xkv, xq_gain, xk_gain, w_xo,
           moe_norm, w_router, b_router, w_gate_up, b_gate_up, w_down, b_down):
    depth = mix_norm.shape[0]
    for l in range(depth):
        x = _layer(x, mem, mix_norm[l], w_in[l], b_forget[l], fox_q_gain[l], fox_k_gain[l], fox_out_gain[l],
                   gdn_conv_w[l], gdn_a_log[l], gdn_dt_bias[l], gdn_out_gain[l], w_out[l], xattn_norm[l],
                   mem_norm[l], w_xq[l], w_xkv[l], xq_gain[l], xk_gain[l], w_xo[l], moe_norm[l], w_router[l],
                   b_router[l], w_gate_up[l], b_gate_up[l], w_down[l], b_down[l])
    return x
```

```python
import functools

import jax
import jax.numpy as jnp
from jax import lax
from jax.experimental import pallas as pl
from jax.experimental.pallas import tpu as pltpu

F32 = jnp.float32
BF16 = jnp.bfloat16

EPS = 1e-6
FOX_HEADS, FOX_HEAD_DIM = 8, 64
GDN_HEADS, GDN_HEAD_DIM = 4, 128
FOX_WIDTH = FOX_HEADS * FOX_HEAD_DIM
GDN_WIDTH = GDN_HEADS * GDN_HEAD_DIM
CONV_WIDTH = 4
CHUNK = 64
XA_HEADS = 4
N_EXPERTS = 32
TOP_K = 4
SWIGLU_LIMIT = 7.0
SWIGLU_ALPHA = 1.702
LOG2E = 1.4426950408889634

LANES = 128
SUBLANES = 8
VMEM_LIMIT = 52 * 1024 * 1024

SM_F0, SM_G0, SM_B0, SM_END = 0, 8, 12, 16

NT_DIMS = (((1,), (1,)), ((), ()))


def _cparams(sem):
    return pltpu.CompilerParams(dimension_semantics=sem, vmem_limit_bytes=VMEM_LIMIT)


def _rms(x, gain):
    return x * lax.rsqrt(jnp.mean(x * x, axis=-1, keepdims=True) + EPS) * gain


def _split_bf16(x):
    hi = x.astype(BF16)
    return hi, (x - hi.astype(F32)).astype(BF16)


def _dot_mask_left(mask, x):
    hi, lo = _split_bf16(x)
    mb = mask.astype(BF16)
    return jnp.dot(mb, hi, preferred_element_type=F32) + jnp.dot(mb, lo, preferred_element_type=F32)


def _dot_mask_right(x, mask):
    hi, lo = _split_bf16(x)
    mb = mask.astype(BF16)
    return jnp.dot(hi, mb, preferred_element_type=F32) + jnp.dot(lo, mb, preferred_element_type=F32)


def _softplus(t):
    return jnp.maximum(t, 0.0) + jnp.log1p(jnp.exp(-jnp.abs(t)))


def _sigmoid(t):
    return 1.0 / (1.0 + jnp.exp(-t))


def _small_act(v, bias, alog, idx):
    t = v + bias
    tail = jnp.log1p(jnp.exp(-jnp.abs(t)))
    log_f = jnp.minimum(t, 0.0) - tail
    g = -jnp.exp(alog) * (jnp.maximum(t, 0.0) + tail)
    beta = _sigmoid(v)
    return jnp.where(idx < SM_G0, log_f, jnp.where(idx < SM_B0, g, jnp.where(idx < SM_END, beta, 0.0)))


def _inproj_kernel(x_ref, gain_ref, w_ref, ws_ref, wst_ref, bd_ref, qg_ref, kg_ref, prow_ref, pcol_ref,
                   fq_ref, fk_ref, fv_ref, gqkv_ref, gz_ref, sm_ref, smt_ref, carry_r, carry_c,
                   *, tm, tiles_per_seq):
    i = pl.program_id(0)
    hb = _rms(x_ref[...], gain_ref[...]).astype(BF16)

    def proj(lo, hi):
        return jnp.dot(hb, w_ref[:, lo:hi], preferred_element_type=F32)

    def headnorm(p, g):
        ss = jnp.dot((p * p).astype(BF16), bd_ref[...], preferred_element_type=F32)
        return p * lax.rsqrt(ss * (1.0 / FOX_HEAD_DIM) + EPS) * g

    w0 = FOX_WIDTH
    fq_ref[...] = (headnorm(proj(0, w0), qg_ref[...]) * (FOX_HEAD_DIM ** -0.5 * LOG2E)).astype(BF16)
    fk_ref[...] = headnorm(proj(w0, 2 * w0), kg_ref[...]).astype(BF16)
    fv_ref[...] = proj(2 * w0, 3 * w0).astype(BF16)
    g0 = 3 * w0
    gqkv_ref[...] = proj(g0, g0 + 3 * GDN_WIDTH)
    gz_ref[...] = proj(g0 + 3 * GDN_WIDTH, g0 + 4 * GDN_WIDTH)

    @pl.when(i % tiles_per_seq == 0)
    def _():
        carry_r[...] = jnp.zeros_like(carry_r)
        carry_c[...] = jnp.zeros_like(carry_c)

    sm = jnp.dot(hb, ws_ref[...], preferred_element_type=F32)
    smt = lax.dot_general(wst_ref[...], hb, NT_DIMS, preferred_element_type=F32)
    lane = lax.broadcasted_iota(jnp.int32, (1, LANES), 1)
    srow = lax.broadcasted_iota(jnp.int32, (LANES, 1), 0)
    vals = _small_act(sm, prow_ref[0:1, :], prow_ref[1:2, :], lane)
    vals_t = _small_act(smt, pcol_ref[:, 0:1], pcol_ref[:, 1:2], srow)

    r = lax.broadcasted_iota(jnp.int32, (tm, tm), 0)
    c = lax.broadcasted_iota(jnp.int32, (tm, tm), 1)
    cum = _dot_mask_left(c <= r, vals) + carry_r[...]
    cum_t = _dot_mask_right(vals_t, r <= c) + carry_c[...]
    carry_r[...] = cum[tm - 1:tm, :]
    carry_c[...] = cum_t[:, tm - 1:tm]
    sm_ref[...] = jnp.where(lane < SM_G0, cum, vals)
    smt_ref[...] = jnp.where(srow < SM_G0, cum_t, vals_t)


def _inproj(x2d, gain, w_main, w_small, w_small_t, bd, qg, kg, prow, pcol, *, seq, tm):
    n, d = x2d.shape
    wm = w_main.shape[1]
    full = lambda shape: pl.BlockSpec(shape, lambda i: (0,) * len(shape))
    rows = lambda width: pl.BlockSpec((tm, width), lambda i: (i, 0))
    out_shape = (
        jax.ShapeDtypeStruct((n, FOX_WIDTH), BF16),
        jax.ShapeDtypeStruct((n, FOX_WIDTH), BF16),
        jax.ShapeDtypeStruct((n, FOX_WIDTH), BF16),
        jax.ShapeDtypeStruct((n, 3 * GDN_WIDTH), F32),
        jax.ShapeDtypeStruct((n, GDN_WIDTH), F32),
        jax.ShapeDtypeStruct((n, LANES), F32),
        jax.ShapeDtypeStruct((LANES, n), F32),
    )
    return pl.pallas_call(
        functools.partial(_inproj_kernel, tm=tm, tiles_per_seq=seq // tm),
        grid=(n // tm,),
        in_specs=[rows(d), full((1, d)), full((d, wm)), full((d, LANES)), full((LANES, d)),
                  full((FOX_WIDTH, FOX_WIDTH)), full((1, FOX_WIDTH)), full((1, FOX_WIDTH)),
                  full((SUBLANES, LANES)), full((LANES, 2))],
        out_specs=(rows(FOX_WIDTH), rows(FOX_WIDTH), rows(FOX_WIDTH), rows(3 * GDN_WIDTH), rows(GDN_WIDTH),
                   rows(LANES), pl.BlockSpec((LANES, tm), lambda i: (0, i))),
        out_shape=out_shape,
        scratch_shapes=[pltpu.VMEM((1, LANES), F32), pltpu.VMEM((LANES, 1), F32)],
        compiler_params=_cparams(("arbitrary",)),
        name="inproj",
    )(x2d, gain, w_main, w_small, w_small_t, bd, qg, kg, prow, pcol)


def _fox_kernel(q_ref, k_ref, v_ref, cum_ref, gain_ref, o_ref, m0_ref, m1_ref, acc0_ref, acc1_ref, *, blk):
    qi = pl.program_id(2)
    lane = lax.broadcasted_iota(jnp.int32, (1, LANES), 1)
    lo = lane < FOX_HEAD_DIM
    q = q_ref[...]
    zero = jnp.zeros_like(q)
    q_heads = (jnp.where(lo, q, zero), jnp.where(lo, zero, q))
    q0 = pl.multiple_of(qi * blk, blk)
    c_ref = [cum_ref[hh:hh + 1, pl.ds(q0, LANES)][:, 0:1] for hh in range(2)]
    head_lanes = (lo, jnp.logical_not(lo))
    sum_lane = (FOX_HEAD_DIM, 0)
    ones_col = [jnp.where(lane == sum_lane[hh], 1.0, 0.0).astype(BF16) for hh in range(2)]
    m_refs = (m0_ref, m1_ref)
    acc_refs = (acc0_ref, acc1_ref)
    for hh in range(2):
        m_refs[hh][...] = jnp.full_like(m_refs[hh], -1e30)
        acc_refs[hh][...] = jnp.zeros_like(acc_refs[hh])

    def step(k0, masked):
        kb = k_ref[pl.ds(k0, blk), :]
        vb = v_ref[pl.ds(k0, blk), :]
        scores = [lax.dot_general(q_heads[hh], kb, NT_DIMS, preferred_element_type=F32) for hh in range(2)]
        for hh in range(2):
            s = scores[hh] + (c_ref[hh] - cum_ref[hh:hh + 1, pl.ds(k0, blk)]) * LOG2E
            if masked:
                r = lax.broadcasted_iota(jnp.int32, (blk, blk), 0)
                c = lax.broadcasted_iota(jnp.int32, (blk, blk), 1)
                s = jnp.where(c <= r, s, -jnp.inf)
            m_old = m_refs[hh][...]
            m_new = jnp.maximum(m_old, jnp.max(s, axis=-1, keepdims=True))
            alpha = jnp.exp2(m_old - m_new)
            p = jnp.exp2(s - jnp.concatenate([m_new] * (blk // LANES), axis=1))
            v_h = jnp.where(head_lanes[hh], vb, ones_col[hh])
            pv = jnp.dot(p.astype(BF16), v_h, preferred_element_type=F32)
            acc_refs[hh][...] = acc_refs[hh][...] * alpha + pv
            m_refs[hh][...] = m_new

    def body(kp, carry):
        step(pl.multiple_of(kp * (2 * blk), blk), False)
        step(pl.multiple_of(kp * (2 * blk) + blk, blk), False)
        return carry

    lax.fori_loop(0, qi // 2, body, 0)

    @pl.when(qi % 2 == 1)
    def _():
        step(pl.multiple_of((qi - 1) * blk, blk), False)

    step(q0, True)

    acc0, acc1 = acc0_ref[...], acc1_ref[...]
    l0 = acc0[:, sum_lane[0]:sum_lane[0] + 1]
    l1 = acc1[:, sum_lane[1]:sum_lane[1] + 1]
    o = jnp.where(lo, acc0 * (1.0 / l0), acc1 * (1.0 / l1))
    o2 = o * o
    ss_lo = jnp.sum(jnp.where(lo, o2, 0.0), axis=-1, keepdims=True)
    ss_hi = jnp.sum(jnp.where(lo, 0.0, o2), axis=-1, keepdims=True)
    ms = jnp.where(lo, ss_lo, ss_hi) * (1.0 / FOX_HEAD_DIM)
    o_ref[...] = (o * lax.rsqrt(ms + EPS) * gain_ref[...]).astype(o_ref.dtype)


def _fox_attention(fq, fk, fv, cum, gain_pairs, *, batch, seq, blk):
    npairs = FOX_HEADS // 2
    return pl.pallas_call(
        functools.partial(_fox_kernel, blk=blk),
        grid=(batch, npairs, seq // blk),
        in_specs=[
            pl.BlockSpec((None, blk, LANES), lambda b, j, i: (b, i, j)),
            pl.BlockSpec((None, seq, LANES), lambda b, j, i: (b, 0, j)),
            pl.BlockSpec((None, seq, LANES), lambda b, j, i: (b, 0, j)),
            pl.BlockSpec((None, None, 2, seq), lambda b, j, i: (b, j, 0, 0)),
            pl.BlockSpec((None, 1, LANES), lambda b, j, i: (j, 0, 0)),
        ],
        out_specs=pl.BlockSpec((None, blk, LANES), lambda b, j, i: (b, i, j)),
        out_shape=jax.ShapeDtypeStruct((batch, seq, FOX_WIDTH), BF16),
        scratch_shapes=[pltpu.VMEM((blk, LANES), F32) for _ in range(4)],
        compiler_params=_cparams(("parallel", "parallel", "arbitrary")),
        name="fox_attention",
    )(fq, fk, fv, cum, gain_pairs)


GDN_BLK = 4 * CHUNK
QK_BLK = 2 * CHUNK


def _gdn_prep_kernel(x_ref, halo_ref, cw_ref, sm_ref, smt_ref,
                     u_ref, w_ref, qd_ref, qk_ref, kdt_ref, egl_ref, xpad_ref):
    i = pl.program_id(1)
    nb = GDN_BLK
    halo = halo_ref[...]
    xpad_ref[0:SUBLANES, :] = jnp.where(i > 0, halo, jnp.zeros_like(halo))
    xpad_ref[SUBLANES:, :] = x_ref[...]
    y = None
    for j in range(CONV_WIDTH):
        start = SUBLANES - (CONV_WIDTH - 1) + j
        term = cw_ref[j:j + 1, :] * xpad_ref[start:start + nb, :]
        y = term if y is None else y + term
    y = y * _sigmoid(y)

    r = lax.broadcasted_iota(jnp.int32, (nb, nb), 0)
    c = lax.broadcasted_iota(jnp.int32, (nb, nb), 1)
    chunk_shift = CHUNK.bit_length() - 1
    same = jnp.right_shift(r, chunk_shift) == jnp.right_shift(c, chunk_shift)
    incl = same & (c <= r)
    strict = same & (c < r)
    sm = sm_ref[...]
    g_cum = _dot_mask_left(incl, sm)
    g_tot = _dot_mask_left(same, sm)
    g_cum_t = _dot_mask_right(smt_ref[...], same & (r <= c))
    eye = (r == c).astype(F32)

    powers, t_invs, rhs = [], [], []
    for h in range(GDN_HEADS):
        sl = slice(h * GDN_HEAD_DIM, (h + 1) * GDN_HEAD_DIM)
        q = y[:, h * GDN_HEAD_DIM:(h + 1) * GDN_HEAD_DIM]
        k = y[:, GDN_WIDTH + h * GDN_HEAD_DIM:GDN_WIDTH + (h + 1) * GDN_HEAD_DIM]
        v = y[:, 2 * GDN_WIDTH + h * GDN_HEAD_DIM:2 * GDN_WIDTH + (h + 1) * GDN_HEAD_DIM]
        qn = q * lax.rsqrt(jnp.sum(q * q, axis=-1, keepdims=True) + EPS) * GDN_HEAD_DIM ** -0.5
        kn = k * lax.rsqrt(jnp.sum(k * k, axis=-1, keepdims=True) + EPS)
        gc = g_cum[:, SM_G0 + h:SM_G0 + h + 1]
        gl = g_tot[:, SM_G0 + h:SM_G0 + h + 1]
        gr = g_cum_t[SM_G0 + h:SM_G0 + h + 1, :]
        beta = sm[:, SM_B0 + h:SM_B0 + h + 1]
        decay = jnp.where(incl, jnp.exp(jnp.where(incl, gc - gr, 0.0)), 0.0)
        qb, kb = qn.astype(BF16), kn.astype(BF16)
        kk = lax.dot_general(kb, kb, NT_DIMS, preferred_element_type=F32)
        a = jnp.where(strict, beta * kk * decay, 0.0)
        eg = jnp.exp(gc)
        powers.append(a)
        t_invs.append(eye - a)
        rhs.append(jnp.concatenate([(v * beta).astype(BF16), (kn * (beta * eg)).astype(BF16)], axis=1))
        qk = lax.dot_general(qb, kb, NT_DIMS, preferred_element_type=F32)
        qk = jnp.where(incl, qk * decay, 0.0).astype(BF16)
        qk_ref[:, h * QK_BLK:(h + 1) * QK_BLK] = jnp.concatenate(
            [qk[j * QK_BLK:(j + 1) * QK_BLK, j * QK_BLK:(j + 1) * QK_BLK] for j in range(nb // QK_BLK)], axis=0)
        qd_ref[:, sl] = (qn * eg).astype(BF16)
        kdt_ref[sl, :] = (kn * jnp.exp(gl - gc)).T.astype(BF16)
        egl_ref[:, sl] = jnp.broadcast_to(jnp.exp(gl), (nb, GDN_HEAD_DIM))

    for _ in range(5):
        for h in range(GDN_HEADS):
            pb = powers[h].astype(BF16)
            powers[h] = jnp.dot(pb, pb, preferred_element_type=F32)
        for h in range(GDN_HEADS):
            t_invs[h] = t_invs[h] + jnp.dot(t_invs[h].astype(BF16), powers[h].astype(BF16),
                                            preferred_element_type=F32)
    for h in range(GDN_HEADS):
        sl = slice(h * GDN_HEAD_DIM, (h + 1) * GDN_HEAD_DIM)
        uw = jnp.dot(t_invs[h].astype(BF16), rhs[h], preferred_element_type=F32)
        u_ref[:, sl] = uw[:, :GDN_HEAD_DIM]
        w_ref[:, sl] = uw[:, GDN_HEAD_DIM:].astype(BF16)


def _gdn_prep(gqkv, conv_w_t, sm, smt, *, batch, seq):
    nb = GDN_BLK
    bps = seq // nb
    hps = nb // SUBLANES
    width = 3 * GDN_WIDTH
    row = lambda w: pl.BlockSpec((None, nb, w), lambda b, i: (b, i, 0))
    return pl.pallas_call(
        _gdn_prep_kernel,
        grid=(batch, bps),
        in_specs=[
            pl.BlockSpec((nb, width), lambda b, i: (b * bps + i, 0)),
            pl.BlockSpec((SUBLANES, width), lambda b, i: (jnp.maximum((b * bps + i) * hps - 1, 0), 0)),
            pl.BlockSpec((CONV_WIDTH, width), lambda b, i: (0, 0)),
            pl.BlockSpec((nb, LANES), lambda b, i: (b * bps + i, 0)),
            pl.BlockSpec((LANES, nb), lambda b, i: (0, b * bps + i)),
        ],
        out_specs=(row(GDN_WIDTH), row(GDN_WIDTH), row(GDN_WIDTH), row(GDN_HEADS * QK_BLK),
                   pl.BlockSpec((None, GDN_WIDTH, nb), lambda b, i: (b, 0, i)), row(GDN_WIDTH)),
        out_shape=(
            jax.ShapeDtypeStruct((batch, seq, GDN_WIDTH), F32),
            jax.ShapeDtypeStruct((batch, seq, GDN_WIDTH), BF16),
            jax.ShapeDtypeStruct((batch, seq, GDN_WIDTH), BF16),
            jax.ShapeDtypeStruct((batch, seq, GDN_HEADS * QK_BLK), BF16),
            jax.ShapeDtypeStruct((batch, GDN_WIDTH, seq), BF16),
            jax.ShapeDtypeStruct((batch, seq, GDN_WIDTH), F32),
        ),
        scratch_shapes=[pltpu.VMEM((nb + SUBLANES, width), F32)],
        compiler_params=_cparams(("parallel", "parallel")),
        name="gdn_prep",
    )(gqkv, gqkv, conv_w_t, sm, smt)


def _gdn_scan_kernel(u_ref, w_ref, qd_ref, qk_ref, kdt_ref, egl_ref, z_ref, gain_ref, o_ref, s_ref, vz_ref,
                     *, batch):
    pb = QK_BLK

    @pl.when(pl.program_id(0) == 0)
    def _():
        s_ref[...] = jnp.zeros_like(s_ref)

    vz_ref[...] = jnp.zeros_like(vz_ref)
    for cidx in range(GDN_BLK // CHUNK):
        rows = slice(cidx * CHUNK, (cidx + 1) * CHUNK)
        blk_rows = slice(cidx * CHUNK // pb * pb, (cidx * CHUNK // pb + 1) * pb)
        for b in range(batch):
            for h in range(GDN_HEADS):
                bh = b * GDN_HEADS + h
                sl = slice(h * GDN_HEAD_DIM, (h + 1) * GDN_HEAD_DIM)
                s_old = s_ref[bh]
                lhs1 = jnp.concatenate([w_ref[b, rows, sl], qd_ref[b, rows, sl]], axis=0)
                r1 = jnp.dot(lhs1, s_old.astype(BF16), preferred_element_type=F32)
                v_new = u_ref[b, rows, sl] - r1[:CHUNK]
                vz_ref[bh, rows, :] = v_new.astype(BF16)
                lhs2 = jnp.concatenate([qk_ref[b, rows, h * pb:(h + 1) * pb], kdt_ref[b, sl, blk_rows]], axis=0)
                r2 = jnp.dot(lhs2, vz_ref[bh, blk_rows, :], preferred_element_type=F32)
                vz_ref[bh, rows, :] = jnp.zeros((CHUNK, GDN_HEAD_DIM), BF16)
                last = egl_ref[b, (cidx + 1) * CHUNK - 1:(cidx + 1) * CHUNK, sl]
                s_ref[bh] = s_old * last + r2[CHUNK:]
                o = r1[CHUNK:] + r2[:CHUNK]
                z = z_ref[b, rows, sl]
                o_ref[b, rows, sl] = (_rms(o, gain_ref[...]) * (z * _sigmoid(z))).astype(o_ref.dtype)


def _gdn_scan(u, w, qd, qk, kdt, egl, z, gain, *, batch, seq):
    nb = GDN_BLK
    row = lambda width: pl.BlockSpec((batch, nb, width), lambda i: (0, i, 0))
    return pl.pallas_call(
        functools.partial(_gdn_scan_kernel, batch=batch),
        grid=(seq // nb,),
        in_specs=[row(GDN_WIDTH), row(GDN_WIDTH), row(GDN_WIDTH), row(GDN_HEADS * QK_BLK),
                  pl.BlockSpec((batch, GDN_WIDTH, nb), lambda i: (0, 0, i)), row(GDN_WIDTH), row(GDN_WIDTH),
                  pl.BlockSpec((1, GDN_HEAD_DIM), lambda i: (0, 0))],
        out_specs=row(GDN_WIDTH),
        out_shape=jax.ShapeDtypeStruct((batch, seq, GDN_WIDTH), BF16),
        scratch_shapes=[pltpu.VMEM((batch * GDN_HEADS, GDN_HEAD_DIM, GDN_HEAD_DIM), F32),
                        pltpu.VMEM((batch * GDN_HEADS, nb, GDN_HEAD_DIM), BF16)],
        compiler_params=_cparams(("arbitrary",)),
        name="gdn_scan",
    )(u, w, qd, qk, kdt, egl, z, gain)


def _memkv_kernel(m_ref, gain_ref, w_ref, kg_ref, k_ref, v_ref):
    d = m_ref.shape[-1]
    hd = d // XA_HEADS
    mb = _rms(m_ref[...], gain_ref[...]).astype(BF16)
    kv = jnp.dot(mb, w_ref[...], preferred_element_type=F32)
    for h in range(XA_HEADS):
        sl = slice(h * hd, (h + 1) * hd)
        k_ref[:, sl] = _rms(kv[:, sl], kg_ref[...]).astype(BF16)
    v_ref[...] = kv[:, d:].astype(BF16)


def _memkv(mem2d, gain, w_xkv, xk_gain, *, batch, mem_len):
    d = mem2d.shape[-1]
    full = lambda shape: pl.BlockSpec(shape, lambda b: (0,) * len(shape))
    row = pl.BlockSpec((mem_len, d), lambda b: (b, 0))
    return pl.pallas_call(
        _memkv_kernel,
        grid=(batch,),
        in_specs=[row, full((1, d)), full((d, 2 * d)), full((1, d // XA_HEADS))],
        out_specs=(row, row),
        out_shape=(jax.ShapeDtypeStruct(mem2d.shape, BF16), jax.ShapeDtypeStruct(mem2d.shape, BF16)),
        compiler_params=_cparams(("parallel",)),
        name="mem_kv",
    )(mem2d, gain, w_xkv, xk_gain)


def _mid_kernel(x_ref, of_ref, og_ref, wo_ref, xg_ref, wq_ref, qg_ref, k_ref, v_ref, wxo_ref, mg_ref,
                wr_ref, br_ref,
                x2_ref, h3_ref, mi_ref, mw_ref, cnt_ref, carry_ref, *, tm):
    i = pl.program_id(0)
    d = x_ref.shape[-1]
    hd = d // XA_HEADS
    x1 = (x_ref[...]
          + jnp.dot(of_ref[...], wo_ref[0:FOX_WIDTH, :], preferred_element_type=F32)
          + jnp.dot(og_ref[...], wo_ref[FOX_WIDTH:, :], preferred_element_type=F32))
    h2 = _rms(x1, xg_ref[...]).astype(BF16)
    q = jnp.dot(h2, wq_ref[...], preferred_element_type=F32)
    heads = []
    for h in range(XA_HEADS):
        sl = slice(h * hd, (h + 1) * hd)
        qn = (_rms(q[:, sl], qg_ref[...]) * hd ** -0.5).astype(BF16)
        s = lax.dot_general(qn, k_ref[:, sl], NT_DIMS, preferred_element_type=F32)
        p = jnp.exp(s - jnp.max(s, axis=-1, keepdims=True))
        p = p * (1.0 / jnp.sum(p, axis=-1, keepdims=True))
        heads.append(jnp.dot(p.astype(BF16), v_ref[:, sl], preferred_element_type=F32).astype(BF16))
    x2 = x1 + jnp.dot(jnp.concatenate(heads, axis=-1), wxo_ref[...], preferred_element_type=F32)
    x2_ref[...] = x2
    h3 = _rms(x2, mg_ref[...])
    for cc in range(SUBLANES):
        h3_ref[pl.ds(cc, tm, stride=SUBLANES), :] = h3[:, cc * LANES:(cc + 1) * LANES]
    h_hi, h_lo = _split_bf16(h3)
    logits = (jnp.dot(h_hi, wr_ref[0], preferred_element_type=F32)
              + jnp.dot(h_lo, wr_ref[0], preferred_element_type=F32)
              + jnp.dot(h_hi, wr_ref[1], preferred_element_type=F32)) + br_ref[...]
    lane = lax.broadcasted_iota(jnp.int32, (tm, LANES), 1)
    work = logits
    vals, idxs = [], []
    onehot = jnp.zeros((tm, LANES), F32)
    for _ in range(TOP_K):
        mx = jnp.max(work, axis=-1, keepdims=True)
        idx = jnp.min(jnp.where(work == mx, lane, LANES), axis=-1, keepdims=True)
        sel = lane == idx
        onehot = jnp.where(sel, 1.0, onehot)
        work = jnp.where(sel, -jnp.inf, work)
        vals.append(mx)
        idxs.append(idx)
    es = [jnp.exp(v - vals[0]) for v in vals]
    inv_denom = 1.0 / (es[0] + es[1] + es[2] + es[3])

    @pl.when(i == 0)
    def _():
        carry_ref[...] = jnp.zeros_like(carry_ref)

    r = lax.broadcasted_iota(jnp.int32, (tm, tm), 0)
    c = lax.broadcasted_iota(jnp.int32, (tm, tm), 1)
    before = jnp.dot((c < r).astype(BF16), onehot.astype(BF16), preferred_element_type=F32) + carry_ref[...]
    mi = jnp.zeros((tm, LANES), F32)
    mw = jnp.zeros((tm, LANES), F32)
    for kk in range(TOP_K):
        rank = jnp.sum(jnp.where(lane == idxs[kk], before, 0.0), axis=-1, keepdims=True)
        mi = jnp.where(lane == kk, idxs[kk].astype(F32), mi)
        mi = jnp.where(lane == TOP_K + kk, rank, mi)
        mw = jnp.where(lane == kk, es[kk] * inv_denom, mw)
    mi_ref[...] = mi.T[0:2 * TOP_K, :].astype(jnp.int32)
    mw_ref[...] = mw
    total = carry_ref[...] + jnp.sum(onehot, axis=0, keepdims=True)
    carry_ref[...] = total
    cnt_ref[...] = jnp.broadcast_to(total, cnt_ref.shape).astype(jnp.int32)


def _mid(x2d, o_fox, o_gdn, w_out, xg, w_xq, xq_gain, kx, vx, w_xo, mg, w_r, b_r, *, seq, mem_len, tm):
    n, d = x2d.shape
    full = lambda shape: pl.BlockSpec(shape, lambda i: (0,) * len(shape))
    rows = lambda width: pl.BlockSpec((tm, width), lambda i: (i, 0))
    mem = pl.BlockSpec((mem_len, d), lambda i: (i // (seq // tm), 0))
    return pl.pallas_call(
        functools.partial(_mid_kernel, tm=tm),
        grid=(n // tm,),
        in_specs=[rows(d), rows(FOX_WIDTH), rows(GDN_WIDTH), full((d, d)), full((1, d)), full((d, d)),
                  full((1, d // XA_HEADS)), mem, mem, full((d, d)), full((1, d)), full((2, d, LANES)),
                  full((1, LANES))],
        out_specs=(rows(d), pl.BlockSpec((tm * d // LANES, LANES), lambda i: (i, 0)),
                   pl.BlockSpec((2 * TOP_K, tm), lambda i: (0, i)), rows(LANES), full((SUBLANES, LANES))),
        out_shape=(jax.ShapeDtypeStruct((n, d), F32), jax.ShapeDtypeStruct((n * d // LANES, LANES), F32),
                   jax.ShapeDtypeStruct((2 * TOP_K, n), jnp.int32), jax.ShapeDtypeStruct((n, LANES), F32),
                   jax.ShapeDtypeStruct((SUBLANES, LANES), jnp.int32)),
        scratch_shapes=[pltpu.VMEM((1, LANES), F32)],
        compiler_params=_cparams(("arbitrary",)),
        name="outproj_xattn_router",
    )(x2d, o_fox, o_gdn, w_out, xg, w_xq, xq_gain, kx, vx, w_xo, mg, w_r, b_r)


MOE_TM = 512
DISPATCH_TOKENS = 512


def _dispatch_kernel(pos_ref, src_ref, dst_ref, sem):
    nt = DISPATCH_TOKENS

    def copy(t, kk):
        src = src_ref.at[pl.ds(pl.multiple_of(t * SUBLANES, SUBLANES), SUBLANES), :]
        row = pl.multiple_of(pos_ref[0, kk * nt + t], SUBLANES)
        return pltpu.make_async_copy(src, dst_ref.at[pl.ds(row, SUBLANES), :], sem)

    def issue(t, carry):
        for kk in range(TOP_K):
            copy(t, kk).start(priority=kk % 2)
        return carry

    lax.fori_loop(0, nt, issue, 0, unroll=4)

    def drain(t, carry):
        for kk in range(TOP_K):
            copy(t, kk).wait()
        return carry

    lax.fori_loop(0, nt, drain, 0, unroll=4)


def _dispatch(h3_tiles, pos, *, rows_padded):
    n = h3_tiles.shape[0] // SUBLANES
    nt = DISPATCH_TOKENS
    return pl.pallas_call(
        _dispatch_kernel,
        grid=(n // nt,),
        in_specs=[pl.BlockSpec((None, 1, nt * TOP_K), lambda i: (i, 0, 0), memory_space=pltpu.SMEM),
                  pl.BlockSpec((nt * SUBLANES, LANES), lambda i: (i, 0))],
        out_specs=pl.BlockSpec(memory_space=pl.ANY),
        out_shape=jax.ShapeDtypeStruct((rows_padded * SUBLANES, LANES), F32),
        scratch_shapes=[pltpu.SemaphoreType.DMA],
        compiler_params=pltpu.CompilerParams(dimension_semantics=("arbitrary",), has_side_effects=True),
        name="moe_dispatch",
    )(pos, h3_tiles)


def _expert_kernel(te_ref, nv_ref, xs_ref, wgu_ref, bgu_ref, wd_ref, bd_ref, y_ref, wgu_bf, wd_bf, acc_ref,
                   *, chunk):
    i = pl.program_id(0)
    tm = xs_ref.shape[0] // SUBLANES
    d = wd_bf.shape[1]
    f = wd_bf.shape[0]
    nvalid = nv_ref[i]
    first = jnp.logical_or(i == 0, te_ref[i] != te_ref[jnp.maximum(i - 1, 0)])

    @pl.when(jnp.logical_and(first, nvalid > 0))
    def _():
        wgu_bf[...] = wgu_ref[0].astype(BF16)
        wd_bf[...] = wd_ref[0].astype(BF16)

    @pl.when(nvalid > 0)
    def _():
        row = lax.broadcasted_iota(jnp.int32, (tm, 1), 0)
        x = jnp.concatenate([xs_ref[pl.ds(cc, tm, stride=SUBLANES), :] for cc in range(SUBLANES)], axis=-1)
        x = jnp.where(row < nvalid, x, 0.0).astype(BF16)
        for j in range(f // chunk):
            cs = slice(j * chunk, (j + 1) * chunk)
            us = slice(f + j * chunk, f + (j + 1) * chunk)
            g = jnp.dot(x, wgu_bf[:, cs], preferred_element_type=F32) + bgu_ref[0, :, cs]
            u = jnp.dot(x, wgu_bf[:, us], preferred_element_type=F32) + bgu_ref[0, :, us]
            gate = jnp.minimum(g, SWIGLU_LIMIT)
            up = jnp.clip(u, -SWIGLU_LIMIT, SWIGLU_LIMIT)
            act = ((up + 1.0) * (gate * _sigmoid(SWIGLU_ALPHA * gate))).astype(BF16)
            part = jnp.dot(act, wd_bf[cs, :], preferred_element_type=F32)
            if j == 0:
                acc_ref[...] = part + bd_ref[0]
            else:
                acc_ref[...] += part
        for cc in range(SUBLANES):
            y_ref[pl.ds(cc, tm, stride=SUBLANES), :] = acc_ref[:, cc * LANES:(cc + 1) * LANES]

    @pl.when(nvalid <= 0)
    def _():
        y_ref[...] = jnp.zeros_like(y_ref)


def _experts(tile_expert, tile_valid, xs, w_gate_up, b_gate_up, w_down, b_down):
    rows_padded = xs.shape[0] // SUBLANES
    e, d, f2 = w_gate_up.shape
    f = f2 // 2
    tm = MOE_TM
    grid_spec = pltpu.PrefetchScalarGridSpec(
        num_scalar_prefetch=2,
        grid=(rows_padded // tm,),
        in_specs=[
            pl.BlockSpec((tm * SUBLANES, LANES), lambda i, te, nv: (i, 0)),
            pl.BlockSpec((1, d, f2), lambda i, te, nv: (te[i], 0, 0)),
            pl.BlockSpec((1, 1, f2), lambda i, te, nv: (te[i], 0, 0)),
            pl.BlockSpec((1, f, d), lambda i, te, nv: (te[i], 0, 0)),
            pl.BlockSpec((1, 1, d), lambda i, te, nv: (te[i], 0, 0)),
        ],
        out_specs=pl.BlockSpec((tm * SUBLANES, LANES), lambda i, te, nv: (i, 0)),
        scratch_shapes=[pltpu.VMEM((d, f2), BF16), pltpu.VMEM((f, d), BF16), pltpu.VMEM((tm, d), F32)],
    )
    return pl.pallas_call(
        functools.partial(_expert_kernel, chunk=512),
        grid_spec=grid_spec,
        out_shape=jax.ShapeDtypeStruct(xs.shape, F32),
        compiler_params=_cparams(("arbitrary",)),
        name="moe_experts",
    )(tile_expert, tile_valid, xs, w_gate_up, b_gate_up.reshape(e, 1, f2), w_down, b_down.reshape(e, 1, d))


COMBINE_TOKENS = 256


def _combine_kernel(pos_ref, y_ref, x2_ref, mw_ref, o_ref, ybuf, sems):
    nt = COMBINE_TOKENS
    half = nt // 2

    def copy(t, kk, hf):
        row = pl.multiple_of(pos_ref[0, kk * nt + t], SUBLANES)
        dst = ybuf.at[kk, pl.ds(pl.multiple_of(t * SUBLANES, SUBLANES), SUBLANES), :]
        return pltpu.make_async_copy(y_ref.at[pl.ds(row, SUBLANES), :], dst, sems.at[hf])

    def issue_half(hf):
        def issue(t, carry):
            for kk in range(TOP_K):
                copy(t, kk, hf).start(priority=kk % 2)
            return carry

        lax.fori_loop(hf * half, (hf + 1) * half, issue, 0, unroll=4)

    def drain_half(hf):
        def drain(t, carry):
            for kk in range(TOP_K):
                copy(t, kk, hf).wait()
            return carry

        lax.fori_loop(hf * half, (hf + 1) * half, drain, 0, unroll=4)

    def combine_half(hf):
        rows = slice(hf * half, (hf + 1) * half)
        mw = mw_ref[rows, :]
        for cc in range(SUBLANES):
            cs = slice(cc * LANES, (cc + 1) * LANES)
            acc = x2_ref[rows, cs]
            for kk in range(TOP_K):
                ys = ybuf.at[kk][pl.ds(hf * half * SUBLANES + cc, half, stride=SUBLANES), :]
                acc = acc + mw[:, kk:kk + 1] * ys
            o_ref[rows, cs] = acc

    issue_half(0)
    issue_half(1)
    drain_half(0)
    combine_half(0)
    drain_half(1)
    combine_half(1)


def _combine(pos, y, x2, mw):
    n, d = x2.shape
    nt = COMBINE_TOKENS
    return pl.pallas_call(
        _combine_kernel,
        grid=(n // nt,),
        in_specs=[pl.BlockSpec((None, 1, nt * TOP_K), lambda i: (i, 0, 0), memory_space=pltpu.SMEM),
                  pl.BlockSpec(memory_space=pl.ANY),
                  pl.BlockSpec((nt, d), lambda i: (i, 0)),
                  pl.BlockSpec((nt, LANES), lambda i: (i, 0))],
        out_specs=pl.BlockSpec((nt, d), lambda i: (i, 0)),
        out_shape=jax.ShapeDtypeStruct((n, d), F32),
        scratch_shapes=[pltpu.VMEM((TOP_K, nt * SUBLANES, LANES), F32), pltpu.SemaphoreType.DMA((2,))],
        compiler_params=_cparams(("arbitrary",)),
        name="moe_combine",
    )(pos, y, x2, mw)


def _layer(x, mem, mix_norm, w_in, b_forget, fox_q_gain, fox_k_gain, fox_out_gain, gdn_conv_w, gdn_a_log,
           gdn_dt_bias, gdn_out_gain, w_out, xattn_norm, mem_norm, w_xq, w_xkv, xq_gain, xk_gain, w_xo,
           moe_norm, w_router, b_router, w_gate_up, b_gate_up, w_down, b_down):
    batch, seq, d = x.shape
    n = batch * seq
    mem_len = mem.shape[1]
    x2d = x.reshape(n, d)

    o_ff = 3 * FOX_WIDTH
    o_gq = o_ff + FOX_HEADS
    o_ga = o_gq + 3 * GDN_WIDTH
    o_gb = o_ga + GDN_HEADS
    o_gz = o_gb + GDN_HEADS
    w_main = jnp.concatenate([w_in[:, :o_ff], w_in[:, o_gq:o_ga], w_in[:, o_gz:]], axis=1).astype(BF16)
    w_small = jnp.concatenate([w_in[:, o_ff:o_gq], w_in[:, o_ga:o_gz],
                               jnp.zeros((d, LANES - SM_END), F32)], axis=1).astype(BF16)
    prow = jnp.zeros((SUBLANES, LANES), F32)
    prow = prow.at[0, SM_F0:SM_G0].set(b_forget).at[0, SM_G0:SM_B0].set(gdn_dt_bias)
    prow = prow.at[1, SM_G0:SM_B0].set(gdn_a_log)
    pcol = prow[0:2].T
    head_id = jnp.arange(FOX_WIDTH) // FOX_HEAD_DIM
    bd = (head_id[:, None] == head_id[None, :]).astype(BF16)
    row1 = lambda v: v.reshape(1, -1)

    tm = min(512, seq)
    fq, fk, fv, gqkv, gz, sm, smt = _inproj(
        x2d, row1(mix_norm), w_main, w_small, w_small.T, bd,
        row1(jnp.tile(fox_q_gain, FOX_HEADS)), row1(jnp.tile(fox_k_gain, FOX_HEADS)), prow, pcol,
        seq=seq, tm=tm)

    cum = smt[SM_F0:SM_G0].reshape(FOX_HEADS // 2, 2, batch, seq).transpose(2, 0, 1, 3)
    b3 = lambda a: a.reshape(batch, seq, a.shape[-1])
    o_fox = _fox_attention(b3(fq), b3(fk), b3(fv), cum, fox_out_gain.reshape(FOX_HEADS // 2, 1, LANES),
                           batch=batch, seq=seq, blk=min(512, seq))

    u, w, qd, qk, kdt, egl = _gdn_prep(gqkv, gdn_conv_w.T, sm, smt, batch=batch, seq=seq)
    o_gdn = _gdn_scan(u, w, qd, qk, kdt, egl, b3(gz), row1(gdn_out_gain), batch=batch, seq=seq)

    kx, vx = _memkv(mem.reshape(batch * mem_len, d), row1(mem_norm), w_xkv.astype(BF16), row1(xk_gain),
                    batch=batch, mem_len=mem_len)
    w_r = jnp.concatenate([w_router, jnp.zeros((d, LANES - N_EXPERTS), F32)], axis=1)
    w_r = jnp.stack(_split_bf16(w_r))
    b_r = jnp.concatenate([b_router, jnp.full((LANES - N_EXPERTS,), -jnp.inf, F32)]).reshape(1, LANES)
    x2, h3, mi, mw, cnt = _mid(
        x2d, o_fox.reshape(n, FOX_WIDTH), o_gdn.reshape(n, GDN_WIDTH), w_out.astype(BF16), row1(xattn_norm),
        w_xq.astype(BF16), row1(xq_gain), kx, vx, w_xo.astype(BF16), row1(moe_norm), w_r, b_r,
        seq=seq, mem_len=mem_len, tm=min(512, seq))

    counts = cnt[0, :N_EXPERTS]
    padded = ((counts + MOE_TM - 1) // MOE_TM) * MOE_TM
    ends = jnp.cumsum(padded)
    starts = ends - padded
    rows_padded = n * TOP_K + N_EXPERTS * MOE_TM
    n_tiles = rows_padded // MOE_TM
    tile_start = jnp.arange(n_tiles, dtype=jnp.int32) * MOE_TM
    tile_expert = jnp.zeros((n_tiles,), jnp.int32)
    tile_limit = jnp.zeros((n_tiles,), jnp.int32)
    last_used = jnp.zeros((), jnp.int32)
    base = jnp.zeros((TOP_K, n), jnp.int32)
    for e in range(N_EXPERTS):
        inside = (tile_start >= starts[e]) & (tile_start < ends[e])
        tile_expert = jnp.where(inside, e, tile_expert)
        tile_limit = jnp.where(inside, starts[e] + counts[e], tile_limit)
        last_used = jnp.where(counts[e] > 0, e, last_used)
        base = jnp.where(mi[:TOP_K] == e, starts[e], base)
    tile_valid = jnp.clip(tile_limit - tile_start, 0, MOE_TM).astype(jnp.int32)
    tile_expert = jnp.where(tile_valid > 0, tile_expert, last_used).astype(jnp.int32)

    pos = ((base + mi[TOP_K:]) * SUBLANES).astype(jnp.int32)

    def step_tables(nt):
        return pos.reshape(TOP_K, n // nt, nt).transpose(1, 0, 2).reshape(n // nt, 1, TOP_K * nt)

    xs = _dispatch(h3, step_tables(DISPATCH_TOKENS), rows_padded=rows_padded)
    y = _experts(tile_expert, tile_valid, xs, w_gate_up, b_gate_up, w_down, b_down)
    out = _combine(step_tables(COMBINE_TOKENS), y, x2, mw)
    return out.reshape(batch, seq, d)


def kernel(x, mem, mix_norm, w_in, b_forget, fox_q_gain, fox_k_gain, fox_out_gain, gdn_conv_w, gdn_a_log,
           gdn_dt_bias, gdn_out_gain, w_out, xattn_norm, mem_norm, w_xq, w_xkv, xq_gain, xk_gain, w_xo,
           moe_norm, w_router, b_router, w_gate_up, b_gate_up, w_down, b_down):
    depth = mix_norm.shape[0]
    for l in range(depth):
        x = _layer(x, mem, mix_norm[l], w_in[l], b_forget[l], fox_q_gain[l], fox_k_gain[l], fox_out_gain[l],
                   gdn_conv_w[l], gdn_a_log[l], gdn_dt_bias[l], gdn_out_gain[l], w_out[l], xattn_norm[l],
                   mem_norm[l], w_xq[l], w_xkv[l], xq_gain[l], xk_gain[l], w_xo[l], moe_norm[l], w_router[l],
                   b_router[l], w_gate_up[l], b_gate_up[l], w_down[l], b_down[l])
    return x
```

```python
import functools

import jax
import jax.numpy as jnp
from jax import lax
from jax.experimental import pallas as pl
from jax.experimental.pallas import tpu as pltpu

F32 = jnp.float32
BF16 = jnp.bfloat16

EPS = 1e-6
FOX_HEADS, FOX_HEAD_DIM = 8, 64
GDN_HEADS, GDN_HEAD_DIM = 4, 128
FOX_WIDTH = FOX_HEADS * FOX_HEAD_DIM
GDN_WIDTH = GDN_HEADS * GDN_HEAD_DIM
CONV_WIDTH = 4
CHUNK = 64
XA_HEADS = 4
N_EXPERTS = 32
TOP_K = 4
SWIGLU_LIMIT = 7.0
SWIGLU_ALPHA = 1.702
LOG2E = 1.4426950408889634

LANES = 128
SUBLANES = 8
VMEM_LIMIT = 52 * 1024 * 1024

SM_F0, SM_G0, SM_B0, SM_END = 0, 8, 12, 16

NT_DIMS = (((1,), (1,)), ((), ()))


def _cparams(sem):
    return pltpu.CompilerParams(dimension_semantics=sem, vmem_limit_bytes=VMEM_LIMIT)


def _rms(x, gain):
    return x * lax.rsqrt(jnp.mean(x * x, axis=-1, keepdims=True) + EPS) * gain


def _split_bf16(x):
    hi = x.astype(BF16)
    return hi, (x - hi.astype(F32)).astype(BF16)


def _dot_mask_left(mask, x):
    hi, lo = _split_bf16(x)
    mb = mask.astype(BF16)
    return jnp.dot(mb, hi, preferred_element_type=F32) + jnp.dot(mb, lo, preferred_element_type=F32)


def _dot_mask_right(x, mask):
    hi, lo = _split_bf16(x)
    mb = mask.astype(BF16)
    return jnp.dot(hi, mb, preferred_element_type=F32) + jnp.dot(lo, mb, preferred_element_type=F32)


def _softplus(t):
    return jnp.maximum(t, 0.0) + jnp.log1p(jnp.exp(-jnp.abs(t)))


def _sigmoid(t):
    return 1.0 / (1.0 + jnp.exp(-t))


def _small_act(v, bias, alog, idx):
    t = v + bias
    tail = jnp.log1p(jnp.exp(-jnp.abs(t)))
    log_f = jnp.minimum(t, 0.0) - tail
    g = -jnp.exp(alog) * (jnp.maximum(t, 0.0) + tail)
    beta = _sigmoid(v)
    return jnp.where(idx < SM_G0, log_f, jnp.where(idx < SM_B0, g, jnp.where(idx < SM_END, beta, 0.0)))


def _inproj_kernel(x_ref, gain_ref, w_ref, ws_ref, wst_ref, bd_ref, qg_ref, kg_ref, prow_ref, pcol_ref,
                   fq_ref, fk_ref, fv_ref, gqkv_ref, gz_ref, sm_ref, smt_ref, carry_r, carry_c,
                   *, tm, tiles_per_seq):
    i = pl.program_id(0)
    hb = _rms(x_ref[...], gain_ref[...]).astype(BF16)

    def proj(lo, hi):
        return jnp.dot(hb, w_ref[:, lo:hi], preferred_element_type=F32)

    def headnorm(p, g):
        ss = jnp.dot((p * p).astype(BF16), bd_ref[...], preferred_element_type=F32)
        return p * lax.rsqrt(ss * (1.0 / FOX_HEAD_DIM) + EPS) * g

    w0 = FOX_WIDTH
    fq_ref[...] = (headnorm(proj(0, w0), qg_ref[...]) * (FOX_HEAD_DIM ** -0.5 * LOG2E)).astype(BF16)
    fk_ref[...] = headnorm(proj(w0, 2 * w0), kg_ref[...]).astype(BF16)
    fv_ref[...] = proj(2 * w0, 3 * w0).astype(BF16)
    g0 = 3 * w0
    gqkv_ref[...] = proj(g0, g0 + 3 * GDN_WIDTH)
    gz_ref[...] = proj(g0 + 3 * GDN_WIDTH, g0 + 4 * GDN_WIDTH)

    @pl.when(i % tiles_per_seq == 0)
    def _():
        carry_r[...] = jnp.zeros_like(carry_r)
        carry_c[...] = jnp.zeros_like(carry_c)

    sm = jnp.dot(hb, ws_ref[...], preferred_element_type=F32)
    smt = lax.dot_general(wst_ref[...], hb, NT_DIMS, preferred_element_type=F32)
    lane = lax.broadcasted_iota(jnp.int32, (1, LANES), 1)
    srow = lax.broadcasted_iota(jnp.int32, (LANES, 1), 0)
    vals = _small_act(sm, prow_ref[0:1, :], prow_ref[1:2, :], lane)
    vals_t = _small_act(smt, pcol_ref[:, 0:1], pcol_ref[:, 1:2], srow)

    r = lax.broadcasted_iota(jnp.int32, (tm, tm), 0)
    c = lax.broadcasted_iota(jnp.int32, (tm, tm), 1)
    cum = _dot_mask_left(c <= r, vals) + carry_r[...]
    cum_t = _dot_mask_right(vals_t, r <= c) + carry_c[...]
    carry_r[...] = cum[tm - 1:tm, :]
    carry_c[...] = cum_t[:, tm - 1:tm]
    sm_ref[...] = jnp.where(lane < SM_G0, cum, vals)
    smt_ref[...] = jnp.where(srow < SM_G0, cum_t, vals_t)


def _inproj(x2d, gain, w_main, w_small, w_small_t, bd, qg, kg, prow, pcol, *, seq, tm):
    n, d = x2d.shape
    wm = w_main.shape[1]
    full = lambda shape: pl.BlockSpec(shape, lambda i: (0,) * len(shape))
    rows = lambda width: pl.BlockSpec((tm, width), lambda i: (i, 0))
    out_shape = (
        jax.ShapeDtypeStruct((n, FOX_WIDTH), BF16),
        jax.ShapeDtypeStruct((n, FOX_WIDTH), BF16),
        jax.ShapeDtypeStruct((n, FOX_WIDTH), BF16),
        jax.ShapeDtypeStruct((n, 3 * GDN_WIDTH), F32),
        jax.ShapeDtypeStruct((n, GDN_WIDTH), F32),
        jax.ShapeDtypeStruct((n, LANES), F32),
        jax.ShapeDtypeStruct((LANES, n), F32),
    )
    return pl.pallas_call(
        functools.partial(_inproj_kernel, tm=tm, tiles_per_seq=seq // tm),
        grid=(n // tm,),
        in_specs=[rows(d), full((1, d)), full((d, wm)), full((d, LANES)), full((LANES, d)),
                  full((FOX_WIDTH, FOX_WIDTH)), full((1, FOX_WIDTH)), full((1, FOX_WIDTH)),
                  full((SUBLANES, LANES)), full((LANES, 2))],
        out_specs=(rows(FOX_WIDTH), rows(FOX_WIDTH), rows(FOX_WIDTH), rows(3 * GDN_WIDTH), rows(GDN_WIDTH),
                   rows(LANES), pl.BlockSpec((LANES, tm), lambda i: (0, i))),
        out_shape=out_shape,
        scratch_shapes=[pltpu.VMEM((1, LANES), F32), pltpu.VMEM((LANES, 1), F32)],
        compiler_params=_cparams(("arbitrary",)),
        name="inproj",
    )(x2d, gain, w_main, w_small, w_small_t, bd, qg, kg, prow, pcol)


def _fox_kernel(q_ref, k_ref, v_ref, cum_ref, gain_ref, o_ref, m0_ref, m1_ref, acc0_ref, acc1_ref, *, blk):
    qi = pl.program_id(2)
    lane = lax.broadcasted_iota(jnp.int32, (1, LANES), 1)
    lo = lane < FOX_HEAD_DIM
    q = q_ref[...]
    zero = jnp.zeros_like(q)
    q_heads = (jnp.where(lo, q, zero), jnp.where(lo, zero, q))
    q0 = pl.multiple_of(qi * blk, blk)
    c_ref = [cum_ref[hh:hh + 1, pl.ds(q0, LANES)][:, 0:1] for hh in range(2)]
    head_lanes = (lo, jnp.logical_not(lo))
    sum_lane = (FOX_HEAD_DIM, 0)
    ones_col = [jnp.where(lane == sum_lane[hh], 1.0, 0.0).astype(BF16) for hh in range(2)]
    m_refs = (m0_ref, m1_ref)
    acc_refs = (acc0_ref, acc1_ref)
    for hh in range(2):
        m_refs[hh][...] = jnp.full_like(m_refs[hh], -1e30)
        acc_refs[hh][...] = jnp.zeros_like(acc_refs[hh])

    def step(k0, masked):
        kb = k_ref[pl.ds(k0, blk), :]
        vb = v_ref[pl.ds(k0, blk), :]
        scores = [lax.dot_general(q_heads[hh], kb, NT_DIMS, preferred_element_type=F32) for hh in range(2)]
        for hh in range(2):
            s = scores[hh] + (c_ref[hh] - cum_ref[hh:hh + 1, pl.ds(k0, blk)]) * LOG2E
            if masked:
                r = lax.broadcasted_iota(jnp.int32, (blk, blk), 0)
                c = lax.broadcasted_iota(jnp.int32, (blk, blk), 1)
                s = jnp.where(c <= r, s, -jnp.inf)
            m_old = m_refs[hh][...]
            m_new = jnp.maximum(m_old, jnp.max(s, axis=-1, keepdims=True))
            alpha = jnp.exp2(m_old - m_new)
            p = jnp.exp2(s - jnp.concatenate([m_new] * (blk // LANES), axis=1))
            v_h = jnp.where(head_lanes[hh], vb, ones_col[hh])
            pv = jnp.dot(p.astype(BF16), v_h, preferred_element_type=F32)
            acc_refs[hh][...] = acc_refs[hh][...] * alpha + pv
            m_refs[hh][...] = m_new

    def body(kp, carry):
        step(pl.multiple_of(kp * (2 * blk), blk), False)
        step(pl.multiple_of(kp * (2 * blk) + blk, blk), False)
        return carry

    lax.fori_loop(0, qi // 2, body, 0)

    @pl.when(qi % 2 == 1)
    def _():
        step(pl.multiple_of((qi - 1) * blk, blk), False)

    step(q0, True)

    acc0, acc1 = acc0_ref[...], acc1_ref[...]
    l0 = acc0[:, sum_lane[0]:sum_lane[0] + 1]
    l1 = acc1[:, sum_lane[1]:sum_lane[1] + 1]
    o = jnp.where(lo, acc0 * (1.0 / l0), acc1 * (1.0 / l1))
    o2 = o * o
    ss_lo = jnp.sum(jnp.where(lo, o2, 0.0), axis=-1, keepdims=True)
    ss_hi = jnp.sum(jnp.where(lo, 0.0, o2), axis=-1, keepdims=True)
    ms = jnp.where(lo, ss_lo, ss_hi) * (1.0 / FOX_HEAD_DIM)
    o_ref[...] = (o * lax.rsqrt(ms + EPS) * gain_ref[...]).astype(o_ref.dtype)


def _fox_attention(fq, fk, fv, cum, gain_pairs, *, batch, seq, blk):
    npairs = FOX_HEADS // 2
    return pl.pallas_call(
        functools.partial(_fox_kernel, blk=blk),
        grid=(batch, npairs, seq // blk),
        in_specs=[
            pl.BlockSpec((None, blk, LANES), lambda b, j, i: (b, i, j)),
            pl.BlockSpec((None, seq, LANES), lambda b, j, i: (b, 0, j)),
            pl.BlockSpec((None, seq, LANES), lambda b, j, i: (b, 0, j)),
            pl.BlockSpec((None, None, 2, seq), lambda b, j, i: (b, j, 0, 0)),
            pl.BlockSpec((None, 1, LANES), lambda b, j, i: (j, 0, 0)),
        ],
        out_specs=pl.BlockSpec((None, blk, LANES), lambda b, j, i: (b, i, j)),
        out_shape=jax.ShapeDtypeStruct((batch, seq, FOX_WIDTH), BF16),
        scratch_shapes=[pltpu.VMEM((blk, LANES), F32) for _ in range(4)],
        compiler_params=_cparams(("parallel", "parallel", "arbitrary")),
        name="fox_attention",
    )(fq, fk, fv, cum, gain_pairs)


GDN_BLK = 4 * CHUNK
QK_BLK = 2 * CHUNK


def _gdn_prep_kernel(x_ref, halo_ref, cw_ref, sm_ref, smt_ref,
                     u_ref, w_ref, qd_ref, qk_ref, kdt_ref, egl_ref, xpad_ref):
    i = pl.program_id(1)
    nb = GDN_BLK
    halo = halo_ref[...]
    xpad_ref[0:SUBLANES, :] = jnp.where(i > 0, halo, jnp.zeros_like(halo))
    xpad_ref[SUBLANES:, :] = x_ref[...]
    y = None
    for j in range(CONV_WIDTH):
        start = SUBLANES - (CONV_WIDTH - 1) + j
        term = cw_ref[j:j + 1, :] * xpad_ref[start:start + nb, :]
        y = term if y is None else y + term
    y = y * _sigmoid(y)

    r = lax.broadcasted_iota(jnp.int32, (nb, nb), 0)
    c = lax.broadcasted_iota(jnp.int32, (nb, nb), 1)
    chunk_shift = CHUNK.bit_length() - 1
    same = jnp.right_shift(r, chunk_shift) == jnp.right_shift(c, chunk_shift)
    incl = same & (c <= r)
    strict = same & (c < r)
    sm = sm_ref[...]
    g_cum = _dot_mask_left(incl, sm)
    g_tot = _dot_mask_left(same, sm)
    g_cum_t = _dot_mask_right(smt_ref[...], same & (r <= c))
    eye = (r == c).astype(F32)

    powers, t_invs, rhs = [], [], []
    for h in range(GDN_HEADS):
        sl = slice(h * GDN_HEAD_DIM, (h + 1) * GDN_HEAD_DIM)
        q = y[:, h * GDN_HEAD_DIM:(h + 1) * GDN_HEAD_DIM]
        k = y[:, GDN_WIDTH + h * GDN_HEAD_DIM:GDN_WIDTH + (h + 1) * GDN_HEAD_DIM]
        v = y[:, 2 * GDN_WIDTH + h * GDN_HEAD_DIM:2 * GDN_WIDTH + (h + 1) * GDN_HEAD_DIM]
        qn = q * lax.rsqrt(jnp.sum(q * q, axis=-1, keepdims=True) + EPS) * GDN_HEAD_DIM ** -0.5
        kn = k * lax.rsqrt(jnp.sum(k * k, axis=-1, keepdims=True) + EPS)
        gc = g_cum[:, SM_G0 + h:SM_G0 + h + 1]
        gl = g_tot[:, SM_G0 + h:SM_G0 + h + 1]
        gr = g_cum_t[SM_G0 + h:SM_G0 + h + 1, :]
        beta = sm[:, SM_B0 + h:SM_B0 + h + 1]
        decay = jnp.where(incl, jnp.exp(jnp.where(incl, gc - gr, 0.0)), 0.0)
        qb, kb = qn.astype(BF16), kn.astype(BF16)
        kk = lax.dot_general(kb, kb, NT_DIMS, preferred_element_type=F32)
        a = jnp.where(strict, beta * kk * decay, 0.0)
        eg = jnp.exp(gc)
        powers.append(a)
        t_invs.append(eye - a)
        rhs.append(jnp.concatenate([(v * beta).astype(BF16), (kn * (beta * eg)).astype(BF16)], axis=1))
        qk = lax.dot_general(qb, kb, NT_DIMS, preferred_element_type=F32)
        qk = jnp.where(incl, qk * decay, 0.0).astype(BF16)
        qk_ref[:, h * QK_BLK:(h + 1) * QK_BLK] = jnp.concatenate(
            [qk[j * QK_BLK:(j + 1) * QK_BLK, j * QK_BLK:(j + 1) * QK_BLK] for j in range(nb // QK_BLK)], axis=0)
        qd_ref[:, sl] = (qn * eg).astype(BF16)
        kdt_ref[sl, :] = (kn * jnp.exp(gl - gc)).T.astype(BF16)
        egl_ref[:, sl] = jnp.broadcast_to(jnp.exp(gl), (nb, GDN_HEAD_DIM))

    for _ in range(5):
        for h in range(GDN_HEADS):
            pb = powers[h].astype(BF16)
            powers[h] = jnp.dot(pb, pb, preferred_element_type=F32)
        for h in range(GDN_HEADS):
            t_invs[h] = t_invs[h] + jnp.dot(t_invs[h].astype(BF16), powers[h].astype(BF16),
                                            preferred_element_type=F32)
    for h in range(GDN_HEADS):
        sl = slice(h * GDN_HEAD_DIM, (h + 1) * GDN_HEAD_DIM)
        uw = jnp.dot(t_invs[h].astype(BF16), rhs[h], preferred_element_type=F32)
        u_ref[:, sl] = uw[:, :GDN_HEAD_DIM]
        w_ref[:, sl] = uw[:, GDN_HEAD_DIM:].astype(BF16)


def _gdn_prep(gqkv, conv_w_t, sm, smt, *, batch, seq):
    nb = GDN_BLK
    bps = seq // nb
    hps = nb // SUBLANES
    width = 3 * GDN_WIDTH
    row = lambda w: pl.BlockSpec((None, nb, w), lambda b, i: (b, i, 0))
    return pl.pallas_call(
        _gdn_prep_kernel,
        grid=(batch, bps),
        in_specs=[
            pl.BlockSpec((nb, width), lambda b, i: (b * bps + i, 0)),
            pl.BlockSpec((SUBLANES, width), lambda b, i: (jnp.maximum((b * bps + i) * hps - 1, 0), 0)),
            pl.BlockSpec((CONV_WIDTH, width), lambda b, i: (0, 0)),
            pl.BlockSpec((nb, LANES), lambda b, i: (b * bps + i, 0)),
            pl.BlockSpec((LANES, nb), lambda b, i: (0, b * bps + i)),
        ],
        out_specs=(row(GDN_WIDTH), row(GDN_WIDTH), row(GDN_WIDTH), row(GDN_HEADS * QK_BLK),
                   pl.BlockSpec((None, GDN_WIDTH, nb), lambda b, i: (b, 0, i)), row(GDN_WIDTH)),
        out_shape=(
            jax.ShapeDtypeStruct((batch, seq, GDN_WIDTH), F32),
            jax.ShapeDtypeStruct((batch, seq, GDN_WIDTH), BF16),
            jax.ShapeDtypeStruct((batch, seq, GDN_WIDTH), BF16),
            jax.ShapeDtypeStruct((batch, seq, GDN_HEADS * QK_BLK), BF16),
            jax.ShapeDtypeStruct((batch, GDN_WIDTH, seq), BF16),
            jax.ShapeDtypeStruct((batch, seq, GDN_WIDTH), F32),
        ),
        scratch_shapes=[pltpu.VMEM((nb + SUBLANES, width), F32)],
        compiler_params=_cparams(("parallel", "parallel")),
        name="gdn_prep",
    )(gqkv, gqkv, conv_w_t, sm, smt)


def _gdn_scan_kernel(u_ref, w_ref, qd_ref, qk_ref, kdt_ref, egl_ref, z_ref, gain_ref, o_ref, s_ref, vz_ref,
                     *, batch):
    pb = QK_BLK

    @pl.when(pl.program_id(0) == 0)
    def _():
        s_ref[...] = jnp.zeros_like(s_ref)

    vz_ref[...] = jnp.zeros_like(vz_ref)
    for cidx in range(GDN_BLK // CHUNK):
        rows = slice(cidx * CHUNK, (cidx + 1) * CHUNK)
        blk_rows = slice(cidx * CHUNK // pb * pb, (cidx * CHUNK // pb + 1) * pb)
        for b in range(batch):
            for h in range(GDN_HEADS):
                bh = b * GDN_HEADS + h
                sl = slice(h * GDN_HEAD_DIM, (h + 1) * GDN_HEAD_DIM)
                s_old = s_ref[bh]
                lhs1 = jnp.concatenate([w_ref[b, rows, sl], qd_ref[b, rows, sl]], axis=0)
                r1 = jnp.dot(lhs1, s_old.astype(BF16), preferred_element_type=F32)
                v_new = u_ref[b, rows, sl] - r1[:CHUNK]
                vz_ref[bh, rows, :] = v_new.astype(BF16)
                lhs2 = jnp.concatenate([qk_ref[b, rows, h * pb:(h + 1) * pb], kdt_ref[b, sl, blk_rows]], axis=0)
                r2 = jnp.dot(lhs2, vz_ref[bh, blk_rows, :], preferred_element_type=F32)
                vz_ref[bh, rows, :] = jnp.zeros((CHUNK, GDN_HEAD_DIM), BF16)
                last = egl_ref[b, (cidx + 1) * CHUNK - 1:(cidx + 1) * CHUNK, sl]
                s_ref[bh] = s_old * last + r2[CHUNK:]
                o = r1[CHUNK:] + r2[:CHUNK]
                z = z_ref[b, rows, sl]
                o_ref[b, rows, sl] = (_rms(o, gain_ref[...]) * (z * _sigmoid(z))).astype(o_ref.dtype)


def _gdn_scan(u, w, qd, qk, kdt, egl, z, gain, *, batch, seq):
    nb = GDN_BLK
    row = lambda width: pl.BlockSpec((batch, nb, width), lambda i: (0, i, 0))
    return pl.pallas_call(
        functools.partial(_gdn_scan_kernel, batch=batch),
        grid=(seq // nb,),
        in_specs=[row(GDN_WIDTH), row(GDN_WIDTH), row(GDN_WIDTH), row(GDN_HEADS * QK_BLK),
                  pl.BlockSpec((batch, GDN_WIDTH, nb), lambda i: (0, 0, i)), row(GDN_WIDTH), row(GDN_WIDTH),
                  pl.BlockSpec((1, GDN_HEAD_DIM), lambda i: (0, 0))],
        out_specs=row(GDN_WIDTH),
        out_shape=jax.ShapeDtypeStruct((batch, seq, GDN_WIDTH), BF16),
        scratch_shapes=[pltpu.VMEM((batch * GDN_HEADS, GDN_HEAD_DIM, GDN_HEAD_DIM), F32),
                        pltpu.VMEM((batch * GDN_HEADS, nb, GDN_HEAD_DIM), BF16)],
        compiler_params=_cparams(("arbitrary",)),
        name="gdn_scan",
    )(u, w, qd, qk, kdt, egl, z, gain)


def _memkv_kernel(m_ref, gain_ref, w_ref, kg_ref, k_ref, v_ref):
    d = m_ref.shape[-1]
    hd = d // XA_HEADS
    mb = _rms(m_ref[...], gain_ref[...]).astype(BF16)
    kv = jnp.dot(mb, w_ref[...], preferred_element_type=F32)
    for h in range(XA_HEADS):
        sl = slice(h * hd, (h + 1) * hd)
        k_ref[:, sl] = _rms(kv[:, sl], kg_ref[...]).astype(BF16)
    v_ref[...] = kv[:, d:].astype(BF16)


def _memkv(mem2d, gain, w_xkv, xk_gain, *, batch, mem_len):
    d = mem2d.shape[-1]
    full = lambda shape: pl.BlockSpec(shape, lambda b: (0,) * len(shape))
    row = pl.BlockSpec((mem_len, d), lambda b: (b, 0))
    return pl.pallas_call(
        _memkv_kernel,
        grid=(batch,),
        in_specs=[row, full((1, d)), full((d, 2 * d)), full((1, d // XA_HEADS))],
        out_specs=(row, row),
        out_shape=(jax.ShapeDtypeStruct(mem2d.shape, BF16), jax.ShapeDtypeStruct(mem2d.shape, BF16)),
        compiler_params=_cparams(("parallel",)),
        name="mem_kv",
    )(mem2d, gain, w_xkv, xk_gain)


def _mid_kernel(x_ref, of_ref, og_ref, wo_ref, xg_ref, wq_ref, qg_ref, k_ref, v_ref, wxo_ref, mg_ref,
                wr_ref, br_ref,
                x2_ref, h3_ref, mi_ref, mw_ref, cnt_ref, carry_ref, *, tm):
    i = pl.program_id(0)
    d = x_ref.shape[-1]
    hd = d // XA_HEADS
    x1 = (x_ref[...]
          + jnp.dot(of_ref[...], wo_ref[0:FOX_WIDTH, :], preferred_element_type=F32)
          + jnp.dot(og_ref[...], wo_ref[FOX_WIDTH:, :], preferred_element_type=F32))
    h2 = _rms(x1, xg_ref[...]).astype(BF16)
    q = jnp.dot(h2, wq_ref[...], preferred_element_type=F32)
    heads = []
    for h in range(XA_HEADS):
        sl = slice(h * hd, (h + 1) * hd)
        qn = (_rms(q[:, sl], qg_ref[...]) * hd ** -0.5).astype(BF16)
        s = lax.dot_general(qn, k_ref[:, sl], NT_DIMS, preferred_element_type=F32)
        p = jnp.exp(s - jnp.max(s, axis=-1, keepdims=True))
        p = p * (1.0 / jnp.sum(p, axis=-1, keepdims=True))
        heads.append(jnp.dot(p.astype(BF16), v_ref[:, sl], preferred_element_type=F32).astype(BF16))
    x2 = x1 + jnp.dot(jnp.concatenate(heads, axis=-1), wxo_ref[...], preferred_element_type=F32)
    x2_ref[...] = x2
    h3 = _rms(x2, mg_ref[...])
    for cc in range(SUBLANES):
        h3_ref[pl.ds(cc, tm, stride=SUBLANES), :] = h3[:, cc * LANES:(cc + 1) * LANES]
    h_hi, h_lo = _split_bf16(h3)
    logits = (jnp.dot(h_hi, wr_ref[0], preferred_element_type=F32)
              + jnp.dot(h_lo, wr_ref[0], preferred_element_type=F32)
              + jnp.dot(h_hi, wr_ref[1], preferred_element_type=F32)) + br_ref[...]
    lane = lax.broadcasted_iota(jnp.int32, (tm, LANES), 1)
    work = logits
    vals, idxs = [], []
    onehot = jnp.zeros((tm, LANES), F32)
    for _ in range(TOP_K):
        mx = jnp.max(work, axis=-1, keepdims=True)
        idx = jnp.min(jnp.where(work == mx, lane, LANES), axis=-1, keepdims=True)
        sel = lane == idx
        onehot = jnp.where(sel, 1.0, onehot)
        work = jnp.where(sel, -jnp.inf, work)
        vals.append(mx)
        idxs.append(idx)
    es = [jnp.exp(v - vals[0]) for v in vals]
    inv_denom = 1.0 / (es[0] + es[1] + es[2] + es[3])

    @pl.when(i == 0)
    def _():
        carry_ref[...] = jnp.zeros_like(carry_ref)

    r = lax.broadcasted_iota(jnp.int32, (tm, tm), 0)
    c = lax.broadcasted_iota(jnp.int32, (tm, tm), 1)
    before = jnp.dot((c < r).astype(BF16), onehot.astype(BF16), preferred_element_type=F32) + carry_ref[...]
    mi = jnp.zeros((tm, LANES), F32)
    mw = jnp.zeros((tm, LANES), F32)
    for kk in range(TOP_K):
        rank = jnp.sum(jnp.where(lane == idxs[kk], before, 0.0), axis=-1, keepdims=True)
        mi = jnp.where(lane == kk, idxs[kk].astype(F32), mi)
        mi = jnp.where(lane == TOP_K + kk, rank, mi)
        mw = jnp.where(lane == kk, es[kk] * inv_denom, mw)
    mi_ref[...] = mi.T[0:2 * TOP_K, :].astype(jnp.int32)
    mw_ref[...] = mw
    total = carry_ref[...] + jnp.sum(onehot, axis=0, keepdims=True)
    carry_ref[...] = total
    cnt_ref[...] = jnp.broadcast_to(total, cnt_ref.shape).astype(jnp.int32)


def _mid(x2d, o_fox, o_gdn, w_out, xg, w_xq, xq_gain, kx, vx, w_xo, mg, w_r, b_r, *, seq, mem_len, tm):
    n, d = x2d.shape
    full = lambda shape: pl.BlockSpec(shape, lambda i: (0,) * len(shape))
    rows = lambda width: pl.BlockSpec((tm, width), lambda i: (i, 0))
    mem = pl.BlockSpec((mem_len, d), lambda i: (i // (seq // tm), 0))
    return pl.pallas_call(
        functools.partial(_mid_kernel, tm=tm),
        grid=(n // tm,),
        in_specs=[rows(d), rows(FOX_WIDTH), rows(GDN_WIDTH), full((d, d)), full((1, d)), full((d, d)),
                  full((1, d // XA_HEADS)), mem, mem, full((d, d)), full((1, d)), full((2, d, LANES)),
                  full((1, LANES))],
        out_specs=(rows(d), pl.BlockSpec((tm * d // LANES, LANES), lambda i: (i, 0)),
                   pl.BlockSpec((None, 2 * TOP_K, tm), lambda i: (i, 0, 0)), rows(LANES),
                   full((SUBLANES, LANES))),
        out_shape=(jax.ShapeDtypeStruct((n, d), F32), jax.ShapeDtypeStruct((n * d // LANES, LANES), F32),
                   jax.ShapeDtypeStruct((n // tm, 2 * TOP_K, tm), jnp.int32),
                   jax.ShapeDtypeStruct((n, LANES), F32),
                   jax.ShapeDtypeStruct((SUBLANES, LANES), jnp.int32)),
        scratch_shapes=[pltpu.VMEM((1, LANES), F32)],
        compiler_params=_cparams(("arbitrary",)),
        name="outproj_xattn_router",
    )(x2d, o_fox, o_gdn, w_out, xg, w_xq, xq_gain, kx, vx, w_xo, mg, w_r, b_r)


MOE_TM = 512
ROUTE_TOKENS = 512


def _dispatch_kernel(pos_ref, src_ref, dst_ref, sem):
    nt = src_ref.shape[0] // SUBLANES

    def copy(t, kk):
        src = src_ref.at[pl.ds(pl.multiple_of(t * SUBLANES, SUBLANES), SUBLANES), :]
        row = pl.multiple_of(pos_ref[0, kk * nt + t], SUBLANES)
        return pltpu.make_async_copy(src, dst_ref.at[pl.ds(row, SUBLANES), :], sem)

    def issue(t, carry):
        for kk in range(TOP_K):
            copy(t, kk).start(priority=kk % 2)
        return carry

    lax.fori_loop(0, nt, issue, 0, unroll=4)

    for kk in range(TOP_K):
        pltpu.make_async_copy(src_ref, dst_ref.at[pl.ds(0, nt * SUBLANES), :], sem).wait()


def _dispatch(h3_tiles, pos, *, rows_padded):
    n = h3_tiles.shape[0] // SUBLANES
    nt = pos.shape[-1] // TOP_K
    return pl.pallas_call(
        _dispatch_kernel,
        grid=(n // nt,),
        in_specs=[pl.BlockSpec((None, 1, nt * TOP_K), lambda i: (i, 0, 0), memory_space=pltpu.SMEM),
                  pl.BlockSpec((nt * SUBLANES, LANES), lambda i: (i, 0))],
        out_specs=pl.BlockSpec(memory_space=pl.ANY),
        out_shape=jax.ShapeDtypeStruct((rows_padded * SUBLANES, LANES), F32),
        scratch_shapes=[pltpu.SemaphoreType.DMA],
        compiler_params=pltpu.CompilerParams(dimension_semantics=("arbitrary",), has_side_effects=True),
        name="moe_dispatch",
    )(pos, h3_tiles)


def _expert_kernel(te_ref, nv_ref, nu_ref, xs_ref, wgu_ref, bgu_ref, wd_ref, bd_ref, y_ref, wgu_bf, wd_bf, acc_ref,
                   *, chunk):
    i = pl.program_id(0)
    tm = xs_ref.shape[0] // SUBLANES
    d = wd_bf.shape[1]
    f = wd_bf.shape[0]
    nvalid = nv_ref[i]
    first = jnp.logical_or(i == 0, te_ref[i] != te_ref[jnp.maximum(i - 1, 0)])

    @pl.when(jnp.logical_and(first, nvalid > 0))
    def _():
        wgu_bf[...] = wgu_ref[0].astype(BF16)
        wd_bf[...] = wd_ref[0].astype(BF16)

    @pl.when(nvalid > 0)
    def _():
        row = lax.broadcasted_iota(jnp.int32, (tm, 1), 0)
        x = jnp.concatenate([xs_ref[pl.ds(cc, tm, stride=SUBLANES), :] for cc in range(SUBLANES)], axis=-1)
        x = jnp.where(row < nvalid, x, 0.0).astype(BF16)
        for j in range(f // chunk):
            cs = slice(j * chunk, (j + 1) * chunk)
            us = slice(f + j * chunk, f + (j + 1) * chunk)
            g = jnp.dot(x, wgu_bf[:, cs], preferred_element_type=F32) + bgu_ref[0, :, cs]
            u = jnp.dot(x, wgu_bf[:, us], preferred_element_type=F32) + bgu_ref[0, :, us]
            gate = jnp.minimum(g, SWIGLU_LIMIT)
            up = jnp.clip(u, -SWIGLU_LIMIT, SWIGLU_LIMIT)
            act = ((up + 1.0) * (gate * _sigmoid(SWIGLU_ALPHA * gate))).astype(BF16)
            part = jnp.dot(act, wd_bf[cs, :], preferred_element_type=F32)
            if j == 0:
                acc_ref[...] = part + bd_ref[0]
            else:
                acc_ref[...] += part
        for cc in range(SUBLANES):
            y_ref[pl.ds(cc, tm, stride=SUBLANES), :] = acc_ref[:, cc * LANES:(cc + 1) * LANES]

    @pl.when(nvalid <= 0)
    def _():
        y_ref[...] = jnp.zeros_like(y_ref)


def _experts(tile_expert, tile_valid, tiles_used, xs, w_gate_up, b_gate_up, w_down, b_down):
    rows_padded = xs.shape[0] // SUBLANES
    e, d, f2 = w_gate_up.shape
    f = f2 // 2
    tm = MOE_TM
    grid_spec = pltpu.PrefetchScalarGridSpec(
        num_scalar_prefetch=3,
        grid=(rows_padded // tm,),
        in_specs=[
            pl.BlockSpec((tm * SUBLANES, LANES), lambda i, te, nv, nu: (jnp.minimum(i, nu[0]), 0)),
            pl.BlockSpec((1, d, f2), lambda i, te, nv, nu: (te[i], 0, 0)),
            pl.BlockSpec((1, 1, f2), lambda i, te, nv, nu: (te[i], 0, 0)),
            pl.BlockSpec((1, f, d), lambda i, te, nv, nu: (te[i], 0, 0)),
            pl.BlockSpec((1, 1, d), lambda i, te, nv, nu: (te[i], 0, 0)),
        ],
        out_specs=pl.BlockSpec((tm * SUBLANES, LANES), lambda i, te, nv, nu: (jnp.minimum(i, nu[0]), 0)),
        scratch_shapes=[pltpu.VMEM((d, f2), BF16), pltpu.VMEM((f, d), BF16), pltpu.VMEM((tm, d), F32)],
    )
    return pl.pallas_call(
        functools.partial(_expert_kernel, chunk=512),
        grid_spec=grid_spec,
        out_shape=jax.ShapeDtypeStruct(xs.shape, F32),
        compiler_params=_cparams(("arbitrary",)),
        name="moe_experts",
    )(tile_expert, tile_valid, tiles_used, xs, w_gate_up, b_gate_up.reshape(e, 1, f2), w_down, b_down.reshape(e, 1, d))


def _combine_kernel(pos_ref, y_ref, x2_ref, mw_ref, o_ref, ybuf, sems):
    nt = x2_ref.shape[0]
    half = nt // 2

    def copy(t, kk, hf):
        row = pl.multiple_of(pos_ref[0, kk * nt + t], SUBLANES)
        dst = ybuf.at[kk, pl.ds(pl.multiple_of(t * SUBLANES, SUBLANES), SUBLANES), :]
        return pltpu.make_async_copy(y_ref.at[pl.ds(row, SUBLANES), :], dst, sems.at[hf])

    def issue_half(hf):
        def issue(t, carry):
            for kk in range(TOP_K):
                copy(t, kk, hf).start(priority=kk % 2)
            return carry

        lax.fori_loop(hf * half, (hf + 1) * half, issue, 0, unroll=4)

    def drain_half(hf):
        for kk in range(TOP_K):
            rows = pl.ds(hf * half * SUBLANES, half * SUBLANES)
            pltpu.make_async_copy(y_ref.at[rows, :], ybuf.at[kk, rows, :], sems.at[hf]).wait()

    def combine_half(hf):
        rows = slice(hf * half, (hf + 1) * half)
        mw = mw_ref[rows, :]
        for cc in range(SUBLANES):
            cs = slice(cc * LANES, (cc + 1) * LANES)
            acc = x2_ref[rows, cs]
            for kk in range(TOP_K):
                ys = ybuf.at[kk][pl.ds(hf * half * SUBLANES + cc, half, stride=SUBLANES), :]
                acc = acc + mw[:, kk:kk + 1] * ys
            o_ref[rows, cs] = acc

    issue_half(0)
    issue_half(1)
    drain_half(0)
    combine_half(0)
    drain_half(1)
    combine_half(1)


def _combine(pos, y, x2, mw):
    n, d = x2.shape
    nt = pos.shape[-1] // TOP_K
    return pl.pallas_call(
        _combine_kernel,
        grid=(n // nt,),
        in_specs=[pl.BlockSpec((None, 1, nt * TOP_K), lambda i: (i, 0, 0), memory_space=pltpu.SMEM),
                  pl.BlockSpec(memory_space=pl.ANY),
                  pl.BlockSpec((nt, d), lambda i: (i, 0)),
                  pl.BlockSpec((nt, LANES), lambda i: (i, 0))],
        out_specs=pl.BlockSpec((nt, d), lambda i: (i, 0)),
        out_shape=jax.ShapeDtypeStruct((n, d), F32),
        scratch_shapes=[pltpu.VMEM((TOP_K, nt * SUBLANES, LANES), F32), pltpu.SemaphoreType.DMA((2,))],
        compiler_params=_cparams(("arbitrary",)),
        name="moe_combine",
    )(pos, y, x2, mw)


def _layer(x, mem, mix_norm, w_in, b_forget, fox_q_gain, fox_k_gain, fox_out_gain, gdn_conv_w, gdn_a_log,
           gdn_dt_bias, gdn_out_gain, w_out, xattn_norm, mem_norm, w_xq, w_xkv, xq_gain, xk_gain, w_xo,
           moe_norm, w_router, b_router, w_gate_up, b_gate_up, w_down, b_down):
    batch, seq, d = x.shape
    n = batch * seq
    mem_len = mem.shape[1]
    x2d = x.reshape(n, d)

    o_ff = 3 * FOX_WIDTH
    o_gq = o_ff + FOX_HEADS
    o_ga = o_gq + 3 * GDN_WIDTH
    o_gb = o_ga + GDN_HEADS
    o_gz = o_gb + GDN_HEADS
    w_main = jnp.concatenate([w_in[:, :o_ff], w_in[:, o_gq:o_ga], w_in[:, o_gz:]], axis=1).astype(BF16)
    w_small = jnp.concatenate([w_in[:, o_ff:o_gq], w_in[:, o_ga:o_gz],
                               jnp.zeros((d, LANES - SM_END), F32)], axis=1).astype(BF16)
    prow = jnp.zeros((SUBLANES, LANES), F32)
    prow = prow.at[0, SM_F0:SM_G0].set(b_forget).at[0, SM_G0:SM_B0].set(gdn_dt_bias)
    prow = prow.at[1, SM_G0:SM_B0].set(gdn_a_log)
    pcol = prow[0:2].T
    head_id = jnp.arange(FOX_WIDTH) // FOX_HEAD_DIM
    bd = (head_id[:, None] == head_id[None, :]).astype(BF16)
    row1 = lambda v: v.reshape(1, -1)

    tm = min(512, seq)
    fq, fk, fv, gqkv, gz, sm, smt = _inproj(
        x2d, row1(mix_norm), w_main, w_small, w_small.T, bd,
        row1(jnp.tile(fox_q_gain, FOX_HEADS)), row1(jnp.tile(fox_k_gain, FOX_HEADS)), prow, pcol,
        seq=seq, tm=tm)

    cum = smt[SM_F0:SM_G0].reshape(FOX_HEADS // 2, 2, batch, seq).transpose(2, 0, 1, 3)
    b3 = lambda a: a.reshape(batch, seq, a.shape[-1])
    o_fox = _fox_attention(b3(fq), b3(fk), b3(fv), cum, fox_out_gain.reshape(FOX_HEADS // 2, 1, LANES),
                           batch=batch, seq=seq, blk=min(512, seq))

    u, w, qd, qk, kdt, egl = _gdn_prep(gqkv, gdn_conv_w.T, sm, smt, batch=batch, seq=seq)
    o_gdn = _gdn_scan(u, w, qd, qk, kdt, egl, b3(gz), row1(gdn_out_gain), batch=batch, seq=seq)

    kx, vx = _memkv(mem.reshape(batch * mem_len, d), row1(mem_norm), w_xkv.astype(BF16), row1(xk_gain),
                    batch=batch, mem_len=mem_len)
    w_r = jnp.concatenate([w_router, jnp.zeros((d, LANES - N_EXPERTS), F32)], axis=1)
    w_r = jnp.stack(_split_bf16(w_r))
    b_r = jnp.concatenate([b_router, jnp.full((LANES - N_EXPERTS,), -jnp.inf, F32)]).reshape(1, LANES)
    x2, h3, mi, mw, cnt = _mid(
        x2d, o_fox.reshape(n, FOX_WIDTH), o_gdn.reshape(n, GDN_WIDTH), w_out.astype(BF16), row1(xattn_norm),
        w_xq.astype(BF16), row1(xq_gain), kx, vx, w_xo.astype(BF16), row1(moe_norm), w_r, b_r,
        seq=seq, mem_len=mem_len, tm=min(ROUTE_TOKENS, seq))

    counts = cnt[0, :N_EXPERTS]
    padded = ((counts + MOE_TM - 1) // MOE_TM) * MOE_TM
    ends = jnp.cumsum(padded)
    starts = ends - padded
    rows_padded = n * TOP_K + N_EXPERTS * MOE_TM
    n_tiles = rows_padded // MOE_TM
    tile_start = jnp.arange(n_tiles, dtype=jnp.int32) * MOE_TM
    tile_expert = jnp.zeros((n_tiles,), jnp.int32)
    tile_limit = jnp.zeros((n_tiles,), jnp.int32)
    last_used = jnp.zeros((), jnp.int32)
    experts = mi[:, :TOP_K, :]
    base = jnp.zeros_like(experts)
    for e in range(N_EXPERTS):
        inside = (tile_start >= starts[e]) & (tile_start < ends[e])
        tile_expert = jnp.where(inside, e, tile_expert)
        tile_limit = jnp.where(inside, starts[e] + counts[e], tile_limit)
        last_used = jnp.where(counts[e] > 0, e, last_used)
        base = jnp.where(experts == e, starts[e], base)
    tile_valid = jnp.clip(tile_limit - tile_start, 0, MOE_TM).astype(jnp.int32)
    tile_expert = jnp.where(tile_valid > 0, tile_expert, last_used).astype(jnp.int32)

    pos = ((base + mi[:, TOP_K:, :]) * SUBLANES).astype(jnp.int32).reshape(mi.shape[0], 1, -1)
    tiles_used = (ends[N_EXPERTS - 1] // MOE_TM).astype(jnp.int32).reshape(1)

    xs = _dispatch(h3, pos, rows_padded=rows_padded)
    y = _experts(tile_expert, tile_valid, tiles_used, xs, w_gate_up, b_gate_up, w_down, b_down)
    out = _combine(pos, y, x2, mw)
    return out.reshape(batch, seq, d)


def kernel(x, mem, mix_norm, w_in, b_forget, fox_q_gain, fox_k_gain, fox_out_gain, gdn_conv_w, gdn_a_log,
           gdn_dt_bias, gdn_out_gain, w_out, xattn_norm, mem_norm, w_xq, w_xkv, xq_gain, xk_gain, w_xo,
           moe_norm, w_router, b_router, w_gate_up, b_gate_up, w_down, b_down):
    depth = mix_norm.shape[0]
    for l in range(depth):
        x = _layer(x, mem, mix_norm[l], w_in[l], b_forget[l], fox_q_gain[l], fox_k_gain[l], fox_out_gain[l],
                   gdn_conv_w[l], gdn_a_log[l], gdn_dt_bias[l], gdn_out_gain[l], w_out[l], xattn_norm[l],
                   mem_norm[l], w_xq[l], w_xkv[l], xq_gain[l], xk_gain[l], w_xo[l], moe_norm[l], w_router[l],
                   b_router[l], w_gate_up[l], b_gate_up[l], w_down[l], b_down[l])
    return x
```

```python
import functools

import jax
import jax.numpy as jnp
from jax import lax
from jax.experimental import pallas as pl
from jax.experimental.pallas import tpu as pltpu

F32 = jnp.float32
BF16 = jnp.bfloat16

EPS = 1e-6
FOX_HEADS, FOX_HEAD_DIM = 8, 64
GDN_HEADS, GDN_HEAD_DIM = 4, 128
FOX_WIDTH = FOX_HEADS * FOX_HEAD_DIM
GDN_WIDTH = GDN_HEADS * GDN_HEAD_DIM
CONV_WIDTH = 4
CHUNK = 64
XA_HEADS = 4
N_EXPERTS = 32
TOP_K = 4
SWIGLU_LIMIT = 7.0
SWIGLU_ALPHA = 1.702
LOG2E = 1.4426950408889634

LANES = 128
SUBLANES = 8
VMEM_LIMIT = 52 * 1024 * 1024

SM_F0, SM_G0, SM_B0, SM_END = 0, 8, 12, 16

NT_DIMS = (((1,), (1,)), ((), ()))


def _cparams(sem):
    return pltpu.CompilerParams(dimension_semantics=sem, vmem_limit_bytes=VMEM_LIMIT)


def _rms(x, gain):
    return x * lax.rsqrt(jnp.mean(x * x, axis=-1, keepdims=True) + EPS) * gain


def _split_bf16(x):
    hi = x.astype(BF16)
    return hi, (x - hi.astype(F32)).astype(BF16)


def _dot_mask_left(mask, x):
    hi, lo = _split_bf16(x)
    mb = mask.astype(BF16)
    return jnp.dot(mb, hi, preferred_element_type=F32) + jnp.dot(mb, lo, preferred_element_type=F32)


def _dot_mask_right(x, mask):
    hi, lo = _split_bf16(x)
    mb = mask.astype(BF16)
    return jnp.dot(hi, mb, preferred_element_type=F32) + jnp.dot(lo, mb, preferred_element_type=F32)


def _softplus(t):
    return jnp.maximum(t, 0.0) + jnp.log1p(jnp.exp(-jnp.abs(t)))


def _sigmoid(t):
    return 1.0 / (1.0 + jnp.exp(-t))


def _small_act(v, bias, alog, idx):
    t = v + bias
    tail = jnp.log1p(jnp.exp(-jnp.abs(t)))
    log_f = jnp.minimum(t, 0.0) - tail
    g = -jnp.exp(alog) * (jnp.maximum(t, 0.0) + tail)
    beta = _sigmoid(v)
    return jnp.where(idx < SM_G0, log_f, jnp.where(idx < SM_B0, g, jnp.where(idx < SM_END, beta, 0.0)))


def _inproj_kernel(x_ref, gain_ref, w_ref, ws_ref, wst_ref, bd_ref, qg_ref, kg_ref, prow_ref, pcol_ref,
                   fq_ref, fk_ref, fv_ref, gqkv_ref, gz_ref, sm_ref, smt_ref, carry_r, carry_c,
                   *, tm, tiles_per_seq):
    i = pl.program_id(0)
    hb = _rms(x_ref[...], gain_ref[...]).astype(BF16)

    def proj(lo, hi):
        return jnp.dot(hb, w_ref[:, lo:hi], preferred_element_type=F32)

    def headnorm(p, g):
        ss = jnp.dot((p * p).astype(BF16), bd_ref[...], preferred_element_type=F32)
        return p * lax.rsqrt(ss * (1.0 / FOX_HEAD_DIM) + EPS) * g

    w0 = FOX_WIDTH
    fq_ref[...] = (headnorm(proj(0, w0), qg_ref[...]) * (FOX_HEAD_DIM ** -0.5 * LOG2E)).astype(BF16)
    fk_ref[...] = headnorm(proj(w0, 2 * w0), kg_ref[...]).astype(BF16)
    fv_ref[...] = proj(2 * w0, 3 * w0).astype(BF16)
    g0 = 3 * w0
    gqkv_ref[...] = proj(g0, g0 + 3 * GDN_WIDTH)
    gz_ref[...] = proj(g0 + 3 * GDN_WIDTH, g0 + 4 * GDN_WIDTH)

    @pl.when(i % tiles_per_seq == 0)
    def _():
        carry_r[...] = jnp.zeros_like(carry_r)
        carry_c[...] = jnp.zeros_like(carry_c)

    sm = jnp.dot(hb, ws_ref[...], preferred_element_type=F32)
    smt = lax.dot_general(wst_ref[...], hb, NT_DIMS, preferred_element_type=F32)
    lane = lax.broadcasted_iota(jnp.int32, (1, LANES), 1)
    srow = lax.broadcasted_iota(jnp.int32, (LANES, 1), 0)
    vals = _small_act(sm, prow_ref[0:1, :], prow_ref[1:2, :], lane)
    vals_t = _small_act(smt, pcol_ref[:, 0:1], pcol_ref[:, 1:2], srow)

    r = lax.broadcasted_iota(jnp.int32, (tm, tm), 0)
    c = lax.broadcasted_iota(jnp.int32, (tm, tm), 1)
    cum = _dot_mask_left(c <= r, vals) + carry_r[...]
    cum_t = _dot_mask_right(vals_t, r <= c) + carry_c[...]
    carry_r[...] = cum[tm - 1:tm, :]
    carry_c[...] = cum_t[:, tm - 1:tm]
    sm_ref[...] = jnp.where(lane < SM_G0, cum, vals)
    smt_ref[...] = jnp.where(srow < SM_G0, cum_t, vals_t)


def _inproj(x2d, gain, w_main, w_small, w_small_t, bd, qg, kg, prow, pcol, *, seq, tm):
    n, d = x2d.shape
    wm = w_main.shape[1]
    full = lambda shape: pl.BlockSpec(shape, lambda i: (0,) * len(shape))
    rows = lambda width: pl.BlockSpec((tm, width), lambda i: (i, 0))
    out_shape = (
        jax.ShapeDtypeStruct((n, FOX_WIDTH), BF16),
        jax.ShapeDtypeStruct((n, FOX_WIDTH), BF16),
        jax.ShapeDtypeStruct((n, FOX_WIDTH), BF16),
        jax.ShapeDtypeStruct((n, 3 * GDN_WIDTH), F32),
        jax.ShapeDtypeStruct((n, GDN_WIDTH), F32),
        jax.ShapeDtypeStruct((n, LANES), F32),
        jax.ShapeDtypeStruct((LANES, n), F32),
    )
    return pl.pallas_call(
        functools.partial(_inproj_kernel, tm=tm, tiles_per_seq=seq // tm),
        grid=(n // tm,),
        in_specs=[rows(d), full((1, d)), full((d, wm)), full((d, LANES)), full((LANES, d)),
                  full((FOX_WIDTH, FOX_WIDTH)), full((1, FOX_WIDTH)), full((1, FOX_WIDTH)),
                  full((SUBLANES, LANES)), full((LANES, 2))],
        out_specs=(rows(FOX_WIDTH), rows(FOX_WIDTH), rows(FOX_WIDTH), rows(3 * GDN_WIDTH), rows(GDN_WIDTH),
                   rows(LANES), pl.BlockSpec((LANES, tm), lambda i: (0, i))),
        out_shape=out_shape,
        scratch_shapes=[pltpu.VMEM((1, LANES), F32), pltpu.VMEM((LANES, 1), F32)],
        compiler_params=_cparams(("arbitrary",)),
        name="inproj",
    )(x2d, gain, w_main, w_small, w_small_t, bd, qg, kg, prow, pcol)


def _fox_kernel(q_ref, k_ref, v_ref, cum_ref, gain_ref, o_ref, m0_ref, m1_ref, acc0_ref, acc1_ref, *, tq, blk):
    qi = pl.program_id(2)
    lane = lax.broadcasted_iota(jnp.int32, (1, LANES), 1)
    lo = lane < FOX_HEAD_DIM
    q = q_ref[...]
    zero = jnp.zeros_like(q)
    q_heads = (jnp.where(lo, q, zero), jnp.where(lo, zero, q))
    q0 = pl.multiple_of(qi * tq, tq)
    c_ref = [cum_ref[hh:hh + 1, pl.ds(q0, LANES)][:, 0:1] for hh in range(2)]
    head_lanes = (lo, jnp.logical_not(lo))
    sum_lane = (FOX_HEAD_DIM, 0)
    ones_col = [jnp.where(lane == sum_lane[hh], 1.0, 0.0).astype(BF16) for hh in range(2)]
    m_refs = (m0_ref, m1_ref)
    acc_refs = (acc0_ref, acc1_ref)
    for hh in range(2):
        m_refs[hh][...] = jnp.full_like(m_refs[hh], -1e30)
        acc_refs[hh][...] = jnp.zeros_like(acc_refs[hh])

    def step(k0, r0=None):
        rows = slice(0 if r0 is None else r0, tq)
        nr = rows.stop - rows.start
        kb = k_ref[pl.ds(k0, blk), :]
        vb = v_ref[pl.ds(k0, blk), :]
        scores = [lax.dot_general(q_heads[hh][rows], kb, NT_DIMS, preferred_element_type=F32) for hh in range(2)]
        for hh in range(2):
            s = scores[hh] + (c_ref[hh] - cum_ref[hh:hh + 1, pl.ds(k0, blk)]) * LOG2E
            if r0 is not None:
                r = lax.broadcasted_iota(jnp.int32, (nr, blk), 0)
                c = lax.broadcasted_iota(jnp.int32, (nr, blk), 1)
                s = jnp.where(c <= r, s, -jnp.inf)
            m_old = m_refs[hh][rows, :]
            m_new = jnp.maximum(m_old, jnp.max(s, axis=-1, keepdims=True))
            alpha = jnp.exp2(m_old - m_new)
            p = jnp.exp2(s - jnp.concatenate([m_new] * (blk // LANES), axis=1))
            v_h = jnp.where(head_lanes[hh], vb, ones_col[hh])
            pv = jnp.dot(p.astype(BF16), v_h, preferred_element_type=F32)
            acc_refs[hh][rows, :] = acc_refs[hh][rows, :] * alpha + pv
            m_refs[hh][rows, :] = m_new

    def body(kp, carry):
        step(pl.multiple_of(kp * (2 * blk), blk))
        step(pl.multiple_of(kp * (2 * blk) + blk, blk))
        return carry

    n_full = qi * (tq // blk)
    lax.fori_loop(0, n_full // 2, body, 0)

    @pl.when(n_full % 2 == 1)
    def _():
        step(pl.multiple_of((n_full - 1) * blk, blk))

    for j in range(tq // blk):
        step(pl.multiple_of(q0 + j * blk, blk), r0=j * blk)

    acc0, acc1 = acc0_ref[...], acc1_ref[...]
    l0 = acc0[:, sum_lane[0]:sum_lane[0] + 1]
    l1 = acc1[:, sum_lane[1]:sum_lane[1] + 1]
    o = jnp.where(lo, acc0 * (1.0 / l0), acc1 * (1.0 / l1))
    o2 = o * o
    ss_lo = jnp.sum(jnp.where(lo, o2, 0.0), axis=-1, keepdims=True)
    ss_hi = jnp.sum(jnp.where(lo, 0.0, o2), axis=-1, keepdims=True)
    ms = jnp.where(lo, ss_lo, ss_hi) * (1.0 / FOX_HEAD_DIM)
    o_ref[...] = (o * lax.rsqrt(ms + EPS) * gain_ref[...]).astype(o_ref.dtype)


def _fox_attention(fq, fk, fv, cum, gain_pairs, *, batch, seq, tq, blk):
    npairs = FOX_HEADS // 2
    return pl.pallas_call(
        functools.partial(_fox_kernel, tq=tq, blk=blk),
        grid=(batch, npairs, seq // tq),
        in_specs=[
            pl.BlockSpec((None, tq, LANES), lambda b, j, i: (b, i, j)),
            pl.BlockSpec((None, seq, LANES), lambda b, j, i: (b, 0, j)),
            pl.BlockSpec((None, seq, LANES), lambda b, j, i: (b, 0, j)),
            pl.BlockSpec((None, None, 2, seq), lambda b, j, i: (b, j, 0, 0)),
            pl.BlockSpec((None, 1, LANES), lambda b, j, i: (j, 0, 0)),
        ],
        out_specs=pl.BlockSpec((None, tq, LANES), lambda b, j, i: (b, i, j)),
        out_shape=jax.ShapeDtypeStruct((batch, seq, FOX_WIDTH), BF16),
        scratch_shapes=[pltpu.VMEM((tq, LANES), F32) for _ in range(4)],
        compiler_params=_cparams(("parallel", "parallel", "arbitrary")),
        name="fox_attention",
    )(fq, fk, fv, cum, gain_pairs)


GDN_BLK = 4 * CHUNK
QK_BLK = 2 * CHUNK


def _gdn_prep_kernel(x_ref, halo_ref, cw_ref, sm_ref, smt_ref,
                     u_ref, w_ref, qd_ref, qk_ref, kdt_ref, egl_ref, xpad_ref):
    i = pl.program_id(1)
    nb = GDN_BLK
    halo = halo_ref[...]
    xpad_ref[0:SUBLANES, :] = jnp.where(i > 0, halo, jnp.zeros_like(halo))
    xpad_ref[SUBLANES:, :] = x_ref[...]
    y = None
    for j in range(CONV_WIDTH):
        start = SUBLANES - (CONV_WIDTH - 1) + j
        term = cw_ref[j:j + 1, :] * xpad_ref[start:start + nb, :]
        y = term if y is None else y + term
    y = y * _sigmoid(y)

    r = lax.broadcasted_iota(jnp.int32, (nb, nb), 0)
    c = lax.broadcasted_iota(jnp.int32, (nb, nb), 1)
    chunk_shift = CHUNK.bit_length() - 1
    same = jnp.right_shift(r, chunk_shift) == jnp.right_shift(c, chunk_shift)
    incl = same & (c <= r)
    strict = same & (c < r)
    sm = sm_ref[...]
    g_cum = _dot_mask_left(incl, sm)
    g_tot = _dot_mask_left(same, sm)
    g_cum_t = _dot_mask_right(smt_ref[...], same & (r <= c))
    eye = (r == c).astype(F32)

    powers, t_invs, rhs = [], [], []
    for h in range(GDN_HEADS):
        sl = slice(h * GDN_HEAD_DIM, (h + 1) * GDN_HEAD_DIM)
        q = y[:, h * GDN_HEAD_DIM:(h + 1) * GDN_HEAD_DIM]
        k = y[:, GDN_WIDTH + h * GDN_HEAD_DIM:GDN_WIDTH + (h + 1) * GDN_HEAD_DIM]
        v = y[:, 2 * GDN_WIDTH + h * GDN_HEAD_DIM:2 * GDN_WIDTH + (h + 1) * GDN_HEAD_DIM]
        qn = q * lax.rsqrt(jnp.sum(q * q, axis=-1, keepdims=True) + EPS) * GDN_HEAD_DIM ** -0.5
        kn = k * lax.rsqrt(jnp.sum(k * k, axis=-1, keepdims=True) + EPS)
        gc = g_cum[:, SM_G0 + h:SM_G0 + h + 1]
        gl = g_tot[:, SM_G0 + h:SM_G0 + h + 1]
        gr = g_cum_t[SM_G0 + h:SM_G0 + h + 1, :]
        beta = sm[:, SM_B0 + h:SM_B0 + h + 1]
        decay = jnp.where(incl, jnp.exp(jnp.where(incl, gc - gr, 0.0)), 0.0)
        qb, kb = qn.astype(BF16), kn.astype(BF16)
        kk = lax.dot_general(kb, kb, NT_DIMS, preferred_element_type=F32)
        a = jnp.where(strict, beta * kk * decay, 0.0)
        eg = jnp.exp(gc)
        powers.append(a)
        t_invs.append(eye - a)
        rhs.append(jnp.concatenate([(v * beta).astype(BF16), (kn * (beta * eg)).astype(BF16)], axis=1))
        qk = lax.dot_general(qb, kb, NT_DIMS, preferred_element_type=F32)
        qk = jnp.where(incl, qk * decay, 0.0).astype(BF16)
        qk_ref[:, h * QK_BLK:(h + 1) * QK_BLK] = jnp.concatenate(
            [qk[j * QK_BLK:(j + 1) * QK_BLK, j * QK_BLK:(j + 1) * QK_BLK] for j in range(nb // QK_BLK)], axis=0)
        qd_ref[:, sl] = (qn * eg).astype(BF16)
        kdt_ref[sl, :] = (kn * jnp.exp(gl - gc)).T.astype(BF16)
        egl_ref[:, sl] = jnp.broadcast_to(jnp.exp(gl), (nb, GDN_HEAD_DIM))

    for _ in range(5):
        for h in range(GDN_HEADS):
            pb = powers[h].astype(BF16)
            powers[h] = jnp.dot(pb, pb, preferred_element_type=F32)
        for h in range(GDN_HEADS):
            t_invs[h] = t_invs[h] + jnp.dot(t_invs[h].astype(BF16), powers[h].astype(BF16),
                                            preferred_element_type=F32)
    for h in range(GDN_HEADS):
        sl = slice(h * GDN_HEAD_DIM, (h + 1) * GDN_HEAD_DIM)
        uw = jnp.dot(t_invs[h].astype(BF16), rhs[h], preferred_element_type=F32)
        u_ref[:, sl] = uw[:, :GDN_HEAD_DIM]
        w_ref[:, sl] = uw[:, GDN_HEAD_DIM:].astype(BF16)


def _gdn_prep(gqkv, conv_w_t, sm, smt, *, batch, seq):
    nb = GDN_BLK
    bps = seq // nb
    hps = nb // SUBLANES
    width = 3 * GDN_WIDTH
    row = lambda w: pl.BlockSpec((None, nb, w), lambda b, i: (b, i, 0))
    return pl.pallas_call(
        _gdn_prep_kernel,
        grid=(batch, bps),
        in_specs=[
            pl.BlockSpec((nb, width), lambda b, i: (b * bps + i, 0)),
            pl.BlockSpec((SUBLANES, width), lambda b, i: (jnp.maximum((b * bps + i) * hps - 1, 0), 0)),
            pl.BlockSpec((CONV_WIDTH, width), lambda b, i: (0, 0)),
            pl.BlockSpec((nb, LANES), lambda b, i: (b * bps + i, 0)),
            pl.BlockSpec((LANES, nb), lambda b, i: (0, b * bps + i)),
        ],
        out_specs=(row(GDN_WIDTH), row(GDN_WIDTH), row(GDN_WIDTH), row(GDN_HEADS * QK_BLK),
                   pl.BlockSpec((None, GDN_WIDTH, nb), lambda b, i: (b, 0, i)), row(GDN_WIDTH)),
        out_shape=(
            jax.ShapeDtypeStruct((batch, seq, GDN_WIDTH), F32),
            jax.ShapeDtypeStruct((batch, seq, GDN_WIDTH), BF16),
            jax.ShapeDtypeStruct((batch, seq, GDN_WIDTH), BF16),
            jax.ShapeDtypeStruct((batch, seq, GDN_HEADS * QK_BLK), BF16),
            jax.ShapeDtypeStruct((batch, GDN_WIDTH, seq), BF16),
            jax.ShapeDtypeStruct((batch, seq, GDN_WIDTH), F32),
        ),
        scratch_shapes=[pltpu.VMEM((nb + SUBLANES, width), F32)],
        compiler_params=_cparams(("parallel", "parallel")),
        name="gdn_prep",
    )(gqkv, gqkv, conv_w_t, sm, smt)


def _gdn_scan_kernel(u_ref, w_ref, qd_ref, qk_ref, kdt_ref, egl_ref, z_ref, gain_ref, o_ref, s_ref, vz_ref,
                     *, batch):
    pb = QK_BLK

    @pl.when(pl.program_id(0) == 0)
    def _():
        s_ref[...] = jnp.zeros_like(s_ref)

    vz_ref[...] = jnp.zeros_like(vz_ref)
    for cidx in range(GDN_BLK // CHUNK):
        rows = slice(cidx * CHUNK, (cidx + 1) * CHUNK)
        blk_rows = slice(cidx * CHUNK // pb * pb, (cidx * CHUNK // pb + 1) * pb)
        for b in range(batch):
            for h in range(GDN_HEADS):
                bh = b * GDN_HEADS + h
                sl = slice(h * GDN_HEAD_DIM, (h + 1) * GDN_HEAD_DIM)
                s_old = s_ref[bh]
                lhs1 = jnp.concatenate([w_ref[b, rows, sl], qd_ref[b, rows, sl]], axis=0)
                r1 = jnp.dot(lhs1, s_old.astype(BF16), preferred_element_type=F32)
                v_new = u_ref[b, rows, sl] - r1[:CHUNK]
                vz_ref[bh, rows, :] = v_new.astype(BF16)
                lhs2 = jnp.concatenate([qk_ref[b, rows, h * pb:(h + 1) * pb], kdt_ref[b, sl, blk_rows]], axis=0)
                r2 = jnp.dot(lhs2, vz_ref[bh, blk_rows, :], preferred_element_type=F32)
                vz_ref[bh, rows, :] = jnp.zeros((CHUNK, GDN_HEAD_DIM), BF16)
                last = egl_ref[b, (cidx + 1) * CHUNK - 1:(cidx + 1) * CHUNK, sl]
                s_ref[bh] = s_old * last + r2[CHUNK:]
                o = r1[CHUNK:] + r2[:CHUNK]
                z = z_ref[b, rows, sl]
                o_ref[b, rows, sl] = (_rms(o, gain_ref[...]) * (z * _sigmoid(z))).astype(o_ref.dtype)


def _gdn_scan(u, w, qd, qk, kdt, egl, z, gain, *, batch, seq):
    nb = GDN_BLK
    row = lambda width: pl.BlockSpec((batch, nb, width), lambda i: (0, i, 0))
    return pl.pallas_call(
        functools.partial(_gdn_scan_kernel, batch=batch),
        grid=(seq // nb,),
        in_specs=[row(GDN_WIDTH), row(GDN_WIDTH), row(GDN_WIDTH), row(GDN_HEADS * QK_BLK),
                  pl.BlockSpec((batch, GDN_WIDTH, nb), lambda i: (0, 0, i)), row(GDN_WIDTH), row(GDN_WIDTH),
                  pl.BlockSpec((1, GDN_HEAD_DIM), lambda i: (0, 0))],
        out_specs=row(GDN_WIDTH),
        out_shape=jax.ShapeDtypeStruct((batch, seq, GDN_WIDTH), BF16),
        scratch_shapes=[pltpu.VMEM((batch * GDN_HEADS, GDN_HEAD_DIM, GDN_HEAD_DIM), F32),
                        pltpu.VMEM((batch * GDN_HEADS, nb, GDN_HEAD_DIM), BF16)],
        compiler_params=_cparams(("arbitrary",)),
        name="gdn_scan",
    )(u, w, qd, qk, kdt, egl, z, gain)


def _memkv_kernel(m_ref, gain_ref, w_ref, kg_ref, k_ref, v_ref):
    d = m_ref.shape[-1]
    hd = d // XA_HEADS
    mb = _rms(m_ref[...], gain_ref[...]).astype(BF16)
    kv = jnp.dot(mb, w_ref[...], preferred_element_type=F32)
    for h in range(XA_HEADS):
        sl = slice(h * hd, (h + 1) * hd)
        k_ref[:, sl] = _rms(kv[:, sl], kg_ref[...]).astype(BF16)
    v_ref[...] = kv[:, d:].astype(BF16)


def _memkv(mem2d, gain, w_xkv, xk_gain, *, batch, mem_len):
    d = mem2d.shape[-1]
    full = lambda shape: pl.BlockSpec(shape, lambda b: (0,) * len(shape))
    row = pl.BlockSpec((mem_len, d), lambda b: (b, 0))
    return pl.pallas_call(
        _memkv_kernel,
        grid=(batch,),
        in_specs=[row, full((1, d)), full((d, 2 * d)), full((1, d // XA_HEADS))],
        out_specs=(row, row),
        out_shape=(jax.ShapeDtypeStruct(mem2d.shape, BF16), jax.ShapeDtypeStruct(mem2d.shape, BF16)),
        compiler_params=_cparams(("parallel",)),
        name="mem_kv",
    )(mem2d, gain, w_xkv, xk_gain)


def _mid_kernel(x_ref, of_ref, og_ref, wo_ref, xg_ref, wq_ref, qg_ref, k_ref, v_ref, wxo_ref, mg_ref,
                wr_ref, br_ref,
                x2_ref, h3_ref, mi_ref, mw_ref, cnt_ref, carry_ref, *, tm):
    i = pl.program_id(0)
    d = x_ref.shape[-1]
    hd = d // XA_HEADS
    x1 = (x_ref[...]
          + jnp.dot(of_ref[...], wo_ref[0:FOX_WIDTH, :], preferred_element_type=F32)
          + jnp.dot(og_ref[...], wo_ref[FOX_WIDTH:, :], preferred_element_type=F32))
    h2 = _rms(x1, xg_ref[...]).astype(BF16)
    q = jnp.dot(h2, wq_ref[...], preferred_element_type=F32)
    heads = []
    for h in range(XA_HEADS):
        sl = slice(h * hd, (h + 1) * hd)
        qn = (_rms(q[:, sl], qg_ref[...]) * hd ** -0.5).astype(BF16)
        s = lax.dot_general(qn, k_ref[:, sl], NT_DIMS, preferred_element_type=F32)
        p = jnp.exp(s - jnp.max(s, axis=-1, keepdims=True))
        p = p * (1.0 / jnp.sum(p, axis=-1, keepdims=True))
        heads.append(jnp.dot(p.astype(BF16), v_ref[:, sl], preferred_element_type=F32).astype(BF16))
    x2 = x1 + jnp.dot(jnp.concatenate(heads, axis=-1), wxo_ref[...], preferred_element_type=F32)
    x2_ref[...] = x2
    h3 = _rms(x2, mg_ref[...])
    for cc in range(SUBLANES):
        h3_ref[pl.ds(cc, tm, stride=SUBLANES), :] = h3[:, cc * LANES:(cc + 1) * LANES]
    h_hi, h_lo = _split_bf16(h3)
    logits = (jnp.dot(h_hi, wr_ref[0], preferred_element_type=F32)
              + jnp.dot(h_lo, wr_ref[0], preferred_element_type=F32)
              + jnp.dot(h_hi, wr_ref[1], preferred_element_type=F32)) + br_ref[...]
    lane = lax.broadcasted_iota(jnp.int32, (tm, LANES), 1)
    work = logits
    vals, idxs = [], []
    onehot = jnp.zeros((tm, LANES), F32)
    for _ in range(TOP_K):
        mx = jnp.max(work, axis=-1, keepdims=True)
        idx = jnp.min(jnp.where(work == mx, lane, LANES), axis=-1, keepdims=True)
        sel = lane == idx
        onehot = jnp.where(sel, 1.0, onehot)
        work = jnp.where(sel, -jnp.inf, work)
        vals.append(mx)
        idxs.append(idx)
    es = [jnp.exp(v - vals[0]) for v in vals]
    inv_denom = 1.0 / (es[0] + es[1] + es[2] + es[3])

    @pl.when(i == 0)
    def _():
        carry_ref[...] = jnp.zeros_like(carry_ref)

    r = lax.broadcasted_iota(jnp.int32, (tm, tm), 0)
    c = lax.broadcasted_iota(jnp.int32, (tm, tm), 1)
    before = jnp.dot((c < r).astype(BF16), onehot.astype(BF16), preferred_element_type=F32) + carry_ref[...]
    mi = jnp.zeros((tm, LANES), F32)
    mw = jnp.zeros((tm, LANES), F32)
    for kk in range(TOP_K):
        rank = jnp.sum(jnp.where(lane == idxs[kk], before, 0.0), axis=-1, keepdims=True)
        mi = jnp.where(lane == kk, idxs[kk].astype(F32), mi)
        mi = jnp.where(lane == TOP_K + kk, rank, mi)
        mw = jnp.where(lane == kk, es[kk] * inv_denom, mw)
    mi_ref[...] = mi.T[0:2 * TOP_K, :].astype(jnp.int32)
    mw_ref[...] = mw
    total = carry_ref[...] + jnp.sum(onehot, axis=0, keepdims=True)
    carry_ref[...] = total
    cnt_ref[...] = jnp.broadcast_to(total, cnt_ref.shape).astype(jnp.int32)


def _mid(x2d, o_fox, o_gdn, w_out, xg, w_xq, xq_gain, kx, vx, w_xo, mg, w_r, b_r, *, seq, mem_len, tm):
    n, d = x2d.shape
    full = lambda shape: pl.BlockSpec(shape, lambda i: (0,) * len(shape))
    rows = lambda width: pl.BlockSpec((tm, width), lambda i: (i, 0))
    mem = pl.BlockSpec((mem_len, d), lambda i: (i // (seq // tm), 0))
    return pl.pallas_call(
        functools.partial(_mid_kernel, tm=tm),
        grid=(n // tm,),
        in_specs=[rows(d), rows(FOX_WIDTH), rows(GDN_WIDTH), full((d, d)), full((1, d)), full((d, d)),
                  full((1, d // XA_HEADS)), mem, mem, full((d, d)), full((1, d)), full((2, d, LANES)),
                  full((1, LANES))],
        out_specs=(rows(d), pl.BlockSpec((tm * d // LANES, LANES), lambda i: (i, 0)),
                   pl.BlockSpec((None, 2 * TOP_K, tm), lambda i: (i, 0, 0)), rows(LANES),
                   full((SUBLANES, LANES))),
        out_shape=(jax.ShapeDtypeStruct((n, d), F32), jax.ShapeDtypeStruct((n * d // LANES, LANES), F32),
                   jax.ShapeDtypeStruct((n // tm, 2 * TOP_K, tm), jnp.int32),
                   jax.ShapeDtypeStruct((n, LANES), F32),
                   jax.ShapeDtypeStruct((SUBLANES, LANES), jnp.int32)),
        scratch_shapes=[pltpu.VMEM((1, LANES), F32)],
        compiler_params=_cparams(("arbitrary",)),
        name="outproj_xattn_router",
    )(x2d, o_fox, o_gdn, w_out, xg, w_xq, xq_gain, kx, vx, w_xo, mg, w_r, b_r)


MOE_TM = 512
ROUTE_TOKENS = 512


def _dispatch_kernel(pos_ref, src_ref, dst_ref, sem):
    nt = src_ref.shape[0] // SUBLANES

    def copy(t, kk):
        src = src_ref.at[pl.ds(pl.multiple_of(t * SUBLANES, SUBLANES), SUBLANES), :]
        row = pl.multiple_of(pos_ref[0, kk * nt + t], SUBLANES)
        return pltpu.make_async_copy(src, dst_ref.at[pl.ds(row, SUBLANES), :], sem)

    def issue(t, carry):
        for kk in range(TOP_K):
            copy(t, kk).start(priority=kk % 2)
        return carry

    lax.fori_loop(0, nt, issue, 0, unroll=4)

    for kk in range(TOP_K):
        pltpu.make_async_copy(src_ref, dst_ref.at[pl.ds(0, nt * SUBLANES), :], sem).wait()


def _dispatch(h3_tiles, pos, *, rows_padded):
    n = h3_tiles.shape[0] // SUBLANES
    nt = pos.shape[-1] // TOP_K
    return pl.pallas_call(
        _dispatch_kernel,
        grid=(n // nt,),
        in_specs=[pl.BlockSpec((None, 1, nt * TOP_K), lambda i: (i, 0, 0), memory_space=pltpu.SMEM),
                  pl.BlockSpec((nt * SUBLANES, LANES), lambda i: (i, 0))],
        out_specs=pl.BlockSpec(memory_space=pl.ANY),
        out_shape=jax.ShapeDtypeStruct((rows_padded * SUBLANES, LANES), F32),
        scratch_shapes=[pltpu.SemaphoreType.DMA],
        compiler_params=pltpu.CompilerParams(dimension_semantics=("arbitrary",), has_side_effects=True),
        name="moe_dispatch",
    )(pos, h3_tiles)


def _expert_kernel(te_ref, nv_ref, nu_ref, xs_ref, wgu_ref, bgu_ref, wd_ref, bd_ref, y_ref, wgu_bf, wd_bf, acc_ref,
                   *, chunk):
    i = pl.program_id(0)
    tm = xs_ref.shape[0] // SUBLANES
    d = wd_bf.shape[1]
    f = wd_bf.shape[0]
    nvalid = nv_ref[i]
    first = jnp.logical_or(i == 0, te_ref[i] != te_ref[jnp.maximum(i - 1, 0)])

    @pl.when(jnp.logical_and(first, nvalid > 0))
    def _():
        wgu_bf[...] = wgu_ref[0].astype(BF16)
        wd_bf[...] = wd_ref[0].astype(BF16)

    @pl.when(nvalid > 0)
    def _():
        row = lax.broadcasted_iota(jnp.int32, (tm, 1), 0)
        x = jnp.concatenate([xs_ref[pl.ds(cc, tm, stride=SUBLANES), :] for cc in range(SUBLANES)], axis=-1)
        x = jnp.where(row < nvalid, x, 0.0).astype(BF16)
        for j in range(f // chunk):
            cs = slice(j * chunk, (j + 1) * chunk)
            us = slice(f + j * chunk, f + (j + 1) * chunk)
            g = jnp.dot(x, wgu_bf[:, cs], preferred_element_type=F32) + bgu_ref[0, :, cs]
            u = jnp.dot(x, wgu_bf[:, us], preferred_element_type=F32) + bgu_ref[0, :, us]
            gate = jnp.minimum(g, SWIGLU_LIMIT)
            up = jnp.clip(u, -SWIGLU_LIMIT, SWIGLU_LIMIT)
            act = ((up + 1.0) * (gate * _sigmoid(SWIGLU_ALPHA * gate))).astype(BF16)
            part = jnp.dot(act, wd_bf[cs, :], preferred_element_type=F32)
            if j == 0:
                acc_ref[...] = part + bd_ref[0]
            else:
                acc_ref[...] += part
        for cc in range(SUBLANES):
            y_ref[pl.ds(cc, tm, stride=SUBLANES), :] = acc_ref[:, cc * LANES:(cc + 1) * LANES]

    @pl.when(nvalid <= 0)
    def _():
        y_ref[...] = jnp.zeros_like(y_ref)


def _experts(tile_expert, tile_valid, tiles_used, xs, w_gate_up, b_gate_up, w_down, b_down):
    rows_padded = xs.shape[0] // SUBLANES
    e, d, f2 = w_gate_up.shape
    f = f2 // 2
    tm = MOE_TM
    grid_spec = pltpu.PrefetchScalarGridSpec(
        num_scalar_prefetch=3,
        grid=(rows_padded // tm,),
        in_specs=[
            pl.BlockSpec((tm * SUBLANES, LANES), lambda i, te, nv, nu: (jnp.minimum(i, nu[0]), 0)),
            pl.BlockSpec((1, d, f2), lambda i, te, nv, nu: (te[i], 0, 0)),
            pl.BlockSpec((1, 1, f2), lambda i, te, nv, nu: (te[i], 0, 0)),
            pl.BlockSpec((1, f, d), lambda i, te, nv, nu: (te[i], 0, 0)),
            pl.BlockSpec((1, 1, d), lambda i, te, nv, nu: (te[i], 0, 0)),
        ],
        out_specs=pl.BlockSpec((tm * SUBLANES, LANES), lambda i, te, nv, nu: (jnp.minimum(i, nu[0]), 0)),
        scratch_shapes=[pltpu.VMEM((d, f2), BF16), pltpu.VMEM((f, d), BF16), pltpu.VMEM((tm, d), F32)],
    )
    return pl.pallas_call(
        functools.partial(_expert_kernel, chunk=512),
        grid_spec=grid_spec,
        out_shape=jax.ShapeDtypeStruct(xs.shape, F32),
        compiler_params=_cparams(("arbitrary",)),
        name="moe_experts",
    )(tile_expert, tile_valid, tiles_used, xs, w_gate_up, b_gate_up.reshape(e, 1, f2), w_down, b_down.reshape(e, 1, d))


def _combine_kernel(pos_ref, y_ref, x2_ref, mw_ref, o_ref, ybuf, sems):
    nt = x2_ref.shape[0]
    half = nt // 2

    def copy(t, kk, hf):
        row = pl.multiple_of(pos_ref[0, kk * nt + t], SUBLANES)
        dst = ybuf.at[kk, pl.ds(pl.multiple_of(t * SUBLANES, SUBLANES), SUBLANES), :]
        return pltpu.make_async_copy(y_ref.at[pl.ds(row, SUBLANES), :], dst, sems.at[hf])

    def issue_half(hf):
        def issue(t, carry):
            for kk in range(TOP_K):
                copy(t, kk, hf).start(priority=kk % 2)
            return carry

        lax.fori_loop(hf * half, (hf + 1) * half, issue, 0, unroll=4)

    def drain_half(hf):
        for kk in range(TOP_K):
            rows = pl.ds(hf * half * SUBLANES, half * SUBLANES)
            pltpu.make_async_copy(y_ref.at[rows, :], ybuf.at[kk, rows, :], sems.at[hf]).wait()

    def combine_half(hf):
        rows = slice(hf * half, (hf + 1) * half)
        mw = mw_ref[rows, :]
        for cc in range(SUBLANES):
            cs = slice(cc * LANES, (cc + 1) * LANES)
            acc = x2_ref[rows, cs]
            for kk in range(TOP_K):
                ys = ybuf.at[kk][pl.ds(hf * half * SUBLANES + cc, half, stride=SUBLANES), :]
                acc = acc + mw[:, kk:kk + 1] * ys
            o_ref[rows, cs] = acc

    issue_half(0)
    issue_half(1)
    drain_half(0)
    combine_half(0)
    drain_half(1)
    combine_half(1)


def _combine(pos, y, x2, mw):
    n, d = x2.shape
    nt = pos.shape[-1] // TOP_K
    return pl.pallas_call(
        _combine_kernel,
        grid=(n // nt,),
        in_specs=[pl.BlockSpec((None, 1, nt * TOP_K), lambda i: (i, 0, 0), memory_space=pltpu.SMEM),
                  pl.BlockSpec(memory_space=pl.ANY),
                  pl.BlockSpec((nt, d), lambda i: (i, 0)),
                  pl.BlockSpec((nt, LANES), lambda i: (i, 0))],
        out_specs=pl.BlockSpec((nt, d), lambda i: (i, 0)),
        out_shape=jax.ShapeDtypeStruct((n, d), F32),
        scratch_shapes=[pltpu.VMEM((TOP_K, nt * SUBLANES, LANES), F32), pltpu.SemaphoreType.DMA((2,))],
        compiler_params=_cparams(("arbitrary",)),
        name="moe_combine",
    )(pos, y, x2, mw)


def _layer(x, mem, mix_norm, w_in, b_forget, fox_q_gain, fox_k_gain, fox_out_gain, gdn_conv_w, gdn_a_log,
           gdn_dt_bias, gdn_out_gain, w_out, xattn_norm, mem_norm, w_xq, w_xkv, xq_gain, xk_gain, w_xo,
           moe_norm, w_router, b_router, w_gate_up, b_gate_up, w_down, b_down):
    batch, seq, d = x.shape
    n = batch * seq
    mem_len = mem.shape[1]
    x2d = x.reshape(n, d)

    o_ff = 3 * FOX_WIDTH
    o_gq = o_ff + FOX_HEADS
    o_ga = o_gq + 3 * GDN_WIDTH
    o_gb = o_ga + GDN_HEADS
    o_gz = o_gb + GDN_HEADS
    w_main = jnp.concatenate([w_in[:, :o_ff], w_in[:, o_gq:o_ga], w_in[:, o_gz:]], axis=1).astype(BF16)
    w_small = jnp.concatenate([w_in[:, o_ff:o_gq], w_in[:, o_ga:o_gz],
                               jnp.zeros((d, LANES - SM_END), F32)], axis=1).astype(BF16)
    prow = jnp.zeros((SUBLANES, LANES), F32)
    prow = prow.at[0, SM_F0:SM_G0].set(b_forget).at[0, SM_G0:SM_B0].set(gdn_dt_bias)
    prow = prow.at[1, SM_G0:SM_B0].set(gdn_a_log)
    pcol = prow[0:2].T
    head_id = jnp.arange(FOX_WIDTH) // FOX_HEAD_DIM
    bd = (head_id[:, None] == head_id[None, :]).astype(BF16)
    row1 = lambda v: v.reshape(1, -1)

    tm = min(512, seq)
    fq, fk, fv, gqkv, gz, sm, smt = _inproj(
        x2d, row1(mix_norm), w_main, w_small, w_small.T, bd,
        row1(jnp.tile(fox_q_gain, FOX_HEADS)), row1(jnp.tile(fox_k_gain, FOX_HEADS)), prow, pcol,
        seq=seq, tm=tm)

    cum = smt[SM_F0:SM_G0].reshape(FOX_HEADS // 2, 2, batch, seq).transpose(2, 0, 1, 3)
    b3 = lambda a: a.reshape(batch, seq, a.shape[-1])
    o_fox = _fox_attention(b3(fq), b3(fk), b3(fv), cum, fox_out_gain.reshape(FOX_HEADS // 2, 1, LANES),
                           batch=batch, seq=seq, tq=min(2048, seq), blk=min(512, seq))

    u, w, qd, qk, kdt, egl = _gdn_prep(gqkv, gdn_conv_w.T, sm, smt, batch=batch, seq=seq)
    o_gdn = _gdn_scan(u, w, qd, qk, kdt, egl, b3(gz), row1(gdn_out_gain), batch=batch, seq=seq)

    kx, vx = _memkv(mem.reshape(batch * mem_len, d), row1(mem_norm), w_xkv.astype(BF16), row1(xk_gain),
                    batch=batch, mem_len=mem_len)
    w_r = jnp.concatenate([w_router, jnp.zeros((d, LANES - N_EXPERTS), F32)], axis=1)
    w_r = jnp.stack(_split_bf16(w_r))
    b_r = jnp.concatenate([b_router, jnp.full((LANES - N_EXPERTS,), -jnp.inf, F32)]).reshape(1, LANES)
    x2, h3, mi, mw, cnt = _mid(
        x2d, o_fox.reshape(n, FOX_WIDTH), o_gdn.reshape(n, GDN_WIDTH), w_out.astype(BF16), row1(xattn_norm),
        w_xq.astype(BF16), row1(xq_gain), kx, vx, w_xo.astype(BF16), row1(moe_norm), w_r, b_r,
        seq=seq, mem_len=mem_len, tm=min(ROUTE_TOKENS, seq))

    counts = cnt[0, :N_EXPERTS]
    padded = ((counts + MOE_TM - 1) // MOE_TM) * MOE_TM
    ends = jnp.cumsum(padded)
    starts = ends - padded
    rows_padded = n * TOP_K + N_EXPERTS * MOE_TM
    n_tiles = rows_padded // MOE_TM
    tile_start = jnp.arange(n_tiles, dtype=jnp.int32) * MOE_TM
    tile_expert = jnp.zeros((n_tiles,), jnp.int32)
    tile_limit = jnp.zeros((n_tiles,), jnp.int32)
    last_used = jnp.zeros((), jnp.int32)
    experts = mi[:, :TOP_K, :]
    base = jnp.zeros_like(experts)
    for e in range(N_EXPERTS):
        inside = (tile_start >= starts[e]) & (tile_start < ends[e])
        tile_expert = jnp.where(inside, e, tile_expert)
        tile_limit = jnp.where(inside, starts[e] + counts[e], tile_limit)
        last_used = jnp.where(counts[e] > 0, e, last_used)
        base = jnp.where(experts == e, starts[e], base)
    tile_valid = jnp.clip(tile_limit - tile_start, 0, MOE_TM).astype(jnp.int32)
    tile_expert = jnp.where(tile_valid > 0, tile_expert, last_used).astype(jnp.int32)

    pos = ((base + mi[:, TOP_K:, :]) * SUBLANES).astype(jnp.int32).reshape(mi.shape[0], 1, -1)
    tiles_used = (ends[N_EXPERTS - 1] // MOE_TM).astype(jnp.int32).reshape(1)

    xs = _dispatch(h3, pos, rows_padded=rows_padded)
    y = _experts(tile_expert, tile_valid, tiles_used, xs, w_gate_up, b_gate_up, w_down, b_down)
    out = _combine(pos, y, x2, mw)
    return out.reshape(batch, seq, d)


def kernel(x, mem, mix_norm, w_in, b_forget, fox_q_gain, fox_k_gain, fox_out_gain, gdn_conv_w, gdn_a_log,
           gdn_dt_bias, gdn_out_gain, w_out, xattn_norm, mem_norm, w_xq, w_xkv, xq_gain, xk_gain, w_xo,
           moe_norm, w_router, b_router, w_gate_up, b_gate_up, w_down, b_down):
    depth = mix_norm.shape[0]
    for l in range(depth):
        x = _layer(x, mem, mix_norm[l], w_in[l], b_forget[l], fox_q_gain[l], fox_k_gain[l], fox_out_gain[l],
                   gdn_conv_w[l], gdn_a_log[l], gdn_dt_bias[l], gdn_out_gain[l], w_out[l], xattn_norm[l],
                   mem_norm[l], w_xq[l], w_xkv[l], xq_gain[l], xk_gain[l], w_xo[l], moe_norm[l], w_router[l],
                   b_router[l], w_gate_up[l], b_gate_up[l], w_down[l], b_down[l])
    return x
```

```python
import functools

import jax
import jax.numpy as jnp
from jax import lax
from jax.experimental import pallas as pl
from jax.experimental.pallas import tpu as pltpu

F32 = jnp.float32
BF16 = jnp.bfloat16

EPS = 1e-6
FOX_HEADS, FOX_HEAD_DIM = 8, 64
GDN_HEADS, GDN_HEAD_DIM = 4, 128
FOX_WIDTH = FOX_HEADS * FOX_HEAD_DIM
GDN_WIDTH = GDN_HEADS * GDN_HEAD_DIM
CONV_WIDTH = 4
CHUNK = 64
XA_HEADS = 4
N_EXPERTS = 32
TOP_K = 4
SWIGLU_LIMIT = 7.0
SWIGLU_ALPHA = 1.702
LOG2E = 1.4426950408889634

LANES = 128
SUBLANES = 8
VMEM_LIMIT = 52 * 1024 * 1024

SM_F0, SM_G0, SM_B0, SM_END = 0, 8, 12, 16

NT_DIMS = (((1,), (1,)), ((), ()))


def _cparams(sem):
    return pltpu.CompilerParams(dimension_semantics=sem, vmem_limit_bytes=VMEM_LIMIT)


def _rms(x, gain):
    return x * lax.rsqrt(jnp.mean(x * x, axis=-1, keepdims=True) + EPS) * gain


def _split_bf16(x):
    hi = x.astype(BF16)
    return hi, (x - hi.astype(F32)).astype(BF16)


def _dot_mask_left(mask, x):
    hi, lo = _split_bf16(x)
    mb = mask.astype(BF16)
    return jnp.dot(mb, hi, preferred_element_type=F32) + jnp.dot(mb, lo, preferred_element_type=F32)


def _dot_mask_right(x, mask):
    hi, lo = _split_bf16(x)
    mb = mask.astype(BF16)
    return jnp.dot(hi, mb, preferred_element_type=F32) + jnp.dot(lo, mb, preferred_element_type=F32)


def _softplus(t):
    return jnp.maximum(t, 0.0) + jnp.log1p(jnp.exp(-jnp.abs(t)))


def _sigmoid(t):
    return 1.0 / (1.0 + jnp.exp(-t))


def _small_act(v, bias, alog, idx):
    t = v + bias
    tail = jnp.log1p(jnp.exp(-jnp.abs(t)))
    log_f = jnp.minimum(t, 0.0) - tail
    g = -jnp.exp(alog) * (jnp.maximum(t, 0.0) + tail)
    beta = _sigmoid(v)
    return jnp.where(idx < SM_G0, log_f, jnp.where(idx < SM_B0, g, jnp.where(idx < SM_END, beta, 0.0)))


def _inproj_kernel(x_ref, gain_ref, w_ref, ws_ref, wst_ref, bd_ref, qg_ref, kg_ref, prow_ref, pcol_ref,
                   fq_ref, fk_ref, fv_ref, gqkv_ref, gz_ref, sm_ref, smt_ref, carry_r, carry_c,
                   *, tm, tiles_per_seq):
    i = pl.program_id(0)
    hb = _rms(x_ref[...], gain_ref[...]).astype(BF16)

    def proj(lo, hi):
        return jnp.dot(hb, w_ref[:, lo:hi], preferred_element_type=F32)

    def headnorm(p, g):
        ss = jnp.dot((p * p).astype(BF16), bd_ref[...], preferred_element_type=F32)
        return p * lax.rsqrt(ss * (1.0 / FOX_HEAD_DIM) + EPS) * g

    w0 = FOX_WIDTH
    fq_ref[...] = (headnorm(proj(0, w0), qg_ref[...]) * (FOX_HEAD_DIM ** -0.5 * LOG2E)).astype(BF16)
    fk_ref[...] = headnorm(proj(w0, 2 * w0), kg_ref[...]).astype(BF16)
    fv_ref[...] = proj(2 * w0, 3 * w0).astype(BF16)
    g0 = 3 * w0
    gqkv_ref[...] = proj(g0, g0 + 3 * GDN_WIDTH)
    gz_ref[...] = proj(g0 + 3 * GDN_WIDTH, g0 + 4 * GDN_WIDTH)

    @pl.when(i % tiles_per_seq == 0)
    def _():
        carry_r[...] = jnp.zeros_like(carry_r)
        carry_c[...] = jnp.zeros_like(carry_c)

    sm = jnp.dot(hb, ws_ref[...], preferred_element_type=F32)
    smt = lax.dot_general(wst_ref[...], hb, NT_DIMS, preferred_element_type=F32)
    lane = lax.broadcasted_iota(jnp.int32, (1, LANES), 1)
    srow = lax.broadcasted_iota(jnp.int32, (LANES, 1), 0)
    vals = _small_act(sm, prow_ref[0:1, :], prow_ref[1:2, :], lane)
    vals_t = _small_act(smt, pcol_ref[:, 0:1], pcol_ref[:, 1:2], srow)

    r = lax.broadcasted_iota(jnp.int32, (tm, tm), 0)
    c = lax.broadcasted_iota(jnp.int32, (tm, tm), 1)
    cum = _dot_mask_left(c <= r, vals) + carry_r[...]
    cum_t = _dot_mask_right(vals_t, r <= c) + carry_c[...]
    carry_r[...] = cum[tm - 1:tm, :]
    carry_c[...] = cum_t[:, tm - 1:tm]
    sm_ref[...] = jnp.where(lane < SM_G0, cum, vals)
    smt_ref[...] = jnp.where(srow < SM_G0, cum_t, vals_t)


def _inproj(x2d, gain, w_main, w_small, w_small_t, bd, qg, kg, prow, pcol, *, seq, tm):
    n, d = x2d.shape
    wm = w_main.shape[1]
    full = lambda shape: pl.BlockSpec(shape, lambda i: (0,) * len(shape))
    rows = lambda width: pl.BlockSpec((tm, width), lambda i: (i, 0))
    out_shape = (
        jax.ShapeDtypeStruct((n, FOX_WIDTH), BF16),
        jax.ShapeDtypeStruct((n, FOX_WIDTH), BF16),
        jax.ShapeDtypeStruct((n, FOX_WIDTH), BF16),
        jax.ShapeDtypeStruct((n, 3 * GDN_WIDTH), F32),
        jax.ShapeDtypeStruct((n, GDN_WIDTH), F32),
        jax.ShapeDtypeStruct((n, LANES), F32),
        jax.ShapeDtypeStruct((LANES, n), F32),
    )
    return pl.pallas_call(
        functools.partial(_inproj_kernel, tm=tm, tiles_per_seq=seq // tm),
        grid=(n // tm,),
        in_specs=[rows(d), full((1, d)), full((d, wm)), full((d, LANES)), full((LANES, d)),
                  full((FOX_WIDTH, FOX_WIDTH)), full((1, FOX_WIDTH)), full((1, FOX_WIDTH)),
                  full((SUBLANES, LANES)), full((LANES, 2))],
        out_specs=(rows(FOX_WIDTH), rows(FOX_WIDTH), rows(FOX_WIDTH), rows(3 * GDN_WIDTH), rows(GDN_WIDTH),
                   rows(LANES), pl.BlockSpec((LANES, tm), lambda i: (0, i))),
        out_shape=out_shape,
        scratch_shapes=[pltpu.VMEM((1, LANES), F32), pltpu.VMEM((LANES, 1), F32)],
        compiler_params=_cparams(("arbitrary",)),
        name="inproj",
    )(x2d, gain, w_main, w_small, w_small_t, bd, qg, kg, prow, pcol)


def _fox_kernel(q_ref, k_ref, v_ref, cum_ref, gain_ref, o_ref, m0_ref, m1_ref, acc0_ref, acc1_ref, *, tq, blk):
    qi = pl.program_id(2)
    lane = lax.broadcasted_iota(jnp.int32, (1, LANES), 1)
    lo = lane < FOX_HEAD_DIM
    q = q_ref[...]
    zero = jnp.zeros_like(q)
    q_heads = (jnp.where(lo, q, zero), jnp.where(lo, zero, q))
    q0 = pl.multiple_of(qi * tq, tq)
    c_ref = [cum_ref[hh:hh + 1, pl.ds(q0, LANES)][:, 0:1] for hh in range(2)]
    head_lanes = (lo, jnp.logical_not(lo))
    sum_lane = (FOX_HEAD_DIM, 0)
    ones_col = [jnp.where(lane == sum_lane[hh], 1.0, 0.0).astype(BF16) for hh in range(2)]
    m_refs = (m0_ref, m1_ref)
    acc_refs = (acc0_ref, acc1_ref)
    for hh in range(2):
        m_refs[hh][...] = jnp.full_like(m_refs[hh], -1e30)
        acc_refs[hh][...] = jnp.zeros_like(acc_refs[hh])

    def step(k0, r0=None):
        rows = slice(0 if r0 is None else r0, tq)
        nr = rows.stop - rows.start
        kb = k_ref[pl.ds(k0, blk), :]
        vb = v_ref[pl.ds(k0, blk), :]
        scores = [lax.dot_general(q_heads[hh][rows], kb, NT_DIMS, preferred_element_type=F32) for hh in range(2)]
        for hh in range(2):
            s = scores[hh] + (c_ref[hh] - cum_ref[hh:hh + 1, pl.ds(k0, blk)]) * LOG2E
            if r0 is not None:
                r = lax.broadcasted_iota(jnp.int32, (nr, blk), 0)
                c = lax.broadcasted_iota(jnp.int32, (nr, blk), 1)
                s = jnp.where(c <= r, s, -jnp.inf)
            m_old = m_refs[hh][rows, :]
            m_new = jnp.maximum(m_old, jnp.max(s, axis=-1, keepdims=True))
            alpha = jnp.exp2(m_old - m_new)
            p = jnp.exp2(s - jnp.concatenate([m_new] * (blk // LANES), axis=1))
            v_h = jnp.where(head_lanes[hh], vb, ones_col[hh])
            pv = jnp.dot(p.astype(BF16), v_h, preferred_element_type=F32)
            acc_refs[hh][rows, :] = acc_refs[hh][rows, :] * alpha + pv
            m_refs[hh][rows, :] = m_new

    def body(kp, carry):
        step(pl.multiple_of(kp * (2 * blk), blk))
        step(pl.multiple_of(kp * (2 * blk) + blk, blk))
        return carry

    n_full = qi * (tq // blk)
    lax.fori_loop(0, n_full // 2, body, 0)

    @pl.when(n_full % 2 == 1)
    def _():
        step(pl.multiple_of((n_full - 1) * blk, blk))

    for j in range(tq // blk):
        step(pl.multiple_of(q0 + j * blk, blk), r0=j * blk)

    acc0, acc1 = acc0_ref[...], acc1_ref[...]
    l0 = acc0[:, sum_lane[0]:sum_lane[0] + 1]
    l1 = acc1[:, sum_lane[1]:sum_lane[1] + 1]
    o = jnp.where(lo, acc0 * (1.0 / l0), acc1 * (1.0 / l1))
    o2 = o * o
    ss_lo = jnp.sum(jnp.where(lo, o2, 0.0), axis=-1, keepdims=True)
    ss_hi = jnp.sum(jnp.where(lo, 0.0, o2), axis=-1, keepdims=True)
    ms = jnp.where(lo, ss_lo, ss_hi) * (1.0 / FOX_HEAD_DIM)
    o_ref[...] = (o * lax.rsqrt(ms + EPS) * gain_ref[...]).astype(o_ref.dtype)


def _fox_attention(fq, fk, fv, cum, gain_pairs, *, batch, seq, tq, blk):
    npairs = FOX_HEADS // 2
    return pl.pallas_call(
        functools.partial(_fox_kernel, tq=tq, blk=blk),
        grid=(batch, npairs, seq // tq),
        in_specs=[
            pl.BlockSpec((None, tq, LANES), lambda b, j, i: (b, i, j)),
            pl.BlockSpec((None, seq, LANES), lambda b, j, i: (b, 0, j)),
            pl.BlockSpec((None, seq, LANES), lambda b, j, i: (b, 0, j)),
            pl.BlockSpec((None, None, 2, seq), lambda b, j, i: (b, j, 0, 0)),
            pl.BlockSpec((None, 1, LANES), lambda b, j, i: (j, 0, 0)),
        ],
        out_specs=pl.BlockSpec((None, tq, LANES), lambda b, j, i: (b, i, j)),
        out_shape=jax.ShapeDtypeStruct((batch, seq, FOX_WIDTH), BF16),
        scratch_shapes=[pltpu.VMEM((tq, LANES), F32) for _ in range(4)],
        compiler_params=_cparams(("parallel", "parallel", "arbitrary")),
        name="fox_attention",
    )(fq, fk, fv, cum, gain_pairs)


GDN_BLK = 4 * CHUNK
QK_BLK = 2 * CHUNK


def _gdn_prep_kernel(x_ref, halo_ref, cw_ref, sm_ref, smt_ref,
                     u_ref, w_ref, qd_ref, qk_ref, kdt_ref, egl_ref, xpad_ref):
    i = pl.program_id(1)
    nb = GDN_BLK
    halo = halo_ref[...]
    xpad_ref[0:SUBLANES, :] = jnp.where(i > 0, halo, jnp.zeros_like(halo))
    xpad_ref[SUBLANES:, :] = x_ref[...]
    y = None
    for j in range(CONV_WIDTH):
        start = SUBLANES - (CONV_WIDTH - 1) + j
        term = cw_ref[j:j + 1, :] * xpad_ref[start:start + nb, :]
        y = term if y is None else y + term
    y = y * _sigmoid(y)

    r = lax.broadcasted_iota(jnp.int32, (nb, nb), 0)
    c = lax.broadcasted_iota(jnp.int32, (nb, nb), 1)
    chunk_shift = CHUNK.bit_length() - 1
    same = jnp.right_shift(r, chunk_shift) == jnp.right_shift(c, chunk_shift)
    incl = same & (c <= r)
    strict = same & (c < r)
    sm = sm_ref[...]
    g_cum = _dot_mask_left(incl, sm)
    g_tot = _dot_mask_left(same, sm)
    g_cum_t = _dot_mask_right(smt_ref[...], same & (r <= c))
    eye = (r == c).astype(F32)

    powers, t_invs, rhs = [], [], []
    for h in range(GDN_HEADS):
        sl = slice(h * GDN_HEAD_DIM, (h + 1) * GDN_HEAD_DIM)
        q = y[:, h * GDN_HEAD_DIM:(h + 1) * GDN_HEAD_DIM]
        k = y[:, GDN_WIDTH + h * GDN_HEAD_DIM:GDN_WIDTH + (h + 1) * GDN_HEAD_DIM]
        v = y[:, 2 * GDN_WIDTH + h * GDN_HEAD_DIM:2 * GDN_WIDTH + (h + 1) * GDN_HEAD_DIM]
        qn = q * lax.rsqrt(jnp.sum(q * q, axis=-1, keepdims=True) + EPS) * GDN_HEAD_DIM ** -0.5
        kn = k * lax.rsqrt(jnp.sum(k * k, axis=-1, keepdims=True) + EPS)
        gc = g_cum[:, SM_G0 + h:SM_G0 + h + 1]
        gl = g_tot[:, SM_G0 + h:SM_G0 + h + 1]
        gr = g_cum_t[SM_G0 + h:SM_G0 + h + 1, :]
        beta = sm[:, SM_B0 + h:SM_B0 + h + 1]
        decay = jnp.where(incl, jnp.exp(jnp.where(incl, gc - gr, 0.0)), 0.0)
        qb, kb = qn.astype(BF16), kn.astype(BF16)
        kk = lax.dot_general(kb, kb, NT_DIMS, preferred_element_type=F32)
        a = jnp.where(strict, beta * kk * decay, 0.0)
        eg = jnp.exp(gc)
        powers.append(a)
        t_invs.append(eye - a)
        rhs.append(jnp.concatenate([(v * beta).astype(BF16), (kn * (beta * eg)).astype(BF16)], axis=1))
        qk = lax.dot_general(qb, kb, NT_DIMS, preferred_element_type=F32)
        qk = jnp.where(incl, qk * decay, 0.0).astype(BF16)
        qk_ref[:, h * QK_BLK:(h + 1) * QK_BLK] = jnp.concatenate(
            [qk[j * QK_BLK:(j + 1) * QK_BLK, j * QK_BLK:(j + 1) * QK_BLK] for j in range(nb // QK_BLK)], axis=0)
        qd_ref[:, sl] = (qn * eg).astype(BF16)
        kdt_ref[sl, :] = (kn * jnp.exp(gl - gc)).T.astype(BF16)
        egl_ref[:, sl] = jnp.broadcast_to(jnp.exp(gl), (nb, GDN_HEAD_DIM))

    for _ in range(5):
        for h in range(GDN_HEADS):
            pb = powers[h].astype(BF16)
            powers[h] = jnp.dot(pb, pb, preferred_element_type=F32)
        for h in range(GDN_HEADS):
            t_invs[h] = t_invs[h] + jnp.dot(t_invs[h].astype(BF16), powers[h].astype(BF16),
                                            preferred_element_type=F32)
    for h in range(GDN_HEADS):
        sl = slice(h * GDN_HEAD_DIM, (h + 1) * GDN_HEAD_DIM)
        uw = jnp.dot(t_invs[h].astype(BF16), rhs[h], preferred_element_type=F32)
        u_ref[:, sl] = uw[:, :GDN_HEAD_DIM]
        w_ref[:, sl] = uw[:, GDN_HEAD_DIM:].astype(BF16)


def _gdn_prep(gqkv, conv_w_t, sm, smt, *, batch, seq):
    nb = GDN_BLK
    bps = seq // nb
    hps = nb // SUBLANES
    width = 3 * GDN_WIDTH
    row = lambda w: pl.BlockSpec((None, nb, w), lambda b, i: (b, i, 0))
    return pl.pallas_call(
        _gdn_prep_kernel,
        grid=(batch, bps),
        in_specs=[
            pl.BlockSpec((nb, width), lambda b, i: (b * bps + i, 0)),
            pl.BlockSpec((SUBLANES, width), lambda b, i: (jnp.maximum((b * bps + i) * hps - 1, 0), 0)),
            pl.BlockSpec((CONV_WIDTH, width), lambda b, i: (0, 0)),
            pl.BlockSpec((nb, LANES), lambda b, i: (b * bps + i, 0)),
            pl.BlockSpec((LANES, nb), lambda b, i: (0, b * bps + i)),
        ],
        out_specs=(row(GDN_WIDTH), row(GDN_WIDTH), row(GDN_WIDTH), row(GDN_HEADS * QK_BLK),
                   pl.BlockSpec((None, GDN_WIDTH, nb), lambda b, i: (b, 0, i)), row(GDN_WIDTH)),
        out_shape=(
            jax.ShapeDtypeStruct((batch, seq, GDN_WIDTH), F32),
            jax.ShapeDtypeStruct((batch, seq, GDN_WIDTH), BF16),
            jax.ShapeDtypeStruct((batch, seq, GDN_WIDTH), BF16),
            jax.ShapeDtypeStruct((batch, seq, GDN_HEADS * QK_BLK), BF16),
            jax.ShapeDtypeStruct((batch, GDN_WIDTH, seq), BF16),
            jax.ShapeDtypeStruct((batch, seq, GDN_WIDTH), F32),
        ),
        scratch_shapes=[pltpu.VMEM((nb + SUBLANES, width), F32)],
        compiler_params=_cparams(("parallel", "parallel")),
        name="gdn_prep",
    )(gqkv, gqkv, conv_w_t, sm, smt)


def _gdn_scan_kernel(u_ref, w_ref, qd_ref, qk_ref, kdt_ref, egl_ref, z_ref, gain_ref, o_ref, s_ref, vz_ref,
                     *, batch):
    pb = QK_BLK

    @pl.when(pl.program_id(0) == 0)
    def _():
        s_ref[...] = jnp.zeros_like(s_ref)

    vz_ref[...] = jnp.zeros_like(vz_ref)
    for cidx in range(GDN_BLK // CHUNK):
        rows = slice(cidx * CHUNK, (cidx + 1) * CHUNK)
        blk_rows = slice(cidx * CHUNK // pb * pb, (cidx * CHUNK // pb + 1) * pb)
        for b in range(batch):
            for h in range(GDN_HEADS):
                bh = b * GDN_HEADS + h
                sl = slice(h * GDN_HEAD_DIM, (h + 1) * GDN_HEAD_DIM)
                s_old = s_ref[bh]
                lhs1 = jnp.concatenate([w_ref[b, rows, sl], qd_ref[b, rows, sl]], axis=0)
                r1 = jnp.dot(lhs1, s_old.astype(BF16), preferred_element_type=F32)
                v_new = u_ref[b, rows, sl] - r1[:CHUNK]
                vz_ref[bh, rows, :] = v_new.astype(BF16)
                lhs2 = jnp.concatenate([qk_ref[b, rows, h * pb:(h + 1) * pb], kdt_ref[b, sl, blk_rows]], axis=0)
                r2 = jnp.dot(lhs2, vz_ref[bh, blk_rows, :], preferred_element_type=F32)
                vz_ref[bh, rows, :] = jnp.zeros((CHUNK, GDN_HEAD_DIM), BF16)
                last = egl_ref[b, (cidx + 1) * CHUNK - 1:(cidx + 1) * CHUNK, sl]
                s_ref[bh] = s_old * last + r2[CHUNK:]
                o = r1[CHUNK:] + r2[:CHUNK]
                z = z_ref[b, rows, sl]
                o_ref[b, rows, sl] = (_rms(o, gain_ref[...]) * (z * _sigmoid(z))).astype(o_ref.dtype)


def _gdn_scan(u, w, qd, qk, kdt, egl, z, gain, *, batch, seq):
    nb = GDN_BLK
    row = lambda width: pl.BlockSpec((batch, nb, width), lambda i: (0, i, 0))
    return pl.pallas_call(
        functools.partial(_gdn_scan_kernel, batch=batch),
        grid=(seq // nb,),
        in_specs=[row(GDN_WIDTH), row(GDN_WIDTH), row(GDN_WIDTH), row(GDN_HEADS * QK_BLK),
                  pl.BlockSpec((batch, GDN_WIDTH, nb), lambda i: (0, 0, i)), row(GDN_WIDTH), row(GDN_WIDTH),
                  pl.BlockSpec((1, GDN_HEAD_DIM), lambda i: (0, 0))],
        out_specs=row(GDN_WIDTH),
        out_shape=jax.ShapeDtypeStruct((batch, seq, GDN_WIDTH), BF16),
        scratch_shapes=[pltpu.VMEM((batch * GDN_HEADS, GDN_HEAD_DIM, GDN_HEAD_DIM), F32),
                        pltpu.VMEM((batch * GDN_HEADS, nb, GDN_HEAD_DIM), BF16)],
        compiler_params=_cparams(("arbitrary",)),
        name="gdn_scan",
    )(u, w, qd, qk, kdt, egl, z, gain)


def _memkv_kernel(m_ref, gain_ref, w_ref, kg_ref, k_ref, v_ref):
    d = m_ref.shape[-1]
    hd = d // XA_HEADS
    mb = _rms(m_ref[...], gain_ref[...]).astype(BF16)
    kv = jnp.dot(mb, w_ref[...], preferred_element_type=F32)
    for h in range(XA_HEADS):
        sl = slice(h * hd, (h + 1) * hd)
        k_ref[:, sl] = _rms(kv[:, sl], kg_ref[...]).astype(BF16)
    v_ref[...] = kv[:, d:].astype(BF16)


def _memkv(mem2d, gain, w_xkv, xk_gain, *, batch, mem_len):
    d = mem2d.shape[-1]
    full = lambda shape: pl.BlockSpec(shape, lambda b: (0,) * len(shape))
    row = pl.BlockSpec((mem_len, d), lambda b: (b, 0))
    return pl.pallas_call(
        _memkv_kernel,
        grid=(batch,),
        in_specs=[row, full((1, d)), full((d, 2 * d)), full((1, d // XA_HEADS))],
        out_specs=(row, row),
        out_shape=(jax.ShapeDtypeStruct(mem2d.shape, BF16), jax.ShapeDtypeStruct(mem2d.shape, BF16)),
        compiler_params=_cparams(("parallel",)),
        name="mem_kv",
    )(mem2d, gain, w_xkv, xk_gain)


def _mid_kernel(x_ref, of_ref, og_ref, wo_ref, xg_ref, wq_ref, qg_ref, k_ref, v_ref, wxo_ref, mg_ref,
                wr_ref, br_ref,
                x2_ref, h3_ref, mi_ref, mw_ref, cnt_ref, carry_ref, *, tm):
    i = pl.program_id(0)
    d = x_ref.shape[-1]
    hd = d // XA_HEADS
    x1 = (x_ref[...]
          + jnp.dot(of_ref[...], wo_ref[0:FOX_WIDTH, :], preferred_element_type=F32)
          + jnp.dot(og_ref[...], wo_ref[FOX_WIDTH:, :], preferred_element_type=F32))
    h2 = _rms(x1, xg_ref[...]).astype(BF16)
    q = jnp.dot(h2, wq_ref[...], preferred_element_type=F32)
    heads = []
    for h in range(XA_HEADS):
        sl = slice(h * hd, (h + 1) * hd)
        qn = (_rms(q[:, sl], qg_ref[...]) * hd ** -0.5).astype(BF16)
        s = lax.dot_general(qn, k_ref[:, sl], NT_DIMS, preferred_element_type=F32)
        p = jnp.exp(s - jnp.max(s, axis=-1, keepdims=True))
        p = p * (1.0 / jnp.sum(p, axis=-1, keepdims=True))
        heads.append(jnp.dot(p.astype(BF16), v_ref[:, sl], preferred_element_type=F32).astype(BF16))
    x2 = x1 + jnp.dot(jnp.concatenate(heads, axis=-1), wxo_ref[...], preferred_element_type=F32)
    x2_ref[...] = x2
    h3 = _rms(x2, mg_ref[...])
    for cc in range(SUBLANES):
        h3_ref[pl.ds(cc, tm, stride=SUBLANES), :] = h3[:, cc * LANES:(cc + 1) * LANES]
    h_hi, h_lo = _split_bf16(h3)
    logits = (jnp.dot(h_hi, wr_ref[0], preferred_element_type=F32)
              + jnp.dot(h_lo, wr_ref[0], preferred_element_type=F32)
              + jnp.dot(h_hi, wr_ref[1], preferred_element_type=F32)) + br_ref[...]
    lane = lax.broadcasted_iota(jnp.int32, (tm, LANES), 1)
    work = logits
    vals, idxs = [], []
    onehot = jnp.zeros((tm, LANES), F32)
    for _ in range(TOP_K):
        mx = jnp.max(work, axis=-1, keepdims=True)
        idx = jnp.min(jnp.where(work == mx, lane, LANES), axis=-1, keepdims=True)
        sel = lane == idx
        onehot = jnp.where(sel, 1.0, onehot)
        work = jnp.where(sel, -jnp.inf, work)
        vals.append(mx)
        idxs.append(idx)
    es = [jnp.exp(v - vals[0]) for v in vals]
    inv_denom = 1.0 / (es[0] + es[1] + es[2] + es[3])

    @pl.when(i == 0)
    def _():
        carry_ref[...] = jnp.zeros_like(carry_ref)

    r = lax.broadcasted_iota(jnp.int32, (tm, tm), 0)
    c = lax.broadcasted_iota(jnp.int32, (tm, tm), 1)
    before = jnp.dot((c < r).astype(BF16), onehot.astype(BF16), preferred_element_type=F32) + carry_ref[...]
    mi = jnp.zeros((tm, LANES), F32)
    mw = jnp.zeros((tm, LANES), F32)
    for kk in range(TOP_K):
        rank = jnp.sum(jnp.where(lane == idxs[kk], before, 0.0), axis=-1, keepdims=True)
        mi = jnp.where(lane == kk, idxs[kk].astype(F32), mi)
        mi = jnp.where(lane == TOP_K + kk, rank, mi)
        mw = jnp.where(lane == kk, es[kk] * inv_denom, mw)
    mi_ref[...] = mi.T[0:2 * TOP_K, :].astype(jnp.int32)
    mw_ref[...] = mw
    total = carry_ref[...] + jnp.sum(onehot, axis=0, keepdims=True)
    carry_ref[...] = total
    cnt_ref[...] = jnp.broadcast_to(total, cnt_ref.shape).astype(jnp.int32)


def _mid(x2d, o_fox, o_gdn, w_out, xg, w_xq, xq_gain, kx, vx, w_xo, mg, w_r, b_r, *, seq, mem_len, tm):
    n, d = x2d.shape
    full = lambda shape: pl.BlockSpec(shape, lambda i: (0,) * len(shape))
    rows = lambda width: pl.BlockSpec((tm, width), lambda i: (i, 0))
    mem = pl.BlockSpec((mem_len, d), lambda i: (i // (seq // tm), 0))
    return pl.pallas_call(
        functools.partial(_mid_kernel, tm=tm),
        grid=(n // tm,),
        in_specs=[rows(d), rows(FOX_WIDTH), rows(GDN_WIDTH), full((d, d)), full((1, d)), full((d, d)),
                  full((1, d // XA_HEADS)), mem, mem, full((d, d)), full((1, d)), full((2, d, LANES)),
                  full((1, LANES))],
        out_specs=(rows(d), pl.BlockSpec((tm * d // LANES, LANES), lambda i: (i, 0)),
                   pl.BlockSpec((None, 2 * TOP_K, tm), lambda i: (i, 0, 0)), rows(LANES),
                   full((SUBLANES, LANES))),
        out_shape=(jax.ShapeDtypeStruct((n, d), F32), jax.ShapeDtypeStruct((n * d // LANES, LANES), F32),
                   jax.ShapeDtypeStruct((n // tm, 2 * TOP_K, tm), jnp.int32),
                   jax.ShapeDtypeStruct((n, LANES), F32),
                   jax.ShapeDtypeStruct((SUBLANES, LANES), jnp.int32)),
        scratch_shapes=[pltpu.VMEM((1, LANES), F32)],
        compiler_params=_cparams(("arbitrary",)),
        name="outproj_xattn_router",
    )(x2d, o_fox, o_gdn, w_out, xg, w_xq, xq_gain, kx, vx, w_xo, mg, w_r, b_r)


MOE_TM = 512
ROUTE_TOKENS = 512


def _rows_kernel(starts_ref, mi_ref, pos_ref):
    tm = mi_ref.shape[-1]
    experts = mi_ref[:, 0:TOP_K, :]
    base = jnp.zeros_like(experts)
    for e in range(N_EXPERTS):
        base = jnp.where(experts == e, starts_ref[e], base)
    rows = (base + mi_ref[:, TOP_K:2 * TOP_K, :]) * SUBLANES
    for kk in range(TOP_K):
        pos_ref[:, :, kk * tm:(kk + 1) * tm] = rows[:, kk:kk + 1, :]


def _rows(starts, mi):
    steps, _, tm = mi.shape
    grid_spec = pltpu.PrefetchScalarGridSpec(
        num_scalar_prefetch=1,
        grid=(1,),
        in_specs=[pl.BlockSpec(mi.shape, lambda i, st: (0, 0, 0))],
        out_specs=pl.BlockSpec((steps, 1, TOP_K * tm), lambda i, st: (0, 0, 0)),
    )
    return pl.pallas_call(
        _rows_kernel,
        grid_spec=grid_spec,
        out_shape=jax.ShapeDtypeStruct((steps, 1, TOP_K * tm), jnp.int32),
        compiler_params=_cparams(("arbitrary",)),
        name="moe_rows",
    )(starts, mi)


def _dispatch_kernel(pos_ref, src_ref, dst_ref, sem):
    nt = src_ref.shape[0] // SUBLANES

    def copy(t, kk):
        src = src_ref.at[pl.ds(pl.multiple_of(t * SUBLANES, SUBLANES), SUBLANES), :]
        row = pl.multiple_of(pos_ref[0, kk * nt + t], SUBLANES)
        return pltpu.make_async_copy(src, dst_ref.at[pl.ds(row, SUBLANES), :], sem)

    def issue(t, carry):
        for kk in range(TOP_K):
            copy(t, kk).start(priority=kk % 2)
        return carry

    lax.fori_loop(0, nt, issue, 0, unroll=4)

    for kk in range(TOP_K):
        pltpu.make_async_copy(src_ref, dst_ref.at[pl.ds(0, nt * SUBLANES), :], sem).wait()


def _dispatch(h3_tiles, pos, *, rows_padded):
    n = h3_tiles.shape[0] // SUBLANES
    nt = pos.shape[-1] // TOP_K
    return pl.pallas_call(
        _dispatch_kernel,
        grid=(n // nt,),
        in_specs=[pl.BlockSpec((None, 1, nt * TOP_K), lambda i: (i, 0, 0), memory_space=pltpu.SMEM),
                  pl.BlockSpec((nt * SUBLANES, LANES), lambda i: (i, 0))],
        out_specs=pl.BlockSpec(memory_space=pl.ANY),
        out_shape=jax.ShapeDtypeStruct((rows_padded * SUBLANES, LANES), F32),
        scratch_shapes=[pltpu.SemaphoreType.DMA],
        compiler_params=pltpu.CompilerParams(dimension_semantics=("arbitrary",), has_side_effects=True),
        name="moe_dispatch",
    )(pos, h3_tiles)


def _expert_kernel(te_ref, nv_ref, nu_ref, xs_ref, wgu_ref, bgu_ref, wd_ref, bd_ref, y_ref, wgu_bf, wd_bf, acc_ref,
                   *, chunk):
    i = pl.program_id(0)
    tm = xs_ref.shape[0] // SUBLANES
    d = wd_bf.shape[1]
    f = wd_bf.shape[0]
    nvalid = nv_ref[i]
    first = jnp.logical_or(i == 0, te_ref[i] != te_ref[jnp.maximum(i - 1, 0)])

    @pl.when(jnp.logical_and(first, nvalid > 0))
    def _():
        wgu_bf[...] = wgu_ref[0].astype(BF16)
        wd_bf[...] = wd_ref[0].astype(BF16)

    @pl.when(nvalid > 0)
    def _():
        row = lax.broadcasted_iota(jnp.int32, (tm, 1), 0)
        x = jnp.concatenate([xs_ref[pl.ds(cc, tm, stride=SUBLANES), :] for cc in range(SUBLANES)], axis=-1)
        x = jnp.where(row < nvalid, x, 0.0).astype(BF16)
        for j in range(f // chunk):
            cs = slice(j * chunk, (j + 1) * chunk)
            us = slice(f + j * chunk, f + (j + 1) * chunk)
            g = jnp.dot(x, wgu_bf[:, cs], preferred_element_type=F32) + bgu_ref[0, :, cs]
            u = jnp.dot(x, wgu_bf[:, us], preferred_element_type=F32) + bgu_ref[0, :, us]
            gate = jnp.minimum(g, SWIGLU_LIMIT)
            up = jnp.clip(u, -SWIGLU_LIMIT, SWIGLU_LIMIT)
            act = ((up + 1.0) * (gate * _sigmoid(SWIGLU_ALPHA * gate))).astype(BF16)
            part = jnp.dot(act, wd_bf[cs, :], preferred_element_type=F32)
            if j == 0:
                acc_ref[...] = part + bd_ref[0]
            else:
                acc_ref[...] += part
        for cc in range(SUBLANES):
            y_ref[pl.ds(cc, tm, stride=SUBLANES), :] = acc_ref[:, cc * LANES:(cc + 1) * LANES]

    @pl.when(nvalid <= 0)
    def _():
        y_ref[...] = jnp.zeros_like(y_ref)


def _experts(tile_expert, tile_valid, tiles_used, xs, w_gate_up, b_gate_up, w_down, b_down):
    rows_padded = xs.shape[0] // SUBLANES
    e, d, f2 = w_gate_up.shape
    f = f2 // 2
    tm = MOE_TM
    grid_spec = pltpu.PrefetchScalarGridSpec(
        num_scalar_prefetch=3,
        grid=(rows_padded // tm,),
        in_specs=[
            pl.BlockSpec((tm * SUBLANES, LANES), lambda i, te, nv, nu: (jnp.minimum(i, nu[0]), 0)),
            pl.BlockSpec((1, d, f2), lambda i, te, nv, nu: (te[i], 0, 0)),
            pl.BlockSpec((1, 1, f2), lambda i, te, nv, nu: (te[i], 0, 0)),
            pl.BlockSpec((1, f, d), lambda i, te, nv, nu: (te[i], 0, 0)),
            pl.BlockSpec((1, 1, d), lambda i, te, nv, nu: (te[i], 0, 0)),
        ],
        out_specs=pl.BlockSpec((tm * SUBLANES, LANES), lambda i, te, nv, nu: (jnp.minimum(i, nu[0]), 0)),
        scratch_shapes=[pltpu.VMEM((d, f2), BF16), pltpu.VMEM((f, d), BF16), pltpu.VMEM((tm, d), F32)],
    )
    return pl.pallas_call(
        functools.partial(_expert_kernel, chunk=512),
        grid_spec=grid_spec,
        out_shape=jax.ShapeDtypeStruct(xs.shape, F32),
        compiler_params=_cparams(("arbitrary",)),
        name="moe_experts",
    )(tile_expert, tile_valid, tiles_used, xs, w_gate_up, b_gate_up.reshape(e, 1, f2), w_down, b_down.reshape(e, 1, d))


def _combine_kernel(pos_ref, y_ref, x2_ref, mw_ref, o_ref, ybuf, sems):
    nt = x2_ref.shape[0]
    half = nt // 2

    def copy(t, kk, hf):
        row = pl.multiple_of(pos_ref[0, kk * nt + t], SUBLANES)
        dst = ybuf.at[kk, pl.ds(pl.multiple_of(t * SUBLANES, SUBLANES), SUBLANES), :]
        return pltpu.make_async_copy(y_ref.at[pl.ds(row, SUBLANES), :], dst, sems.at[hf])

    def issue_half(hf):
        def issue(t, carry):
            for kk in range(TOP_K):
                copy(t, kk, hf).start(priority=kk % 2)
            return carry

        lax.fori_loop(hf * half, (hf + 1) * half, issue, 0, unroll=4)

    def drain_half(hf):
        for kk in range(TOP_K):
            rows = pl.ds(hf * half * SUBLANES, half * SUBLANES)
            pltpu.make_async_copy(y_ref.at[rows, :], ybuf.at[kk, rows, :], sems.at[hf]).wait()

    def combine_half(hf):
        rows = slice(hf * half, (hf + 1) * half)
        mw = mw_ref[rows, :]
        for cc in range(SUBLANES):
            cs = slice(cc * LANES, (cc + 1) * LANES)
            acc = x2_ref[rows, cs]
            for kk in range(TOP_K):
                ys = ybuf.at[kk][pl.ds(hf * half * SUBLANES + cc, half, stride=SUBLANES), :]
                acc = acc + mw[:, kk:kk + 1] * ys
            o_ref[rows, cs] = acc

    issue_half(0)
    issue_half(1)
    drain_half(0)
    combine_half(0)
    drain_half(1)
    combine_half(1)


def _combine(pos, y, x2, mw):
    n, d = x2.shape
    nt = pos.shape[-1] // TOP_K
    return pl.pallas_call(
        _combine_kernel,
        grid=(n // nt,),
        in_specs=[pl.BlockSpec((None, 1, nt * TOP_K), lambda i: (i, 0, 0), memory_space=pltpu.SMEM),
                  pl.BlockSpec(memory_space=pl.ANY),
                  pl.BlockSpec((nt, d), lambda i: (i, 0)),
                  pl.BlockSpec((nt, LANES), lambda i: (i, 0))],
        out_specs=pl.BlockSpec((nt, d), lambda i: (i, 0)),
        out_shape=jax.ShapeDtypeStruct((n, d), F32),
        scratch_shapes=[pltpu.VMEM((TOP_K, nt * SUBLANES, LANES), F32), pltpu.SemaphoreType.DMA((2,))],
        compiler_params=_cparams(("arbitrary",)),
        name="moe_combine",
    )(pos, y, x2, mw)


def _layer(x, mem, mix_norm, w_in, b_forget, fox_q_gain, fox_k_gain, fox_out_gain, gdn_conv_w, gdn_a_log,
           gdn_dt_bias, gdn_out_gain, w_out, xattn_norm, mem_norm, w_xq, w_xkv, xq_gain, xk_gain, w_xo,
           moe_norm, w_router, b_router, w_gate_up, b_gate_up, w_down, b_down):
    batch, seq, d = x.shape
    n = batch * seq
    mem_len = mem.shape[1]
    x2d = x.reshape(n, d)

    o_ff = 3 * FOX_WIDTH
    o_gq = o_ff + FOX_HEADS
    o_ga = o_gq + 3 * GDN_WIDTH
    o_gb = o_ga + GDN_HEADS
    o_gz = o_gb + GDN_HEADS
    w_main = jnp.concatenate([w_in[:, :o_ff], w_in[:, o_gq:o_ga], w_in[:, o_gz:]], axis=1).astype(BF16)
    w_small = jnp.concatenate([w_in[:, o_ff:o_gq], w_in[:, o_ga:o_gz],
                               jnp.zeros((d, LANES - SM_END), F32)], axis=1).astype(BF16)
    prow = jnp.zeros((SUBLANES, LANES), F32)
    prow = prow.at[0, SM_F0:SM_G0].set(b_forget).at[0, SM_G0:SM_B0].set(gdn_dt_bias)
    prow = prow.at[1, SM_G0:SM_B0].set(gdn_a_log)
    pcol = prow[0:2].T
    head_id = jnp.arange(FOX_WIDTH) // FOX_HEAD_DIM
    bd = (head_id[:, None] == head_id[None, :]).astype(BF16)
    row1 = lambda v: v.reshape(1, -1)

    tm = min(512, seq)
    fq, fk, fv, gqkv, gz, sm, smt = _inproj(
        x2d, row1(mix_norm), w_main, w_small, w_small.T, bd,
        row1(jnp.tile(fox_q_gain, FOX_HEADS)), row1(jnp.tile(fox_k_gain, FOX_HEADS)), prow, pcol,
        seq=seq, tm=tm)

    cum = smt[SM_F0:SM_G0].reshape(FOX_HEADS // 2, 2, batch, seq).transpose(2, 0, 1, 3)
    b3 = lambda a: a.reshape(batch, seq, a.shape[-1])
    o_fox = _fox_attention(b3(fq), b3(fk), b3(fv), cum, fox_out_gain.reshape(FOX_HEADS // 2, 1, LANES),
                           batch=batch, seq=seq, tq=min(2048, seq), blk=min(512, seq))

    u, w, qd, qk, kdt, egl = _gdn_prep(gqkv, gdn_conv_w.T, sm, smt, batch=batch, seq=seq)
    o_gdn = _gdn_scan(u, w, qd, qk, kdt, egl, b3(gz), row1(gdn_out_gain), batch=batch, seq=seq)

    kx, vx = _memkv(mem.reshape(batch * mem_len, d), row1(mem_norm), w_xkv.astype(BF16), row1(xk_gain),
                    batch=batch, mem_len=mem_len)
    w_r = jnp.concatenate([w_router, jnp.zeros((d, LANES - N_EXPERTS), F32)], axis=1)
    w_r = jnp.stack(_split_bf16(w_r))
    b_r = jnp.concatenate([b_router, jnp.full((LANES - N_EXPERTS,), -jnp.inf, F32)]).reshape(1, LANES)
    x2, h3, mi, mw, cnt = _mid(
        x2d, o_fox.reshape(n, FOX_WIDTH), o_gdn.reshape(n, GDN_WIDTH), w_out.astype(BF16), row1(xattn_norm),
        w_xq.astype(BF16), row1(xq_gain), kx, vx, w_xo.astype(BF16), row1(moe_norm), w_r, b_r,
        seq=seq, mem_len=mem_len, tm=min(ROUTE_TOKENS, seq))

    counts = cnt[0, :N_EXPERTS]
    padded = ((counts + MOE_TM - 1) // MOE_TM) * MOE_TM
    ends = jnp.cumsum(padded)
    starts = ends - padded
    rows_padded = n * TOP_K + N_EXPERTS * MOE_TM
    n_tiles = rows_padded // MOE_TM
    tile_start = jnp.arange(n_tiles, dtype=jnp.int32) * MOE_TM
    tile_expert = jnp.zeros((n_tiles,), jnp.int32)
    tile_limit = jnp.zeros((n_tiles,), jnp.int32)
    last_used = jnp.zeros((), jnp.int32)
    for e in range(N_EXPERTS):
        inside = (tile_start >= starts[e]) & (tile_start < ends[e])
        tile_expert = jnp.where(inside, e, tile_expert)
        tile_limit = jnp.where(inside, starts[e] + counts[e], tile_limit)
        last_used = jnp.where(counts[e] > 0, e, last_used)
    tile_valid = jnp.clip(tile_limit - tile_start, 0, MOE_TM).astype(jnp.int32)
    tile_expert = jnp.where(tile_valid > 0, tile_expert, last_used).astype(jnp.int32)

    pos = _rows(starts.astype(jnp.int32), mi)
    tiles_used = (ends[N_EXPERTS - 1] // MOE_TM).astype(jnp.int32).reshape(1)

    xs = _dispatch(h3, pos, rows_padded=rows_padded)
    y = _experts(tile_expert, tile_valid, tiles_used, xs, w_gate_up, b_gate_up, w_down, b_down)
    out = _combine(pos, y, x2, mw)
    return out.reshape(batch, seq, d)


def kernel(x, mem, mix_norm, w_in, b_forget, fox_q_gain, fox_k_gain, fox_out_gain, gdn_conv_w, gdn_a_log,
           gdn_dt_bias, gdn_out_gain, w_out, xattn_norm, mem_norm, w_xq, w_xkv, xq_gain, xk_gain, w_xo,
           moe_norm, w_router, b_router, w_gate_up, b_gate_up, w_down, b_down):
    depth = mix_norm.shape[0]
    for l in range(depth):
        x = _layer(x, mem, mix_norm[l], w_in[l], b_forget[l], fox_q_gain[l], fox_k_gain[l], fox_out_gain[l],
                   gdn_conv_w[l], gdn_a_log[l], gdn_dt_bias[l], gdn_out_gain[l], w_out[l], xattn_norm[l],
                   mem_norm[l], w_xq[l], w_xkv[l], xq_gain[l], xk_gain[l], w_xo[l], moe_norm[l], w_router[l],
                   b_router[l], w_gate_up[l], b_gate_up[l], w_down[l], b_down[l])
    return x
```

```python
import functools

import jax
import jax.numpy as jnp
from jax import lax
from jax.experimental import pallas as pl
from jax.experimental.pallas import tpu as pltpu

F32 = jnp.float32
BF16 = jnp.bfloat16

EPS = 1e-6
FOX_HEADS, FOX_HEAD_DIM = 8, 64
GDN_HEADS, GDN_HEAD_DIM = 4, 128
FOX_WIDTH = FOX_HEADS * FOX_HEAD_DIM
GDN_WIDTH = GDN_HEADS * GDN_HEAD_DIM
CONV_WIDTH = 4
CHUNK = 64
XA_HEADS = 4
N_EXPERTS = 32
TOP_K = 4
SWIGLU_LIMIT = 7.0
SWIGLU_ALPHA = 1.702
LOG2E = 1.4426950408889634

LANES = 128
SUBLANES = 8
VMEM_LIMIT = 52 * 1024 * 1024

SM_F0, SM_G0, SM_B0, SM_END = 0, 8, 12, 16

NT_DIMS = (((1,), (1,)), ((), ()))


def _cparams(sem):
    return pltpu.CompilerParams(dimension_semantics=sem, vmem_limit_bytes=VMEM_LIMIT)


def _rms(x, gain):
    return x * lax.rsqrt(jnp.mean(x * x, axis=-1, keepdims=True) + EPS) * gain


def _split_bf16(x):
    hi = x.astype(BF16)
    return hi, (x - hi.astype(F32)).astype(BF16)


def _dot_mask_left(mask, x):
    hi, lo = _split_bf16(x)
    mb = mask.astype(BF16)
    return jnp.dot(mb, hi, preferred_element_type=F32) + jnp.dot(mb, lo, preferred_element_type=F32)


def _dot_mask_right(x, mask):
    hi, lo = _split_bf16(x)
    mb = mask.astype(BF16)
    return jnp.dot(hi, mb, preferred_element_type=F32) + jnp.dot(lo, mb, preferred_element_type=F32)


def _softplus(t):
    return jnp.maximum(t, 0.0) + jnp.log1p(jnp.exp(-jnp.abs(t)))


def _sigmoid(t):
    return 1.0 / (1.0 + jnp.exp(-t))


def _small_act(v, bias, alog, idx):
    t = v + bias
    tail = jnp.log1p(jnp.exp(-jnp.abs(t)))
    log_f = jnp.minimum(t, 0.0) - tail
    g = -jnp.exp(alog) * (jnp.maximum(t, 0.0) + tail)
    beta = _sigmoid(v)
    return jnp.where(idx < SM_G0, log_f, jnp.where(idx < SM_B0, g, jnp.where(idx < SM_END, beta, 0.0)))


def _inproj_kernel(x_ref, gain_ref, w_ref, ws_ref, wst_ref, bd_ref, qg_ref, kg_ref, prow_ref, pcol_ref,
                   fq_ref, fk_ref, fv_ref, gqkv_ref, gz_ref, sm_ref, smt_ref, carry_r, carry_c,
                   *, tm, tiles_per_seq):
    i = pl.program_id(0)
    hb = _rms(x_ref[...], gain_ref[...]).astype(BF16)

    def proj(lo, hi):
        return jnp.dot(hb, w_ref[:, lo:hi], preferred_element_type=F32)

    def headnorm(p, g):
        ss = jnp.dot((p * p).astype(BF16), bd_ref[...], preferred_element_type=F32)
        return p * lax.rsqrt(ss * (1.0 / FOX_HEAD_DIM) + EPS) * g

    w0 = FOX_WIDTH
    fq_ref[...] = (headnorm(proj(0, w0), qg_ref[...]) * (FOX_HEAD_DIM ** -0.5 * LOG2E)).astype(BF16)
    fk_ref[...] = headnorm(proj(w0, 2 * w0), kg_ref[...]).astype(BF16)
    fv_ref[...] = proj(2 * w0, 3 * w0).astype(BF16)
    g0 = 3 * w0
    gqkv_ref[...] = proj(g0, g0 + 3 * GDN_WIDTH)
    gz_ref[...] = proj(g0 + 3 * GDN_WIDTH, g0 + 4 * GDN_WIDTH)

    @pl.when(i % tiles_per_seq == 0)
    def _():
        carry_r[...] = jnp.zeros_like(carry_r)
        carry_c[...] = jnp.zeros_like(carry_c)

    sm = jnp.dot(hb, ws_ref[...], preferred_element_type=F32)
    smt = lax.dot_general(wst_ref[...], hb, NT_DIMS, preferred_element_type=F32)
    lane = lax.broadcasted_iota(jnp.int32, (1, LANES), 1)
    srow = lax.broadcasted_iota(jnp.int32, (LANES, 1), 0)
    vals = _small_act(sm, prow_ref[0:1, :], prow_ref[1:2, :], lane)
    vals_t = _small_act(smt, pcol_ref[:, 0:1], pcol_ref[:, 1:2], srow)

    r = lax.broadcasted_iota(jnp.int32, (tm, tm), 0)
    c = lax.broadcasted_iota(jnp.int32, (tm, tm), 1)
    cum = _dot_mask_left(c <= r, vals) + carry_r[...]
    cum_t = _dot_mask_right(vals_t, r <= c) + carry_c[...]
    carry_r[...] = cum[tm - 1:tm, :]
    carry_c[...] = cum_t[:, tm - 1:tm]
    sm_ref[...] = jnp.where(lane < SM_G0, cum, vals)
    smt_ref[...] = jnp.where(srow < SM_G0, cum_t, vals_t)


def _inproj(x2d, gain, w_main, w_small, w_small_t, bd, qg, kg, prow, pcol, *, seq, tm):
    n, d = x2d.shape
    wm = w_main.shape[1]
    full = lambda shape: pl.BlockSpec(shape, lambda i: (0,) * len(shape))
    rows = lambda width: pl.BlockSpec((tm, width), lambda i: (i, 0))
    out_shape = (
        jax.ShapeDtypeStruct((n, FOX_WIDTH), BF16),
        jax.ShapeDtypeStruct((n, FOX_WIDTH), BF16),
        jax.ShapeDtypeStruct((n, FOX_WIDTH), BF16),
        jax.ShapeDtypeStruct((n, 3 * GDN_WIDTH), F32),
        jax.ShapeDtypeStruct((n, GDN_WIDTH), F32),
        jax.ShapeDtypeStruct((n, LANES), F32),
        jax.ShapeDtypeStruct((LANES, n), F32),
    )
    return pl.pallas_call(
        functools.partial(_inproj_kernel, tm=tm, tiles_per_seq=seq // tm),
        grid=(n // tm,),
        in_specs=[rows(d), full((1, d)), full((d, wm)), full((d, LANES)), full((LANES, d)),
                  full((FOX_WIDTH, FOX_WIDTH)), full((1, FOX_WIDTH)), full((1, FOX_WIDTH)),
                  full((SUBLANES, LANES)), full((LANES, 2))],
        out_specs=(rows(FOX_WIDTH), rows(FOX_WIDTH), rows(FOX_WIDTH), rows(3 * GDN_WIDTH), rows(GDN_WIDTH),
                   rows(LANES), pl.BlockSpec((LANES, tm), lambda i: (0, i))),
        out_shape=out_shape,
        scratch_shapes=[pltpu.VMEM((1, LANES), F32), pltpu.VMEM((LANES, 1), F32)],
        compiler_params=_cparams(("arbitrary",)),
        name="inproj",
    )(x2d, gain, w_main, w_small, w_small_t, bd, qg, kg, prow, pcol)


def _fox_kernel(q_ref, k_ref, v_ref, cum_ref, gain_ref, o_ref, m0_ref, m1_ref, acc0_ref, acc1_ref, *, tq, blk):
    qi = pl.program_id(2)
    lane = lax.broadcasted_iota(jnp.int32, (1, LANES), 1)
    lo = lane < FOX_HEAD_DIM
    q = q_ref[...]
    zero = jnp.zeros_like(q)
    q_heads = (jnp.where(lo, q, zero), jnp.where(lo, zero, q))
    q0 = pl.multiple_of(qi * tq, tq)
    c_ref = [cum_ref[hh:hh + 1, pl.ds(q0, LANES)][:, 0:1] for hh in range(2)]
    head_lanes = (lo, jnp.logical_not(lo))
    sum_lane = (FOX_HEAD_DIM, 0)
    ones_col = [jnp.where(lane == sum_lane[hh], 1.0, 0.0).astype(BF16) for hh in range(2)]
    m_refs = (m0_ref, m1_ref)
    acc_refs = (acc0_ref, acc1_ref)
    for hh in range(2):
        m_refs[hh][...] = jnp.full_like(m_refs[hh], -1e30)
        acc_refs[hh][...] = jnp.zeros_like(acc_refs[hh])

    def step(k0, r0=None):
        rows = slice(0 if r0 is None else r0, tq)
        nr = rows.stop - rows.start
        kb = k_ref[pl.ds(k0, blk), :]
        vb = v_ref[pl.ds(k0, blk), :]
        scores = [lax.dot_general(q_heads[hh][rows], kb, NT_DIMS, preferred_element_type=F32) for hh in range(2)]
        for hh in range(2):
            s = scores[hh] + (c_ref[hh] - cum_ref[hh:hh + 1, pl.ds(k0, blk)]) * LOG2E
            if r0 is not None:
                r = lax.broadcasted_iota(jnp.int32, (nr, blk), 0)
                c = lax.broadcasted_iota(jnp.int32, (nr, blk), 1)
                s = jnp.where(c <= r, s, -jnp.inf)
            m_old = m_refs[hh][rows, :]
            m_new = jnp.maximum(m_old, jnp.max(s, axis=-1, keepdims=True))
            alpha = jnp.exp2(m_old - m_new)
            p = jnp.exp2(s - jnp.concatenate([m_new] * (blk // LANES), axis=1))
            v_h = jnp.where(head_lanes[hh], vb, ones_col[hh])
            pv = jnp.dot(p.astype(BF16), v_h, preferred_element_type=F32)
            acc_refs[hh][rows, :] = acc_refs[hh][rows, :] * alpha + pv
            m_refs[hh][rows, :] = m_new

    def body(kp, carry):
        step(pl.multiple_of(kp * (2 * blk), blk))
        step(pl.multiple_of(kp * (2 * blk) + blk, blk))
        return carry

    n_full = qi * (tq // blk)
    lax.fori_loop(0, n_full // 2, body, 0)

    @pl.when(n_full % 2 == 1)
    def _():
        step(pl.multiple_of((n_full - 1) * blk, blk))

    for j in range(tq // blk):
        step(pl.multiple_of(q0 + j * blk, blk), r0=j * blk)

    acc0, acc1 = acc0_ref[...], acc1_ref[...]
    l0 = acc0[:, sum_lane[0]:sum_lane[0] + 1]
    l1 = acc1[:, sum_lane[1]:sum_lane[1] + 1]
    o = jnp.where(lo, acc0 * (1.0 / l0), acc1 * (1.0 / l1))
    o2 = o * o
    ss_lo = jnp.sum(jnp.where(lo, o2, 0.0), axis=-1, keepdims=True)
    ss_hi = jnp.sum(jnp.where(lo, 0.0, o2), axis=-1, keepdims=True)
    ms = jnp.where(lo, ss_lo, ss_hi) * (1.0 / FOX_HEAD_DIM)
    o_ref[...] = (o * lax.rsqrt(ms + EPS) * gain_ref[...]).astype(o_ref.dtype)


def _fox_attention(fq, fk, fv, cum, gain_pairs, *, batch, seq, tq, blk):
    npairs = FOX_HEADS // 2
    return pl.pallas_call(
        functools.partial(_fox_kernel, tq=tq, blk=blk),
        grid=(batch, npairs, seq // tq),
        in_specs=[
            pl.BlockSpec((None, tq, LANES), lambda b, j, i: (b, i, j)),
            pl.BlockSpec((None, seq, LANES), lambda b, j, i: (b, 0, j)),
            pl.BlockSpec((None, seq, LANES), lambda b, j, i: (b, 0, j)),
            pl.BlockSpec((None, None, 2, seq), lambda b, j, i: (b, j, 0, 0)),
            pl.BlockSpec((None, 1, LANES), lambda b, j, i: (j, 0, 0)),
        ],
        out_specs=pl.BlockSpec((None, tq, LANES), lambda b, j, i: (b, i, j)),
        out_shape=jax.ShapeDtypeStruct((batch, seq, FOX_WIDTH), BF16),
        scratch_shapes=[pltpu.VMEM((tq, LANES), F32) for _ in range(4)],
        compiler_params=_cparams(("parallel", "parallel", "arbitrary")),
        name="fox_attention",
    )(fq, fk, fv, cum, gain_pairs)


GDN_BLK = 4 * CHUNK
QK_BLK = 2 * CHUNK


def _gdn_prep_kernel(x_ref, halo_ref, cw_ref, sm_ref, smt_ref,
                     u_ref, w_ref, qd_ref, qk_ref, kdt_ref, egl_ref, xpad_ref):
    i = pl.program_id(1)
    nb = GDN_BLK
    halo = halo_ref[...]
    xpad_ref[0:SUBLANES, :] = jnp.where(i > 0, halo, jnp.zeros_like(halo))
    xpad_ref[SUBLANES:, :] = x_ref[...]
    y = None
    for j in range(CONV_WIDTH):
        start = SUBLANES - (CONV_WIDTH - 1) + j
        term = cw_ref[j:j + 1, :] * xpad_ref[start:start + nb, :]
        y = term if y is None else y + term
    y = y * _sigmoid(y)

    r = lax.broadcasted_iota(jnp.int32, (nb, nb), 0)
    c = lax.broadcasted_iota(jnp.int32, (nb, nb), 1)
    chunk_shift = CHUNK.bit_length() - 1
    same = jnp.right_shift(r, chunk_shift) == jnp.right_shift(c, chunk_shift)
    incl = same & (c <= r)
    strict = same & (c < r)
    sm = sm_ref[...]
    g_cum = _dot_mask_left(incl, sm)
    g_tot = _dot_mask_left(same, sm)
    g_cum_t = _dot_mask_right(smt_ref[...], same & (r <= c))
    eye = (r == c).astype(F32)

    powers, t_invs, rhs = [], [], []
    for h in range(GDN_HEADS):
        sl = slice(h * GDN_HEAD_DIM, (h + 1) * GDN_HEAD_DIM)
        q = y[:, h * GDN_HEAD_DIM:(h + 1) * GDN_HEAD_DIM]
        k = y[:, GDN_WIDTH + h * GDN_HEAD_DIM:GDN_WIDTH + (h + 1) * GDN_HEAD_DIM]
        v = y[:, 2 * GDN_WIDTH + h * GDN_HEAD_DIM:2 * GDN_WIDTH + (h + 1) * GDN_HEAD_DIM]
        qn = q * lax.rsqrt(jnp.sum(q * q, axis=-1, keepdims=True) + EPS) * GDN_HEAD_DIM ** -0.5
        kn = k * lax.rsqrt(jnp.sum(k * k, axis=-1, keepdims=True) + EPS)
        gc = g_cum[:, SM_G0 + h:SM_G0 + h + 1]
        gl = g_tot[:, SM_G0 + h:SM_G0 + h + 1]
        gr = g_cum_t[SM_G0 + h:SM_G0 + h + 1, :]
        beta = sm[:, SM_B0 + h:SM_B0 + h + 1]
        decay = jnp.where(incl, jnp.exp(jnp.where(incl, gc - gr, 0.0)), 0.0)
        qb, kb = qn.astype(BF16), kn.astype(BF16)
        kk = lax.dot_general(kb, kb, NT_DIMS, preferred_element_type=F32)
        a = jnp.where(strict, beta * kk * decay, 0.0)
        eg = jnp.exp(gc)
        powers.append(a)
        t_invs.append(eye - a)
        rhs.append(jnp.concatenate([(v * beta).astype(BF16), (kn * (beta * eg)).astype(BF16)], axis=1))
        qk = lax.dot_general(qb, kb, NT_DIMS, preferred_element_type=F32)
        qk = jnp.where(incl, qk * decay, 0.0).astype(BF16)
        qk_ref[:, h * QK_BLK:(h + 1) * QK_BLK] = jnp.concatenate(
            [qk[j * QK_BLK:(j + 1) * QK_BLK, j * QK_BLK:(j + 1) * QK_BLK] for j in range(nb // QK_BLK)], axis=0)
        qd_ref[:, sl] = (qn * eg).astype(BF16)
        kdt_ref[sl, :] = (kn * jnp.exp(gl - gc)).T.astype(BF16)
        egl_ref[:, sl] = jnp.broadcast_to(jnp.exp(gl), (nb, GDN_HEAD_DIM))

    for _ in range(5):
        for h in range(GDN_HEADS):
            pb = powers[h].astype(BF16)
            powers[h] = jnp.dot(pb, pb, preferred_element_type=F32)
        for h in range(GDN_HEADS):
            t_invs[h] = t_invs[h] + jnp.dot(t_invs[h].astype(BF16), powers[h].astype(BF16),
                                            preferred_element_type=F32)
    for h in range(GDN_HEADS):
        sl = slice(h * GDN_HEAD_DIM, (h + 1) * GDN_HEAD_DIM)
        uw = jnp.dot(t_invs[h].astype(BF16), rhs[h], preferred_element_type=F32)
        u_ref[:, sl] = uw[:, :GDN_HEAD_DIM]
        w_ref[:, sl] = uw[:, GDN_HEAD_DIM:].astype(BF16)


def _gdn_prep(gqkv, conv_w_t, sm, smt, *, batch, seq):
    nb = GDN_BLK
    bps = seq // nb
    hps = nb // SUBLANES
    width = 3 * GDN_WIDTH
    row = lambda w: pl.BlockSpec((None, nb, w), lambda b, i: (b, i, 0))
    return pl.pallas_call(
        _gdn_prep_kernel,
        grid=(batch, bps),
        in_specs=[
            pl.BlockSpec((nb, width), lambda b, i: (b * bps + i, 0)),
            pl.BlockSpec((SUBLANES, width), lambda b, i: (jnp.maximum((b * bps + i) * hps - 1, 0), 0)),
            pl.BlockSpec((CONV_WIDTH, width), lambda b, i: (0, 0)),
            pl.BlockSpec((nb, LANES), lambda b, i: (b * bps + i, 0)),
            pl.BlockSpec((LANES, nb), lambda b, i: (0, b * bps + i)),
        ],
        out_specs=(row(GDN_WIDTH), row(GDN_WIDTH), row(GDN_WIDTH), row(GDN_HEADS * QK_BLK),
                   pl.BlockSpec((None, GDN_WIDTH, nb), lambda b, i: (b, 0, i)), row(GDN_WIDTH)),
        out_shape=(
            jax.ShapeDtypeStruct((batch, seq, GDN_WIDTH), F32),
            jax.ShapeDtypeStruct((batch, seq, GDN_WIDTH), BF16),
            jax.ShapeDtypeStruct((batch, seq, GDN_WIDTH), BF16),
            jax.ShapeDtypeStruct((batch, seq, GDN_HEADS * QK_BLK), BF16),
            jax.ShapeDtypeStruct((batch, GDN_WIDTH, seq), BF16),
            jax.ShapeDtypeStruct((batch, seq, GDN_WIDTH), F32),
        ),
        scratch_shapes=[pltpu.VMEM((nb + SUBLANES, width), F32)],
        compiler_params=_cparams(("parallel", "parallel")),
        name="gdn_prep",
    )(gqkv, gqkv, conv_w_t, sm, smt)


def _gdn_scan_kernel(u_ref, w_ref, qd_ref, qk_ref, kdt_ref, egl_ref, z_ref, gain_ref, o_ref, s_ref, vz_ref,
                     *, batch):
    pb = QK_BLK

    @pl.when(pl.program_id(0) == 0)
    def _():
        s_ref[...] = jnp.zeros_like(s_ref)

    vz_ref[...] = jnp.zeros_like(vz_ref)
    for cidx in range(GDN_BLK // CHUNK):
        rows = slice(cidx * CHUNK, (cidx + 1) * CHUNK)
        blk_rows = slice(cidx * CHUNK // pb * pb, (cidx * CHUNK // pb + 1) * pb)
        for b in range(batch):
            for h in range(GDN_HEADS):
                bh = b * GDN_HEADS + h
                sl = slice(h * GDN_HEAD_DIM, (h + 1) * GDN_HEAD_DIM)
                s_old = s_ref[bh]
                lhs1 = jnp.concatenate([w_ref[b, rows, sl], qd_ref[b, rows, sl]], axis=0)
                r1 = jnp.dot(lhs1, s_old.astype(BF16), preferred_element_type=F32)
                v_new = u_ref[b, rows, sl] - r1[:CHUNK]
                vz_ref[bh, rows, :] = v_new.astype(BF16)
                lhs2 = jnp.concatenate([qk_ref[b, rows, h * pb:(h + 1) * pb], kdt_ref[b, sl, blk_rows]], axis=0)
                r2 = jnp.dot(lhs2, vz_ref[bh, blk_rows, :], preferred_element_type=F32)
                vz_ref[bh, rows, :] = jnp.zeros((CHUNK, GDN_HEAD_DIM), BF16)
                last = egl_ref[b, (cidx + 1) * CHUNK - 1:(cidx + 1) * CHUNK, sl]
                s_ref[bh] = s_old * last + r2[CHUNK:]
                o = r1[CHUNK:] + r2[:CHUNK]
                z = z_ref[b, rows, sl]
                o_ref[b, rows, sl] = (_rms(o, gain_ref[...]) * (z * _sigmoid(z))).astype(o_ref.dtype)


def _gdn_scan(u, w, qd, qk, kdt, egl, z, gain, *, batch, seq):
    nb = GDN_BLK
    row = lambda width: pl.BlockSpec((batch, nb, width), lambda i: (0, i, 0))
    return pl.pallas_call(
        functools.partial(_gdn_scan_kernel, batch=batch),
        grid=(seq // nb,),
        in_specs=[row(GDN_WIDTH), row(GDN_WIDTH), row(GDN_WIDTH), row(GDN_HEADS * QK_BLK),
                  pl.BlockSpec((batch, GDN_WIDTH, nb), lambda i: (0, 0, i)), row(GDN_WIDTH), row(GDN_WIDTH),
                  pl.BlockSpec((1, GDN_HEAD_DIM), lambda i: (0, 0))],
        out_specs=row(GDN_WIDTH),
        out_shape=jax.ShapeDtypeStruct((batch, seq, GDN_WIDTH), BF16),
        scratch_shapes=[pltpu.VMEM((batch * GDN_HEADS, GDN_HEAD_DIM, GDN_HEAD_DIM), F32),
                        pltpu.VMEM((batch * GDN_HEADS, nb, GDN_HEAD_DIM), BF16)],
        compiler_params=_cparams(("arbitrary",)),
        name="gdn_scan",
    )(u, w, qd, qk, kdt, egl, z, gain)


def _memkv_kernel(m_ref, gain_ref, w_ref, kg_ref, k_ref, v_ref):
    d = m_ref.shape[-1]
    hd = d // XA_HEADS
    mb = _rms(m_ref[...], gain_ref[...]).astype(BF16)
    kv = jnp.dot(mb, w_ref[...], preferred_element_type=F32)
    for h in range(XA_HEADS):
        sl = slice(h * hd, (h + 1) * hd)
        k_ref[:, sl] = _rms(kv[:, sl], kg_ref[...]).astype(BF16)
    v_ref[...] = kv[:, d:].astype(BF16)


def _memkv(mem2d, gain, w_xkv, xk_gain, *, batch, mem_len):
    d = mem2d.shape[-1]
    full = lambda shape: pl.BlockSpec(shape, lambda b: (0,) * len(shape))
    row = pl.BlockSpec((mem_len, d), lambda b: (b, 0))
    return pl.pallas_call(
        _memkv_kernel,
        grid=(batch,),
        in_specs=[row, full((1, d)), full((d, 2 * d)), full((1, d // XA_HEADS))],
        out_specs=(row, row),
        out_shape=(jax.ShapeDtypeStruct(mem2d.shape, BF16), jax.ShapeDtypeStruct(mem2d.shape, BF16)),
        compiler_params=_cparams(("parallel",)),
        name="mem_kv",
    )(mem2d, gain, w_xkv, xk_gain)


def _mid_kernel(x_ref, of_ref, og_ref, wo_ref, xg_ref, wq_ref, qg_ref, k_ref, v_ref, wxo_ref, mg_ref,
                wr_ref, br_ref,
                x2_ref, h3_ref, mi_ref, mw_ref, cnt_ref, carry_ref, *, tm):
    i = pl.program_id(0)
    d = x_ref.shape[-1]
    hd = d // XA_HEADS
    x1 = (x_ref[...]
          + jnp.dot(of_ref[...], wo_ref[0:FOX_WIDTH, :], preferred_element_type=F32)
          + jnp.dot(og_ref[...], wo_ref[FOX_WIDTH:, :], preferred_element_type=F32))
    h2 = _rms(x1, xg_ref[...]).astype(BF16)
    q = jnp.dot(h2, wq_ref[...], preferred_element_type=F32)
    heads = []
    for h in range(XA_HEADS):
        sl = slice(h * hd, (h + 1) * hd)
        qn = (_rms(q[:, sl], qg_ref[...]) * hd ** -0.5).astype(BF16)
        s = lax.dot_general(qn, k_ref[:, sl], NT_DIMS, preferred_element_type=F32)
        p = jnp.exp(s - jnp.max(s, axis=-1, keepdims=True))
        p = p * (1.0 / jnp.sum(p, axis=-1, keepdims=True))
        heads.append(jnp.dot(p.astype(BF16), v_ref[:, sl], preferred_element_type=F32).astype(BF16))
    x2 = x1 + jnp.dot(jnp.concatenate(heads, axis=-1), wxo_ref[...], preferred_element_type=F32)
    x2_ref[...] = x2
    h3 = _rms(x2, mg_ref[...])
    for cc in range(SUBLANES):
        h3_ref[pl.ds(cc, tm, stride=SUBLANES), :] = h3[:, cc * LANES:(cc + 1) * LANES]
    h_hi, h_lo = _split_bf16(h3)
    logits = (jnp.dot(h_hi, wr_ref[0], preferred_element_type=F32)
              + jnp.dot(h_lo, wr_ref[0], preferred_element_type=F32)
              + jnp.dot(h_hi, wr_ref[1], preferred_element_type=F32)) + br_ref[...]
    lane = lax.broadcasted_iota(jnp.int32, (tm, LANES), 1)
    work = logits
    vals, idxs = [], []
    onehot = jnp.zeros((tm, LANES), F32)
    for _ in range(TOP_K):
        mx = jnp.max(work, axis=-1, keepdims=True)
        idx = jnp.min(jnp.where(work == mx, lane, LANES), axis=-1, keepdims=True)
        sel = lane == idx
        onehot = jnp.where(sel, 1.0, onehot)
        work = jnp.where(sel, -jnp.inf, work)
        vals.append(mx)
        idxs.append(idx)
    es = [jnp.exp(v - vals[0]) for v in vals]
    inv_denom = 1.0 / (es[0] + es[1] + es[2] + es[3])

    @pl.when(i == 0)
    def _():
        carry_ref[...] = jnp.zeros_like(carry_ref)

    r = lax.broadcasted_iota(jnp.int32, (tm, tm), 0)
    c = lax.broadcasted_iota(jnp.int32, (tm, tm), 1)
    before = jnp.dot((c < r).astype(BF16), onehot.astype(BF16), preferred_element_type=F32) + carry_ref[...]
    mi = jnp.zeros((tm, LANES), F32)
    mw = jnp.zeros((tm, LANES), F32)
    for kk in range(TOP_K):
        rank = jnp.sum(jnp.where(lane == idxs[kk], before, 0.0), axis=-1, keepdims=True)
        mi = jnp.where(lane == kk, idxs[kk].astype(F32), mi)
        mi = jnp.where(lane == TOP_K + kk, rank, mi)
        mw = jnp.where(lane == kk, es[kk] * inv_denom, mw)
    mi_ref[...] = mi.T[0:2 * TOP_K, :].astype(jnp.int32)
    mw_ref[...] = mw
    total = carry_ref[...] + jnp.sum(onehot, axis=0, keepdims=True)
    carry_ref[...] = total
    cnt_ref[...] = jnp.broadcast_to(total, cnt_ref.shape).astype(jnp.int32)


def _mid(x2d, o_fox, o_gdn, w_out, xg, w_xq, xq_gain, kx, vx, w_xo, mg, w_r, b_r, *, seq, mem_len, tm):
    n, d = x2d.shape
    full = lambda shape: pl.BlockSpec(shape, lambda i: (0,) * len(shape))
    rows = lambda width: pl.BlockSpec((tm, width), lambda i: (i, 0))
    mem = pl.BlockSpec((mem_len, d), lambda i: (i // (seq // tm), 0))
    return pl.pallas_call(
        functools.partial(_mid_kernel, tm=tm),
        grid=(n // tm,),
        in_specs=[rows(d), rows(FOX_WIDTH), rows(GDN_WIDTH), full((d, d)), full((1, d)), full((d, d)),
                  full((1, d // XA_HEADS)), mem, mem, full((d, d)), full((1, d)), full((2, d, LANES)),
                  full((1, LANES))],
        out_specs=(rows(d), pl.BlockSpec((tm * d // LANES, LANES), lambda i: (i, 0)),
                   pl.BlockSpec((None, 2 * TOP_K, tm), lambda i: (i, 0, 0)), rows(LANES),
                   full((SUBLANES, LANES))),
        out_shape=(jax.ShapeDtypeStruct((n, d), F32), jax.ShapeDtypeStruct((n * d // LANES, LANES), F32),
                   jax.ShapeDtypeStruct((n // tm, 2 * TOP_K, tm), jnp.int32),
                   jax.ShapeDtypeStruct((n, LANES), F32),
                   jax.ShapeDtypeStruct((SUBLANES, LANES), jnp.int32)),
        scratch_shapes=[pltpu.VMEM((1, LANES), F32)],
        compiler_params=_cparams(("arbitrary",)),
        name="outproj_xattn_router",
    )(x2d, o_fox, o_gdn, w_out, xg, w_xq, xq_gain, kx, vx, w_xo, mg, w_r, b_r)


MOE_TM = 512
ROUTE_TOKENS = 512


def _rows_kernel(tbl_ref, mi_ref, pos_ref, te_ref, nv_ref):
    tm = mi_ref.shape[-1]
    experts = mi_ref[:, 0:TOP_K, :]
    base = jnp.zeros_like(experts)
    tile_start = lax.broadcasted_iota(jnp.int32, te_ref.shape, 1) * MOE_TM
    tile_expert = jnp.zeros(te_ref.shape, jnp.int32)
    tile_limit = jnp.zeros(te_ref.shape, jnp.int32)
    last_used = jnp.int32(0)
    for e in range(N_EXPERTS):
        start, end, count = tbl_ref[e], tbl_ref[N_EXPERTS + e], tbl_ref[2 * N_EXPERTS + e]
        base = jnp.where(experts == e, start, base)
        inside = (tile_start >= start) & (tile_start < end)
        tile_expert = jnp.where(inside, e, tile_expert)
        tile_limit = jnp.where(inside, start + count, tile_limit)
        last_used = jnp.where(count > 0, e, last_used)
    rows = (base + mi_ref[:, TOP_K:2 * TOP_K, :]) * SUBLANES
    for kk in range(TOP_K):
        pos_ref[:, :, kk * tm:(kk + 1) * tm] = rows[:, kk:kk + 1, :]
    valid = jnp.clip(tile_limit - tile_start, 0, MOE_TM)
    nv_ref[...] = valid
    te_ref[...] = jnp.where(valid > 0, tile_expert, last_used)


def _rows(tbl, mi, n_tiles):
    steps, _, tm = mi.shape
    lanes = pl.cdiv(n_tiles, LANES) * LANES
    whole = lambda shape: pl.BlockSpec(shape, lambda i, t: (0,) * len(shape))
    grid_spec = pltpu.PrefetchScalarGridSpec(
        num_scalar_prefetch=1,
        grid=(1,),
        in_specs=[whole(mi.shape)],
        out_specs=(whole((steps, 1, TOP_K * tm)), whole((1, lanes)), whole((1, lanes))),
    )
    pos, te, nv = pl.pallas_call(
        _rows_kernel,
        grid_spec=grid_spec,
        out_shape=(jax.ShapeDtypeStruct((steps, 1, TOP_K * tm), jnp.int32),
                   jax.ShapeDtypeStruct((1, lanes), jnp.int32), jax.ShapeDtypeStruct((1, lanes), jnp.int32)),
        compiler_params=_cparams(("arbitrary",)),
        name="moe_rows",
    )(tbl, mi)
    return pos, te[0, :n_tiles], nv[0, :n_tiles]


def _dispatch_kernel(pos_ref, src_ref, dst_ref, sem):
    nt = src_ref.shape[0] // SUBLANES

    def copy(t, kk):
        src = src_ref.at[pl.ds(pl.multiple_of(t * SUBLANES, SUBLANES), SUBLANES), :]
        row = pl.multiple_of(pos_ref[0, kk * nt + t], SUBLANES)
        return pltpu.make_async_copy(src, dst_ref.at[pl.ds(row, SUBLANES), :], sem)

    def issue(t, carry):
        for kk in range(TOP_K):
            copy(t, kk).start(priority=kk % 2)
        return carry

    lax.fori_loop(0, nt, issue, 0, unroll=4)

    for kk in range(TOP_K):
        pltpu.make_async_copy(src_ref, dst_ref.at[pl.ds(0, nt * SUBLANES), :], sem).wait()


def _dispatch(h3_tiles, pos, *, rows_padded):
    n = h3_tiles.shape[0] // SUBLANES
    nt = pos.shape[-1] // TOP_K
    return pl.pallas_call(
        _dispatch_kernel,
        grid=(n // nt,),
        in_specs=[pl.BlockSpec((None, 1, nt * TOP_K), lambda i: (i, 0, 0), memory_space=pltpu.SMEM),
                  pl.BlockSpec((nt * SUBLANES, LANES), lambda i: (i, 0))],
        out_specs=pl.BlockSpec(memory_space=pl.ANY),
        out_shape=jax.ShapeDtypeStruct((rows_padded * SUBLANES, LANES), F32),
        scratch_shapes=[pltpu.SemaphoreType.DMA],
        compiler_params=pltpu.CompilerParams(dimension_semantics=("arbitrary",), has_side_effects=True),
        name="moe_dispatch",
    )(pos, h3_tiles)


def _expert_kernel(te_ref, nv_ref, nu_ref, xs_ref, wgu_ref, bgu_ref, wd_ref, bd_ref, y_ref, wgu_bf, wd_bf, acc_ref,
                   *, chunk):
    i = pl.program_id(0)
    tm = xs_ref.shape[0] // SUBLANES
    d = wd_bf.shape[1]
    f = wd_bf.shape[0]
    nvalid = nv_ref[i]
    first = jnp.logical_or(i == 0, te_ref[i] != te_ref[jnp.maximum(i - 1, 0)])

    @pl.when(jnp.logical_and(first, nvalid > 0))
    def _():
        wgu_bf[...] = wgu_ref[0].astype(BF16)
        wd_bf[...] = wd_ref[0].astype(BF16)

    @pl.when(nvalid > 0)
    def _():
        row = lax.broadcasted_iota(jnp.int32, (tm, 1), 0)
        x = jnp.concatenate([xs_ref[pl.ds(cc, tm, stride=SUBLANES), :] for cc in range(SUBLANES)], axis=-1)
        x = jnp.where(row < nvalid, x, 0.0).astype(BF16)
        for j in range(f // chunk):
            cs = slice(j * chunk, (j + 1) * chunk)
            us = slice(f + j * chunk, f + (j + 1) * chunk)
            g = jnp.dot(x, wgu_bf[:, cs], preferred_element_type=F32) + bgu_ref[0, :, cs]
            u = jnp.dot(x, wgu_bf[:, us], preferred_element_type=F32) + bgu_ref[0, :, us]
            gate = jnp.minimum(g, SWIGLU_LIMIT)
            up = jnp.clip(u, -SWIGLU_LIMIT, SWIGLU_LIMIT)
            act = ((up + 1.0) * (gate * _sigmoid(SWIGLU_ALPHA * gate))).astype(BF16)
            part = jnp.dot(act, wd_bf[cs, :], preferred_element_type=F32)
            if j == 0:
                acc_ref[...] = part + bd_ref[0]
            else:
                acc_ref[...] += part
        for cc in range(SUBLANES):
            y_ref[pl.ds(cc, tm, stride=SUBLANES), :] = acc_ref[:, cc * LANES:(cc + 1) * LANES]

    @pl.when(nvalid <= 0)
    def _():
        y_ref[...] = jnp.zeros_like(y_ref)


def _experts(tile_expert, tile_valid, tiles_used, xs, w_gate_up, b_gate_up, w_down, b_down):
    rows_padded = xs.shape[0] // SUBLANES
    e, d, f2 = w_gate_up.shape
    f = f2 // 2
    tm = MOE_TM
    grid_spec = pltpu.PrefetchScalarGridSpec(
        num_scalar_prefetch=3,
        grid=(rows_padded // tm,),
        in_specs=[
            pl.BlockSpec((tm * SUBLANES, LANES), lambda i, te, nv, nu: (jnp.minimum(i, nu[0]), 0)),
            pl.BlockSpec((1, d, f2), lambda i, te, nv, nu: (te[i], 0, 0)),
            pl.BlockSpec((1, 1, f2), lambda i, te, nv, nu: (te[i], 0, 0)),
            pl.BlockSpec((1, f, d), lambda i, te, nv, nu: (te[i], 0, 0)),
            pl.BlockSpec((1, 1, d), lambda i, te, nv, nu: (te[i], 0, 0)),
        ],
        out_specs=pl.BlockSpec((tm * SUBLANES, LANES), lambda i, te, nv, nu: (jnp.minimum(i, nu[0]), 0)),
        scratch_shapes=[pltpu.VMEM((d, f2), BF16), pltpu.VMEM((f, d), BF16), pltpu.VMEM((tm, d), F32)],
    )
    return pl.pallas_call(
        functools.partial(_expert_kernel, chunk=512),
        grid_spec=grid_spec,
        out_shape=jax.ShapeDtypeStruct(xs.shape, F32),
        compiler_params=_cparams(("arbitrary",)),
        name="moe_experts",
    )(tile_expert, tile_valid, tiles_used, xs, w_gate_up, b_gate_up.reshape(e, 1, f2), w_down, b_down.reshape(e, 1, d))


def _combine_kernel(pos_ref, y_ref, x2_ref, mw_ref, o_ref, ybuf, sems):
    nt = x2_ref.shape[0]
    half = nt // 2

    def copy(t, kk, hf):
        row = pl.multiple_of(pos_ref[0, kk * nt + t], SUBLANES)
        dst = ybuf.at[kk, pl.ds(pl.multiple_of(t * SUBLANES, SUBLANES), SUBLANES), :]
        return pltpu.make_async_copy(y_ref.at[pl.ds(row, SUBLANES), :], dst, sems.at[hf])

    def issue_half(hf):
        def issue(t, carry):
            for kk in range(TOP_K):
                copy(t, kk, hf).start(priority=kk % 2)
            return carry

        lax.fori_loop(hf * half, (hf + 1) * half, issue, 0, unroll=4)

    def drain_half(hf):
        for kk in range(TOP_K):
            rows = pl.ds(hf * half * SUBLANES, half * SUBLANES)
            pltpu.make_async_copy(y_ref.at[rows, :], ybuf.at[kk, rows, :], sems.at[hf]).wait()

    def combine_half(hf):
        rows = slice(hf * half, (hf + 1) * half)
        mw = mw_ref[rows, :]
        for cc in range(SUBLANES):
            cs = slice(cc * LANES, (cc + 1) * LANES)
            acc = x2_ref[rows, cs]
            for kk in range(TOP_K):
                ys = ybuf.at[kk][pl.ds(hf * half * SUBLANES + cc, half, stride=SUBLANES), :]
                acc = acc + mw[:, kk:kk + 1] * ys
            o_ref[rows, cs] = acc

    issue_half(0)
    issue_half(1)
    drain_half(0)
    combine_half(0)
    drain_half(1)
    combine_half(1)


def _combine(pos, y, x2, mw):
    n, d = x2.shape
    nt = pos.shape[-1] // TOP_K
    return pl.pallas_call(
        _combine_kernel,
        grid=(n // nt,),
        in_specs=[pl.BlockSpec((None, 1, nt * TOP_K), lambda i: (i, 0, 0), memory_space=pltpu.SMEM),
                  pl.BlockSpec(memory_space=pl.ANY),
                  pl.BlockSpec((nt, d), lambda i: (i, 0)),
                  pl.BlockSpec((nt, LANES), lambda i: (i, 0))],
        out_specs=pl.BlockSpec((nt, d), lambda i: (i, 0)),
        out_shape=jax.ShapeDtypeStruct((n, d), F32),
        scratch_shapes=[pltpu.VMEM((TOP_K, nt * SUBLANES, LANES), F32), pltpu.SemaphoreType.DMA((2,))],
        compiler_params=_cparams(("arbitrary",)),
        name="moe_combine",
    )(pos, y, x2, mw)


def _layer(x, mem, mix_norm, w_in, b_forget, fox_q_gain, fox_k_gain, fox_out_gain, gdn_conv_w, gdn_a_log,
           gdn_dt_bias, gdn_out_gain, w_out, xattn_norm, mem_norm, w_xq, w_xkv, xq_gain, xk_gain, w_xo,
           moe_norm, w_router, b_router, w_gate_up, b_gate_up, w_down, b_down):
    batch, seq, d = x.shape
    n = batch * seq
    mem_len = mem.shape[1]
    x2d = x.reshape(n, d)

    o_ff = 3 * FOX_WIDTH
    o_gq = o_ff + FOX_HEADS
    o_ga = o_gq + 3 * GDN_WIDTH
    o_gb = o_ga + GDN_HEADS
    o_gz = o_gb + GDN_HEADS
    w_main = jnp.concatenate([w_in[:, :o_ff], w_in[:, o_gq:o_ga], w_in[:, o_gz:]], axis=1).astype(BF16)
    w_small = jnp.concatenate([w_in[:, o_ff:o_gq], w_in[:, o_ga:o_gz],
                               jnp.zeros((d, LANES - SM_END), F32)], axis=1).astype(BF16)
    prow = jnp.zeros((SUBLANES, LANES), F32)
    prow = prow.at[0, SM_F0:SM_G0].set(b_forget).at[0, SM_G0:SM_B0].set(gdn_dt_bias)
    prow = prow.at[1, SM_G0:SM_B0].set(gdn_a_log)
    pcol = prow[0:2].T
    head_id = jnp.arange(FOX_WIDTH) // FOX_HEAD_DIM
    bd = (head_id[:, None] == head_id[None, :]).astype(BF16)
    row1 = lambda v: v.reshape(1, -1)

    tm = min(512, seq)
    fq, fk, fv, gqkv, gz, sm, smt = _inproj(
        x2d, row1(mix_norm), w_main, w_small, w_small.T, bd,
        row1(jnp.tile(fox_q_gain, FOX_HEADS)), row1(jnp.tile(fox_k_gain, FOX_HEADS)), prow, pcol,
        seq=seq, tm=tm)

    cum = smt[SM_F0:SM_G0].reshape(FOX_HEADS // 2, 2, batch, seq).transpose(2, 0, 1, 3)
    b3 = lambda a: a.reshape(batch, seq, a.shape[-1])
    o_fox = _fox_attention(b3(fq), b3(fk), b3(fv), cum, fox_out_gain.reshape(FOX_HEADS // 2, 1, LANES),
                           batch=batch, seq=seq, tq=min(2048, seq), blk=min(512, seq))

    u, w, qd, qk, kdt, egl = _gdn_prep(gqkv, gdn_conv_w.T, sm, smt, batch=batch, seq=seq)
    o_gdn = _gdn_scan(u, w, qd, qk, kdt, egl, b3(gz), row1(gdn_out_gain), batch=batch, seq=seq)

    kx, vx = _memkv(mem.reshape(batch * mem_len, d), row1(mem_norm), w_xkv.astype(BF16), row1(xk_gain),
                    batch=batch, mem_len=mem_len)
    w_r = jnp.concatenate([w_router, jnp.zeros((d, LANES - N_EXPERTS), F32)], axis=1)
    w_r = jnp.stack(_split_bf16(w_r))
    b_r = jnp.concatenate([b_router, jnp.full((LANES - N_EXPERTS,), -jnp.inf, F32)]).reshape(1, LANES)
    x2, h3, mi, mw, cnt = _mid(
        x2d, o_fox.reshape(n, FOX_WIDTH), o_gdn.reshape(n, GDN_WIDTH), w_out.astype(BF16), row1(xattn_norm),
        w_xq.astype(BF16), row1(xq_gain), kx, vx, w_xo.astype(BF16), row1(moe_norm), w_r, b_r,
        seq=seq, mem_len=mem_len, tm=min(ROUTE_TOKENS, seq))

    counts = cnt[0, :N_EXPERTS]
    padded = ((counts + MOE_TM - 1) // MOE_TM) * MOE_TM
    ends = jnp.cumsum(padded)
    starts = ends - padded
    rows_padded = n * TOP_K + N_EXPERTS * MOE_TM
    n_tiles = rows_padded // MOE_TM
    tbl = jnp.concatenate([starts, ends, counts]).astype(jnp.int32)
    pos, tile_expert, tile_valid = _rows(tbl, mi, n_tiles)
    tiles_used = (ends[N_EXPERTS - 1] // MOE_TM).astype(jnp.int32).reshape(1)

    xs = _dispatch(h3, pos, rows_padded=rows_padded)
    y = _experts(tile_expert, tile_valid, tiles_used, xs, w_gate_up, b_gate_up, w_down, b_down)
    out = _combine(pos, y, x2, mw)
    return out.reshape(batch, seq, d)


def kernel(x, mem, mix_norm, w_in, b_forget, fox_q_gain, fox_k_gain, fox_out_gain, gdn_conv_w, gdn_a_log,
           gdn_dt_bias, gdn_out_gain, w_out, xattn_norm, mem_norm, w_xq, w_xkv, xq_gain, xk_gain, w_xo,
           moe_norm, w_router, b_router, w_gate_up, b_gate_up, w_down, b_down):
    depth = mix_norm.shape[0]
    for l in range(depth):
        x = _layer(x, mem, mix_norm[l], w_in[l], b_forget[l], fox_q_gain[l], fox_k_gain[l], fox_out_gain[l],
                   gdn_conv_w[l], gdn_a_log[l], gdn_dt_bias[l], gdn_out_gain[l], w_out[l], xattn_norm[l],
                   mem_norm[l], w_xq[l], w_xkv[l], xq_gain[l], xk_gain[l], w_xo[l], moe_norm[l], w_router[l],
                   b_router[l], w_gate_up[l], b_gate_up[l], w_down[l], b_down[l])
    return x
```

```python
import functools

import jax
import jax.numpy as jnp
from jax import lax
from jax.experimental import pallas as pl
from jax.experimental.pallas import tpu as pltpu

F32 = jnp.float32
BF16 = jnp.bfloat16

EPS = 1e-6
FOX_HEADS, FOX_HEAD_DIM = 8, 64
GDN_HEADS, GDN_HEAD_DIM = 4, 128
FOX_WIDTH = FOX_HEADS * FOX_HEAD_DIM
GDN_WIDTH = GDN_HEADS * GDN_HEAD_DIM
CONV_WIDTH = 4
CHUNK = 64
XA_HEADS = 4
N_EXPERTS = 32
TOP_K = 4
SWIGLU_LIMIT = 7.0
SWIGLU_ALPHA = 1.702
LOG2E = 1.4426950408889634

LANES = 128
SUBLANES = 8
VMEM_LIMIT = 52 * 1024 * 1024

SM_F0, SM_G0, SM_B0, SM_END = 0, 8, 12, 16

NT_DIMS = (((1,), (1,)), ((), ()))


def _cparams(sem):
    return pltpu.CompilerParams(dimension_semantics=sem, vmem_limit_bytes=VMEM_LIMIT)


def _rms(x, gain):
    return x * lax.rsqrt(jnp.mean(x * x, axis=-1, keepdims=True) + EPS) * gain


def _split_bf16(x):
    hi = x.astype(BF16)
    return hi, (x - hi.astype(F32)).astype(BF16)


def _dot_mask_left(mask, x):
    hi, lo = _split_bf16(x)
    mb = mask.astype(BF16)
    return jnp.dot(mb, hi, preferred_element_type=F32) + jnp.dot(mb, lo, preferred_element_type=F32)


def _dot_mask_right(x, mask):
    hi, lo = _split_bf16(x)
    mb = mask.astype(BF16)
    return jnp.dot(hi, mb, preferred_element_type=F32) + jnp.dot(lo, mb, preferred_element_type=F32)


def _softplus(t):
    return jnp.maximum(t, 0.0) + jnp.log1p(jnp.exp(-jnp.abs(t)))


def _sigmoid(t):
    return 1.0 / (1.0 + jnp.exp(-t))


def _small_act(v, bias, alog, idx):
    t = v + bias
    tail = jnp.log1p(jnp.exp(-jnp.abs(t)))
    log_f = jnp.minimum(t, 0.0) - tail
    g = -jnp.exp(alog) * (jnp.maximum(t, 0.0) + tail)
    beta = _sigmoid(v)
    return jnp.where(idx < SM_G0, log_f, jnp.where(idx < SM_B0, g, jnp.where(idx < SM_END, beta, 0.0)))


def _inproj_kernel(x_ref, gain_ref, w_ref, ws_ref, wst_ref, bd_ref, qg_ref, kg_ref, prow_ref, pcol_ref,
                   fq_ref, fk_ref, fv_ref, gqkv_ref, gz_ref, sm_ref, smt_ref, carry_r, carry_c,
                   *, tm, tiles_per_seq):
    i = pl.program_id(0)
    hb = _rms(x_ref[...], gain_ref[...]).astype(BF16)

    def proj(lo, hi):
        return jnp.dot(hb, w_ref[:, lo:hi], preferred_element_type=F32)

    def headnorm(p, g):
        ss = jnp.dot((p * p).astype(BF16), bd_ref[...], preferred_element_type=F32)
        return p * lax.rsqrt(ss * (1.0 / FOX_HEAD_DIM) + EPS) * g

    w0 = FOX_WIDTH
    fq_ref[...] = (headnorm(proj(0, w0), qg_ref[...]) * (FOX_HEAD_DIM ** -0.5 * LOG2E)).astype(BF16)
    fk_ref[...] = headnorm(proj(w0, 2 * w0), kg_ref[...]).astype(BF16)
    fv_ref[...] = proj(2 * w0, 3 * w0).astype(BF16)
    g0 = 3 * w0
    gqkv_ref[...] = proj(g0, g0 + 3 * GDN_WIDTH)
    gz_ref[...] = proj(g0 + 3 * GDN_WIDTH, g0 + 4 * GDN_WIDTH)

    @pl.when(i % tiles_per_seq == 0)
    def _():
        carry_r[...] = jnp.zeros_like(carry_r)
        carry_c[...] = jnp.zeros_like(carry_c)

    sm = jnp.dot(hb, ws_ref[...], preferred_element_type=F32)
    smt = lax.dot_general(wst_ref[...], hb, NT_DIMS, preferred_element_type=F32)
    lane = lax.broadcasted_iota(jnp.int32, (1, LANES), 1)
    srow = lax.broadcasted_iota(jnp.int32, (LANES, 1), 0)
    vals = _small_act(sm, prow_ref[0:1, :], prow_ref[1:2, :], lane)
    vals_t = _small_act(smt, pcol_ref[:, 0:1], pcol_ref[:, 1:2], srow)

    r = lax.broadcasted_iota(jnp.int32, (tm, tm), 0)
    c = lax.broadcasted_iota(jnp.int32, (tm, tm), 1)
    cum = _dot_mask_left(c <= r, vals) + carry_r[...]
    cum_t = _dot_mask_right(vals_t, r <= c) + carry_c[...]
    carry_r[...] = cum[tm - 1:tm, :]
    carry_c[...] = cum_t[:, tm - 1:tm]
    sm_ref[...] = jnp.where(lane < SM_G0, cum, vals)
    smt_ref[...] = jnp.where(srow < SM_G0, cum_t, vals_t)


def _inproj(x2d, gain, w_main, w_small, w_small_t, bd, qg, kg, prow, pcol, *, seq, tm):
    n, d = x2d.shape
    wm = w_main.shape[1]
    full = lambda shape: pl.BlockSpec(shape, lambda i: (0,) * len(shape))
    rows = lambda width: pl.BlockSpec((tm, width), lambda i: (i, 0))
    out_shape = (
        jax.ShapeDtypeStruct((n, FOX_WIDTH), BF16),
        jax.ShapeDtypeStruct((n, FOX_WIDTH), BF16),
        jax.ShapeDtypeStruct((n, FOX_WIDTH), BF16),
        jax.ShapeDtypeStruct((n, 3 * GDN_WIDTH), F32),
        jax.ShapeDtypeStruct((n, GDN_WIDTH), F32),
        jax.ShapeDtypeStruct((n, LANES), F32),
        jax.ShapeDtypeStruct((LANES, n), F32),
    )
    return pl.pallas_call(
        functools.partial(_inproj_kernel, tm=tm, tiles_per_seq=seq // tm),
        grid=(n // tm,),
        in_specs=[rows(d), full((1, d)), full((d, wm)), full((d, LANES)), full((LANES, d)),
                  full((FOX_WIDTH, FOX_WIDTH)), full((1, FOX_WIDTH)), full((1, FOX_WIDTH)),
                  full((SUBLANES, LANES)), full((LANES, 2))],
        out_specs=(rows(FOX_WIDTH), rows(FOX_WIDTH), rows(FOX_WIDTH), rows(3 * GDN_WIDTH), rows(GDN_WIDTH),
                   rows(LANES), pl.BlockSpec((LANES, tm), lambda i: (0, i))),
        out_shape=out_shape,
        scratch_shapes=[pltpu.VMEM((1, LANES), F32), pltpu.VMEM((LANES, 1), F32)],
        compiler_params=_cparams(("arbitrary",)),
        name="inproj",
    )(x2d, gain, w_main, w_small, w_small_t, bd, qg, kg, prow, pcol)


def _fox_kernel(q_ref, k_ref, v_ref, cum_ref, gain_ref, o_ref, m0_ref, m1_ref, acc0_ref, acc1_ref, *, tq, blk):
    qi = pl.program_id(2)
    head0 = 2 * pl.program_id(1)
    lane = lax.broadcasted_iota(jnp.int32, (1, LANES), 1)
    lo = lane < FOX_HEAD_DIM
    q = q_ref[...]
    zero = jnp.zeros_like(q)
    q_heads = (jnp.where(lo, q, zero), jnp.where(lo, zero, q))
    q0 = pl.multiple_of(qi * tq, tq)
    head_row = lax.broadcasted_iota(jnp.int32, (FOX_HEADS, 1), 0)

    def cum_row(hh, start, size):
        block = cum_ref[:, pl.ds(start, size)]
        return jnp.sum(jnp.where(head_row == head0 + hh, block, 0.0), axis=0, keepdims=True)

    c_ref = [cum_row(hh, q0, LANES)[:, 0:1] for hh in range(2)]
    head_lanes = (lo, jnp.logical_not(lo))
    sum_lane = (FOX_HEAD_DIM, 0)
    ones_col = [jnp.where(lane == sum_lane[hh], 1.0, 0.0).astype(BF16) for hh in range(2)]
    m_refs = (m0_ref, m1_ref)
    acc_refs = (acc0_ref, acc1_ref)
    for hh in range(2):
        m_refs[hh][...] = jnp.full_like(m_refs[hh], -1e30)
        acc_refs[hh][...] = jnp.zeros_like(acc_refs[hh])

    def step(k0, r0=None):
        rows = slice(0 if r0 is None else r0, tq)
        nr = rows.stop - rows.start
        kb = k_ref[pl.ds(k0, blk), :]
        vb = v_ref[pl.ds(k0, blk), :]
        scores = [lax.dot_general(q_heads[hh][rows], kb, NT_DIMS, preferred_element_type=F32) for hh in range(2)]
        for hh in range(2):
            s = scores[hh] + (c_ref[hh] - cum_row(hh, k0, blk)) * LOG2E
            if r0 is not None:
                r = lax.broadcasted_iota(jnp.int32, (nr, blk), 0)
                c = lax.broadcasted_iota(jnp.int32, (nr, blk), 1)
                s = jnp.where(c <= r, s, -jnp.inf)
            m_old = m_refs[hh][rows, :]
            m_new = jnp.maximum(m_old, jnp.max(s, axis=-1, keepdims=True))
            alpha = jnp.exp2(m_old - m_new)
            p = jnp.exp2(s - jnp.concatenate([m_new] * (blk // LANES), axis=1))
            v_h = jnp.where(head_lanes[hh], vb, ones_col[hh])
            pv = jnp.dot(p.astype(BF16), v_h, preferred_element_type=F32)
            acc_refs[hh][rows, :] = acc_refs[hh][rows, :] * alpha + pv
            m_refs[hh][rows, :] = m_new

    def body(kp, carry):
        step(pl.multiple_of(kp * (2 * blk), blk))
        step(pl.multiple_of(kp * (2 * blk) + blk, blk))
        return carry

    n_full = qi * (tq // blk)
    lax.fori_loop(0, n_full // 2, body, 0)

    @pl.when(n_full % 2 == 1)
    def _():
        step(pl.multiple_of((n_full - 1) * blk, blk))

    for j in range(tq // blk):
        step(pl.multiple_of(q0 + j * blk, blk), r0=j * blk)

    acc0, acc1 = acc0_ref[...], acc1_ref[...]
    l0 = acc0[:, sum_lane[0]:sum_lane[0] + 1]
    l1 = acc1[:, sum_lane[1]:sum_lane[1] + 1]
    o = jnp.where(lo, acc0 * (1.0 / l0), acc1 * (1.0 / l1))
    o2 = o * o
    ss_lo = jnp.sum(jnp.where(lo, o2, 0.0), axis=-1, keepdims=True)
    ss_hi = jnp.sum(jnp.where(lo, 0.0, o2), axis=-1, keepdims=True)
    ms = jnp.where(lo, ss_lo, ss_hi) * (1.0 / FOX_HEAD_DIM)
    o_ref[...] = (o * lax.rsqrt(ms + EPS) * gain_ref[...]).astype(o_ref.dtype)


def _fox_attention(fq, fk, fv, cum, gain_pairs, *, batch, seq, tq, blk):
    npairs = FOX_HEADS // 2
    return pl.pallas_call(
        functools.partial(_fox_kernel, tq=tq, blk=blk),
        grid=(batch, npairs, seq // tq),
        in_specs=[
            pl.BlockSpec((None, tq, LANES), lambda b, j, i: (b, i, j)),
            pl.BlockSpec((None, seq, LANES), lambda b, j, i: (b, 0, j)),
            pl.BlockSpec((None, seq, LANES), lambda b, j, i: (b, 0, j)),
            pl.BlockSpec((FOX_HEADS, seq), lambda b, j, i: (0, b)),
            pl.BlockSpec((None, 1, LANES), lambda b, j, i: (j, 0, 0)),
        ],
        out_specs=pl.BlockSpec((None, tq, LANES), lambda b, j, i: (b, i, j)),
        out_shape=jax.ShapeDtypeStruct((batch, seq, FOX_WIDTH), BF16),
        scratch_shapes=[pltpu.VMEM((tq, LANES), F32) for _ in range(4)],
        compiler_params=_cparams(("parallel", "parallel", "arbitrary")),
        name="fox_attention",
    )(fq, fk, fv, cum, gain_pairs)


GDN_BLK = 4 * CHUNK
QK_BLK = 2 * CHUNK


def _gdn_prep_kernel(x_ref, halo_ref, cw_ref, sm_ref, smt_ref,
                     u_ref, w_ref, qd_ref, qk_ref, kdt_ref, egl_ref, xpad_ref):
    i = pl.program_id(1)
    nb = GDN_BLK
    halo = halo_ref[...]
    xpad_ref[0:SUBLANES, :] = jnp.where(i > 0, halo, jnp.zeros_like(halo))
    xpad_ref[SUBLANES:, :] = x_ref[...]
    y = None
    for j in range(CONV_WIDTH):
        start = SUBLANES - (CONV_WIDTH - 1) + j
        term = cw_ref[j:j + 1, :] * xpad_ref[start:start + nb, :]
        y = term if y is None else y + term
    y = y * _sigmoid(y)

    r = lax.broadcasted_iota(jnp.int32, (nb, nb), 0)
    c = lax.broadcasted_iota(jnp.int32, (nb, nb), 1)
    chunk_shift = CHUNK.bit_length() - 1
    same = jnp.right_shift(r, chunk_shift) == jnp.right_shift(c, chunk_shift)
    incl = same & (c <= r)
    strict = same & (c < r)
    sm = sm_ref[...]
    g_cum = _dot_mask_left(incl, sm)
    g_tot = _dot_mask_left(same, sm)
    g_cum_t = _dot_mask_right(smt_ref[...], same & (r <= c))
    eye = (r == c).astype(F32)

    powers, t_invs, rhs = [], [], []
    for h in range(GDN_HEADS):
        sl = slice(h * GDN_HEAD_DIM, (h + 1) * GDN_HEAD_DIM)
        q = y[:, h * GDN_HEAD_DIM:(h + 1) * GDN_HEAD_DIM]
        k = y[:, GDN_WIDTH + h * GDN_HEAD_DIM:GDN_WIDTH + (h + 1) * GDN_HEAD_DIM]
        v = y[:, 2 * GDN_WIDTH + h * GDN_HEAD_DIM:2 * GDN_WIDTH + (h + 1) * GDN_HEAD_DIM]
        qn = q * lax.rsqrt(jnp.sum(q * q, axis=-1, keepdims=True) + EPS) * GDN_HEAD_DIM ** -0.5
        kn = k * lax.rsqrt(jnp.sum(k * k, axis=-1, keepdims=True) + EPS)
        gc = g_cum[:, SM_G0 + h:SM_G0 + h + 1]
        gl = g_tot[:, SM_G0 + h:SM_G0 + h + 1]
        gr = g_cum_t[SM_G0 + h:SM_G0 + h + 1, :]
        beta = sm[:, SM_B0 + h:SM_B0 + h + 1]
        decay = jnp.where(incl, jnp.exp(jnp.where(incl, gc - gr, 0.0)), 0.0)
        qb, kb = qn.astype(BF16), kn.astype(BF16)
        kk = lax.dot_general(kb, kb, NT_DIMS, preferred_element_type=F32)
        a = jnp.where(strict, beta * kk * decay, 0.0)
        eg = jnp.exp(gc)
        powers.append(a)
        t_invs.append(eye - a)
        rhs.append(jnp.concatenate([(v * beta).astype(BF16), (kn * (beta * eg)).astype(BF16)], axis=1))
        qk = lax.dot_general(qb, kb, NT_DIMS, preferred_element_type=F32)
        qk = jnp.where(incl, qk * decay, 0.0).astype(BF16)
        qk_ref[:, h * QK_BLK:(h + 1) * QK_BLK] = jnp.concatenate(
            [qk[j * QK_BLK:(j + 1) * QK_BLK, j * QK_BLK:(j + 1) * QK_BLK] for j in range(nb // QK_BLK)], axis=0)
        qd_ref[:, sl] = (qn * eg).astype(BF16)
        kdt_ref[sl, :] = (kn * jnp.exp(gl - gc)).T.astype(BF16)
        egl_ref[:, sl] = jnp.broadcast_to(jnp.exp(gl), (nb, GDN_HEAD_DIM))

    for _ in range(5):
        for h in range(GDN_HEADS):
            pb = powers[h].astype(BF16)
            powers[h] = jnp.dot(pb, pb, preferred_element_type=F32)
        for h in range(GDN_HEADS):
            t_invs[h] = t_invs[h] + jnp.dot(t_invs[h].astype(BF16), powers[h].astype(BF16),
                                            preferred_element_type=F32)
    for h in range(GDN_HEADS):
        sl = slice(h * GDN_HEAD_DIM, (h + 1) * GDN_HEAD_DIM)
        uw = jnp.dot(t_invs[h].astype(BF16), rhs[h], preferred_element_type=F32)
        u_ref[:, sl] = uw[:, :GDN_HEAD_DIM]
        w_ref[:, sl] = uw[:, GDN_HEAD_DIM:].astype(BF16)


def _gdn_prep(gqkv, conv_w_t, sm, smt, *, batch, seq):
    nb = GDN_BLK
    bps = seq // nb
    hps = nb // SUBLANES
    width = 3 * GDN_WIDTH
    row = lambda w: pl.BlockSpec((None, nb, w), lambda b, i: (b, i, 0))
    return pl.pallas_call(
        _gdn_prep_kernel,
        grid=(batch, bps),
        in_specs=[
            pl.BlockSpec((nb, width), lambda b, i: (b * bps + i, 0)),
            pl.BlockSpec((SUBLANES, width), lambda b, i: (jnp.maximum((b * bps + i) * hps - 1, 0), 0)),
            pl.BlockSpec((CONV_WIDTH, width), lambda b, i: (0, 0)),
            pl.BlockSpec((nb, LANES), lambda b, i: (b * bps + i, 0)),
            pl.BlockSpec((LANES, nb), lambda b, i: (0, b * bps + i)),
        ],
        out_specs=(row(GDN_WIDTH), row(GDN_WIDTH), row(GDN_WIDTH), row(GDN_HEADS * QK_BLK),
                   pl.BlockSpec((None, GDN_WIDTH, nb), lambda b, i: (b, 0, i)), row(GDN_WIDTH)),
        out_shape=(
            jax.ShapeDtypeStruct((batch, seq, GDN_WIDTH), F32),
            jax.ShapeDtypeStruct((batch, seq, GDN_WIDTH), BF16),
            jax.ShapeDtypeStruct((batch, seq, GDN_WIDTH), BF16),
            jax.ShapeDtypeStruct((batch, seq, GDN_HEADS * QK_BLK), BF16),
            jax.ShapeDtypeStruct((batch, GDN_WIDTH, seq), BF16),
            jax.ShapeDtypeStruct((batch, seq, GDN_WIDTH), F32),
        ),
        scratch_shapes=[pltpu.VMEM((nb + SUBLANES, width), F32)],
        compiler_params=_cparams(("parallel", "parallel")),
        name="gdn_prep",
    )(gqkv, gqkv, conv_w_t, sm, smt)


def _gdn_scan_kernel(u_ref, w_ref, qd_ref, qk_ref, kdt_ref, egl_ref, z_ref, gain_ref, o_ref, s_ref, vz_ref,
                     *, batch):
    pb = QK_BLK

    @pl.when(pl.program_id(0) == 0)
    def _():
        s_ref[...] = jnp.zeros_like(s_ref)

    vz_ref[...] = jnp.zeros_like(vz_ref)
    for cidx in range(GDN_BLK // CHUNK):
        rows = slice(cidx * CHUNK, (cidx + 1) * CHUNK)
        blk_rows = slice(cidx * CHUNK // pb * pb, (cidx * CHUNK // pb + 1) * pb)
        for b in range(batch):
            for h in range(GDN_HEADS):
                bh = b * GDN_HEADS + h
                sl = slice(h * GDN_HEAD_DIM, (h + 1) * GDN_HEAD_DIM)
                s_old = s_ref[bh]
                lhs1 = jnp.concatenate([w_ref[b, rows, sl], qd_ref[b, rows, sl]], axis=0)
                r1 = jnp.dot(lhs1, s_old.astype(BF16), preferred_element_type=F32)
                v_new = u_ref[b, rows, sl] - r1[:CHUNK]
                vz_ref[bh, rows, :] = v_new.astype(BF16)
                lhs2 = jnp.concatenate([qk_ref[b, rows, h * pb:(h + 1) * pb], kdt_ref[b, sl, blk_rows]], axis=0)
                r2 = jnp.dot(lhs2, vz_ref[bh, blk_rows, :], preferred_element_type=F32)
                vz_ref[bh, rows, :] = jnp.zeros((CHUNK, GDN_HEAD_DIM), BF16)
                last = egl_ref[b, (cidx + 1) * CHUNK - 1:(cidx + 1) * CHUNK, sl]
                s_ref[bh] = s_old * last + r2[CHUNK:]
                o = r1[CHUNK:] + r2[:CHUNK]
                z = z_ref[b, rows, sl]
                o_ref[b, rows, sl] = (_rms(o, gain_ref[...]) * (z * _sigmoid(z))).astype(o_ref.dtype)


def _gdn_scan(u, w, qd, qk, kdt, egl, z, gain, *, batch, seq):
    nb = GDN_BLK
    row = lambda width: pl.BlockSpec((batch, nb, width), lambda i: (0, i, 0))
    return pl.pallas_call(
        functools.partial(_gdn_scan_kernel, batch=batch),
        grid=(seq // nb,),
        in_specs=[row(GDN_WIDTH), row(GDN_WIDTH), row(GDN_WIDTH), row(GDN_HEADS * QK_BLK),
                  pl.BlockSpec((batch, GDN_WIDTH, nb), lambda i: (0, 0, i)), row(GDN_WIDTH), row(GDN_WIDTH),
                  pl.BlockSpec((1, GDN_HEAD_DIM), lambda i: (0, 0))],
        out_specs=row(GDN_WIDTH),
        out_shape=jax.ShapeDtypeStruct((batch, seq, GDN_WIDTH), BF16),
        scratch_shapes=[pltpu.VMEM((batch * GDN_HEADS, GDN_HEAD_DIM, GDN_HEAD_DIM), F32),
                        pltpu.VMEM((batch * GDN_HEADS, nb, GDN_HEAD_DIM), BF16)],
        compiler_params=_cparams(("arbitrary",)),
        name="gdn_scan",
    )(u, w, qd, qk, kdt, egl, z, gain)


def _memkv_kernel(m_ref, gain_ref, w_ref, kg_ref, k_ref, v_ref):
    d = m_ref.shape[-1]
    hd = d // XA_HEADS
    mb = _rms(m_ref[...], gain_ref[...]).astype(BF16)
    kv = jnp.dot(mb, w_ref[...], preferred_element_type=F32)
    for h in range(XA_HEADS):
        sl = slice(h * hd, (h + 1) * hd)
        k_ref[:, sl] = _rms(kv[:, sl], kg_ref[...]).astype(BF16)
    v_ref[...] = kv[:, d:].astype(BF16)


def _memkv(mem2d, gain, w_xkv, xk_gain, *, batch, mem_len):
    d = mem2d.shape[-1]
    full = lambda shape: pl.BlockSpec(shape, lambda b: (0,) * len(shape))
    row = pl.BlockSpec((mem_len, d), lambda b: (b, 0))
    return pl.pallas_call(
        _memkv_kernel,
        grid=(batch,),
        in_specs=[row, full((1, d)), full((d, 2 * d)), full((1, d // XA_HEADS))],
        out_specs=(row, row),
        out_shape=(jax.ShapeDtypeStruct(mem2d.shape, BF16), jax.ShapeDtypeStruct(mem2d.shape, BF16)),
        compiler_params=_cparams(("parallel",)),
        name="mem_kv",
    )(mem2d, gain, w_xkv, xk_gain)


def _mid_kernel(x_ref, of_ref, og_ref, wo_ref, xg_ref, wq_ref, qg_ref, k_ref, v_ref, wxo_ref, mg_ref,
                wr_ref, br_ref,
                x2_ref, h3_ref, mi_ref, mw_ref, cnt_ref, carry_ref, *, tm):
    i = pl.program_id(0)
    d = x_ref.shape[-1]
    hd = d // XA_HEADS
    x1 = (x_ref[...]
          + jnp.dot(of_ref[...], wo_ref[0:FOX_WIDTH, :], preferred_element_type=F32)
          + jnp.dot(og_ref[...], wo_ref[FOX_WIDTH:, :], preferred_element_type=F32))
    h2 = _rms(x1, xg_ref[...]).astype(BF16)
    q = jnp.dot(h2, wq_ref[...], preferred_element_type=F32)
    heads = []
    for h in range(XA_HEADS):
        sl = slice(h * hd, (h + 1) * hd)
        qn = (_rms(q[:, sl], qg_ref[...]) * hd ** -0.5).astype(BF16)
        s = lax.dot_general(qn, k_ref[:, sl], NT_DIMS, preferred_element_type=F32)
        p = jnp.exp(s - jnp.max(s, axis=-1, keepdims=True))
        p = p * (1.0 / jnp.sum(p, axis=-1, keepdims=True))
        heads.append(jnp.dot(p.astype(BF16), v_ref[:, sl], preferred_element_type=F32).astype(BF16))
    x2 = x1 + jnp.dot(jnp.concatenate(heads, axis=-1), wxo_ref[...], preferred_element_type=F32)
    x2_ref[...] = x2
    h3 = _rms(x2, mg_ref[...])
    for cc in range(SUBLANES):
        h3_ref[pl.ds(cc, tm, stride=SUBLANES), :] = h3[:, cc * LANES:(cc + 1) * LANES]
    h_hi, h_lo = _split_bf16(h3)
    logits = (jnp.dot(h_hi, wr_ref[0], preferred_element_type=F32)
              + jnp.dot(h_lo, wr_ref[0], preferred_element_type=F32)
              + jnp.dot(h_hi, wr_ref[1], preferred_element_type=F32)) + br_ref[...]
    lane = lax.broadcasted_iota(jnp.int32, (tm, LANES), 1)
    work = logits
    vals, idxs = [], []
    onehot = jnp.zeros((tm, LANES), F32)
    for _ in range(TOP_K):
        mx = jnp.max(work, axis=-1, keepdims=True)
        idx = jnp.min(jnp.where(work == mx, lane, LANES), axis=-1, keepdims=True)
        sel = lane == idx
        onehot = jnp.where(sel, 1.0, onehot)
        work = jnp.where(sel, -jnp.inf, work)
        vals.append(mx)
        idxs.append(idx)
    es = [jnp.exp(v - vals[0]) for v in vals]
    inv_denom = 1.0 / (es[0] + es[1] + es[2] + es[3])

    @pl.when(i == 0)
    def _():
        carry_ref[...] = jnp.zeros_like(carry_ref)

    r = lax.broadcasted_iota(jnp.int32, (tm, tm), 0)
    c = lax.broadcasted_iota(jnp.int32, (tm, tm), 1)
    before = jnp.dot((c < r).astype(BF16), onehot.astype(BF16), preferred_element_type=F32) + carry_ref[...]
    mi = jnp.zeros((tm, LANES), F32)
    mw = jnp.zeros((tm, LANES), F32)
    for kk in range(TOP_K):
        rank = jnp.sum(jnp.where(lane == idxs[kk], before, 0.0), axis=-1, keepdims=True)
        mi = jnp.where(lane == kk, idxs[kk].astype(F32), mi)
        mi = jnp.where(lane == TOP_K + kk, rank, mi)
        mw = jnp.where(lane == kk, es[kk] * inv_denom, mw)
    mi_ref[...] = mi.T[0:2 * TOP_K, :].astype(jnp.int32)
    mw_ref[...] = mw
    total = carry_ref[...] + jnp.sum(onehot, axis=0, keepdims=True)
    carry_ref[...] = total
    cnt_ref[...] = jnp.broadcast_to(total, cnt_ref.shape).astype(jnp.int32)


def _mid(x2d, o_fox, o_gdn, w_out, xg, w_xq, xq_gain, kx, vx, w_xo, mg, w_r, b_r, *, seq, mem_len, tm):
    n, d = x2d.shape
    full = lambda shape: pl.BlockSpec(shape, lambda i: (0,) * len(shape))
    rows = lambda width: pl.BlockSpec((tm, width), lambda i: (i, 0))
    mem = pl.BlockSpec((mem_len, d), lambda i: (i // (seq // tm), 0))
    return pl.pallas_call(
        functools.partial(_mid_kernel, tm=tm),
        grid=(n // tm,),
        in_specs=[rows(d), rows(FOX_WIDTH), rows(GDN_WIDTH), full((d, d)), full((1, d)), full((d, d)),
                  full((1, d // XA_HEADS)), mem, mem, full((d, d)), full((1, d)), full((2, d, LANES)),
                  full((1, LANES))],
        out_specs=(rows(d), pl.BlockSpec((tm * d // LANES, LANES), lambda i: (i, 0)),
                   pl.BlockSpec((None, 2 * TOP_K, tm), lambda i: (i, 0, 0)), rows(LANES),
                   full((SUBLANES, LANES))),
        out_shape=(jax.ShapeDtypeStruct((n, d), F32), jax.ShapeDtypeStruct((n * d // LANES, LANES), F32),
                   jax.ShapeDtypeStruct((n // tm, 2 * TOP_K, tm), jnp.int32),
                   jax.ShapeDtypeStruct((n, LANES), F32),
                   jax.ShapeDtypeStruct((SUBLANES, LANES), jnp.int32)),
        scratch_shapes=[pltpu.VMEM((1, LANES), F32)],
        compiler_params=_cparams(("arbitrary",)),
        name="outproj_xattn_router",
    )(x2d, o_fox, o_gdn, w_out, xg, w_xq, xq_gain, kx, vx, w_xo, mg, w_r, b_r)


MOE_TM = 512
ROUTE_TOKENS = 512


def _rows_kernel(tbl_ref, mi_ref, pos_ref, te_ref, nv_ref):
    tm = mi_ref.shape[-1]
    experts = mi_ref[:, 0:TOP_K, :]
    base = jnp.zeros_like(experts)
    tile_start = lax.broadcasted_iota(jnp.int32, te_ref.shape, 1) * MOE_TM
    tile_expert = jnp.zeros(te_ref.shape, jnp.int32)
    tile_limit = jnp.zeros(te_ref.shape, jnp.int32)
    last_used = jnp.int32(0)
    for e in range(N_EXPERTS):
        start, end, count = tbl_ref[e], tbl_ref[N_EXPERTS + e], tbl_ref[2 * N_EXPERTS + e]
        base = jnp.where(experts == e, start, base)
        inside = (tile_start >= start) & (tile_start < end)
        tile_expert = jnp.where(inside, e, tile_expert)
        tile_limit = jnp.where(inside, start + count, tile_limit)
        last_used = jnp.where(count > 0, e, last_used)
    rows = (base + mi_ref[:, TOP_K:2 * TOP_K, :]) * SUBLANES
    for kk in range(TOP_K):
        pos_ref[:, :, kk * tm:(kk + 1) * tm] = rows[:, kk:kk + 1, :]
    valid = jnp.clip(tile_limit - tile_start, 0, MOE_TM)
    nv_ref[...] = valid
    te_ref[...] = jnp.where(valid > 0, tile_expert, last_used)


def _rows(tbl, mi, n_tiles):
    steps, _, tm = mi.shape
    lanes = pl.cdiv(n_tiles, LANES) * LANES
    whole = lambda shape: pl.BlockSpec(shape, lambda i, t: (0,) * len(shape))
    grid_spec = pltpu.PrefetchScalarGridSpec(
        num_scalar_prefetch=1,
        grid=(1,),
        in_specs=[whole(mi.shape)],
        out_specs=(whole((steps, 1, TOP_K * tm)), whole((1, lanes)), whole((1, lanes))),
    )
    pos, te, nv = pl.pallas_call(
        _rows_kernel,
        grid_spec=grid_spec,
        out_shape=(jax.ShapeDtypeStruct((steps, 1, TOP_K * tm), jnp.int32),
                   jax.ShapeDtypeStruct((1, lanes), jnp.int32), jax.ShapeDtypeStruct((1, lanes), jnp.int32)),
        compiler_params=_cparams(("arbitrary",)),
        name="moe_rows",
    )(tbl, mi)
    return pos, te[0, :n_tiles], nv[0, :n_tiles]


def _dispatch_kernel(pos_ref, src_ref, dst_ref, sem):
    nt = src_ref.shape[0] // SUBLANES

    def copy(t, kk):
        src = src_ref.at[pl.ds(pl.multiple_of(t * SUBLANES, SUBLANES), SUBLANES), :]
        row = pl.multiple_of(pos_ref[0, kk * nt + t], SUBLANES)
        return pltpu.make_async_copy(src, dst_ref.at[pl.ds(row, SUBLANES), :], sem)

    def issue(t, carry):
        for kk in range(TOP_K):
            copy(t, kk).start(priority=kk % 2)
        return carry

    lax.fori_loop(0, nt, issue, 0, unroll=4)

    for kk in range(TOP_K):
        pltpu.make_async_copy(src_ref, dst_ref.at[pl.ds(0, nt * SUBLANES), :], sem).wait()


def _dispatch(h3_tiles, pos, *, rows_padded):
    n = h3_tiles.shape[0] // SUBLANES
    nt = pos.shape[-1] // TOP_K
    return pl.pallas_call(
        _dispatch_kernel,
        grid=(n // nt,),
        in_specs=[pl.BlockSpec((None, 1, nt * TOP_K), lambda i: (i, 0, 0), memory_space=pltpu.SMEM),
                  pl.BlockSpec((nt * SUBLANES, LANES), lambda i: (i, 0))],
        out_specs=pl.BlockSpec(memory_space=pl.ANY),
        out_shape=jax.ShapeDtypeStruct((rows_padded * SUBLANES, LANES), F32),
        scratch_shapes=[pltpu.SemaphoreType.DMA],
        compiler_params=pltpu.CompilerParams(dimension_semantics=("arbitrary",), has_side_effects=True),
        name="moe_dispatch",
    )(pos, h3_tiles)


def _expert_kernel(te_ref, nv_ref, nu_ref, xs_ref, wgu_ref, bgu_ref, wd_ref, bd_ref, y_ref, wgu_bf, wd_bf, acc_ref,
                   *, chunk):
    i = pl.program_id(0)
    tm = xs_ref.shape[0] // SUBLANES
    d = wd_bf.shape[1]
    f = wd_bf.shape[0]
    nvalid = nv_ref[i]
    first = jnp.logical_or(i == 0, te_ref[i] != te_ref[jnp.maximum(i - 1, 0)])

    @pl.when(jnp.logical_and(first, nvalid > 0))
    def _():
        wgu_bf[...] = wgu_ref[0].astype(BF16)
        wd_bf[...] = wd_ref[0].astype(BF16)

    @pl.when(nvalid > 0)
    def _():
        row = lax.broadcasted_iota(jnp.int32, (tm, 1), 0)
        x = jnp.concatenate([xs_ref[pl.ds(cc, tm, stride=SUBLANES), :] for cc in range(SUBLANES)], axis=-1)
        x = jnp.where(row < nvalid, x, 0.0).astype(BF16)
        for j in range(f // chunk):
            cs = slice(j * chunk, (j + 1) * chunk)
            us = slice(f + j * chunk, f + (j + 1) * chunk)
            g = jnp.dot(x, wgu_bf[:, cs], preferred_element_type=F32) + bgu_ref[0, :, cs]
            u = jnp.dot(x, wgu_bf[:, us], preferred_element_type=F32) + bgu_ref[0, :, us]
            gate = jnp.minimum(g, SWIGLU_LIMIT)
            up = jnp.clip(u, -SWIGLU_LIMIT, SWIGLU_LIMIT)
            act = ((up + 1.0) * (gate * _sigmoid(SWIGLU_ALPHA * gate))).astype(BF16)
            part = jnp.dot(act, wd_bf[cs, :], preferred_element_type=F32)
            if j == 0:
                acc_ref[...] = part + bd_ref[0]
            else:
                acc_ref[...] += part
        for cc in range(SUBLANES):
            y_ref[pl.ds(cc, tm, stride=SUBLANES), :] = acc_ref[:, cc * LANES:(cc + 1) * LANES]

    @pl.when(nvalid <= 0)
    def _():
        y_ref[...] = jnp.zeros_like(y_ref)


def _experts(tile_expert, tile_valid, tiles_used, xs, w_gate_up, b_gate_up, w_down, b_down):
    rows_padded = xs.shape[0] // SUBLANES
    e, d, f2 = w_gate_up.shape
    f = f2 // 2
    tm = MOE_TM
    grid_spec = pltpu.PrefetchScalarGridSpec(
        num_scalar_prefetch=3,
        grid=(rows_padded // tm,),
        in_specs=[
            pl.BlockSpec((tm * SUBLANES, LANES), lambda i, te, nv, nu: (jnp.minimum(i, nu[0]), 0)),
            pl.BlockSpec((1, d, f2), lambda i, te, nv, nu: (te[i], 0, 0)),
            pl.BlockSpec((1, 1, f2), lambda i, te, nv, nu: (te[i], 0, 0)),
            pl.BlockSpec((1, f, d), lambda i, te, nv, nu: (te[i], 0, 0)),
            pl.BlockSpec((1, 1, d), lambda i, te, nv, nu: (te[i], 0, 0)),
        ],
        out_specs=pl.BlockSpec((tm * SUBLANES, LANES), lambda i, te, nv, nu: (jnp.minimum(i, nu[0]), 0)),
        scratch_shapes=[pltpu.VMEM((d, f2), BF16), pltpu.VMEM((f, d), BF16), pltpu.VMEM((tm, d), F32)],
    )
    return pl.pallas_call(
        functools.partial(_expert_kernel, chunk=512),
        grid_spec=grid_spec,
        out_shape=jax.ShapeDtypeStruct(xs.shape, F32),
        compiler_params=_cparams(("arbitrary",)),
        name="moe_experts",
    )(tile_expert, tile_valid, tiles_used, xs, w_gate_up, b_gate_up.reshape(e, 1, f2), w_down, b_down.reshape(e, 1, d))


def _combine_kernel(pos_ref, y_ref, x2_ref, mw_ref, o_ref, ybuf, sems):
    nt = x2_ref.shape[0]
    parts = sems.shape[0]
    half = nt // parts

    def copy(t, kk, hf):
        row = pl.multiple_of(pos_ref[0, kk * nt + t], SUBLANES)
        dst = ybuf.at[kk, pl.ds(pl.multiple_of(t * SUBLANES, SUBLANES), SUBLANES), :]
        return pltpu.make_async_copy(y_ref.at[pl.ds(row, SUBLANES), :], dst, sems.at[hf])

    def issue_half(hf):
        def issue(t, carry):
            for kk in range(TOP_K):
                copy(t, kk, hf).start(priority=kk % 2)
            return carry

        lax.fori_loop(hf * half, (hf + 1) * half, issue, 0, unroll=4)

    def drain_half(hf):
        for kk in range(TOP_K):
            rows = pl.ds(hf * half * SUBLANES, half * SUBLANES)
            pltpu.make_async_copy(y_ref.at[rows, :], ybuf.at[kk, rows, :], sems.at[hf]).wait()

    def combine_half(hf):
        rows = slice(hf * half, (hf + 1) * half)
        mw = mw_ref[rows, :]
        for cc in range(SUBLANES):
            cs = slice(cc * LANES, (cc + 1) * LANES)
            acc = x2_ref[rows, cs]
            for kk in range(TOP_K):
                ys = ybuf.at[kk][pl.ds(hf * half * SUBLANES + cc, half, stride=SUBLANES), :]
                acc = acc + mw[:, kk:kk + 1] * ys
            o_ref[rows, cs] = acc

    for hf in range(parts):
        issue_half(hf)
    for hf in range(parts):
        drain_half(hf)
        combine_half(hf)


def _combine(pos, y, x2, mw):
    n, d = x2.shape
    nt = pos.shape[-1] // TOP_K
    return pl.pallas_call(
        _combine_kernel,
        grid=(n // nt,),
        in_specs=[pl.BlockSpec((None, 1, nt * TOP_K), lambda i: (i, 0, 0), memory_space=pltpu.SMEM),
                  pl.BlockSpec(memory_space=pl.ANY),
                  pl.BlockSpec((nt, d), lambda i: (i, 0)),
                  pl.BlockSpec((nt, LANES), lambda i: (i, 0))],
        out_specs=pl.BlockSpec((nt, d), lambda i: (i, 0)),
        out_shape=jax.ShapeDtypeStruct((n, d), F32),
        scratch_shapes=[pltpu.VMEM((TOP_K, nt * SUBLANES, LANES), F32), pltpu.SemaphoreType.DMA((4,))],
        compiler_params=_cparams(("arbitrary",)),
        name="moe_combine",
    )(pos, y, x2, mw)


def _layer(x, mem, mix_norm, w_in, b_forget, fox_q_gain, fox_k_gain, fox_out_gain, gdn_conv_w, gdn_a_log,
           gdn_dt_bias, gdn_out_gain, w_out, xattn_norm, mem_norm, w_xq, w_xkv, xq_gain, xk_gain, w_xo,
           moe_norm, w_router, b_router, w_gate_up, b_gate_up, w_down, b_down):
    batch, seq, d = x.shape
    n = batch * seq
    mem_len = mem.shape[1]
    x2d = x.reshape(n, d)

    o_ff = 3 * FOX_WIDTH
    o_gq = o_ff + FOX_HEADS
    o_ga = o_gq + 3 * GDN_WIDTH
    o_gb = o_ga + GDN_HEADS
    o_gz = o_gb + GDN_HEADS
    w_main = jnp.concatenate([w_in[:, :o_ff], w_in[:, o_gq:o_ga], w_in[:, o_gz:]], axis=1).astype(BF16)
    w_small = jnp.concatenate([w_in[:, o_ff:o_gq], w_in[:, o_ga:o_gz],
                               jnp.zeros((d, LANES - SM_END), F32)], axis=1).astype(BF16)
    prow = jnp.zeros((SUBLANES, LANES), F32)
    prow = prow.at[0, SM_F0:SM_G0].set(b_forget).at[0, SM_G0:SM_B0].set(gdn_dt_bias)
    prow = prow.at[1, SM_G0:SM_B0].set(gdn_a_log)
    pcol = prow[0:2].T
    head_id = jnp.arange(FOX_WIDTH) // FOX_HEAD_DIM
    bd = (head_id[:, None] == head_id[None, :]).astype(BF16)
    row1 = lambda v: v.reshape(1, -1)

    tm = min(512, seq)
    fq, fk, fv, gqkv, gz, sm, smt = _inproj(
        x2d, row1(mix_norm), w_main, w_small, w_small.T, bd,
        row1(jnp.tile(fox_q_gain, FOX_HEADS)), row1(jnp.tile(fox_k_gain, FOX_HEADS)), prow, pcol,
        seq=seq, tm=tm)

    b3 = lambda a: a.reshape(batch, seq, a.shape[-1])
    o_fox = _fox_attention(b3(fq), b3(fk), b3(fv), smt, fox_out_gain.reshape(FOX_HEADS // 2, 1, LANES),
                           batch=batch, seq=seq, tq=min(2048, seq), blk=min(512, seq))

    u, w, qd, qk, kdt, egl = _gdn_prep(gqkv, gdn_conv_w.T, sm, smt, batch=batch, seq=seq)
    o_gdn = _gdn_scan(u, w, qd, qk, kdt, egl, b3(gz), row1(gdn_out_gain), batch=batch, seq=seq)

    kx, vx = _memkv(mem.reshape(batch * mem_len, d), row1(mem_norm), w_xkv.astype(BF16), row1(xk_gain),
                    batch=batch, mem_len=mem_len)
    w_r = jnp.concatenate([w_router, jnp.zeros((d, LANES - N_EXPERTS), F32)], axis=1)
    w_r = jnp.stack(_split_bf16(w_r))
    b_r = jnp.concatenate([b_router, jnp.full((LANES - N_EXPERTS,), -jnp.inf, F32)]).reshape(1, LANES)
    x2, h3, mi, mw, cnt = _mid(
        x2d, o_fox.reshape(n, FOX_WIDTH), o_gdn.reshape(n, GDN_WIDTH), w_out.astype(BF16), row1(xattn_norm),
        w_xq.astype(BF16), row1(xq_gain), kx, vx, w_xo.astype(BF16), row1(moe_norm), w_r, b_r,
        seq=seq, mem_len=mem_len, tm=min(ROUTE_TOKENS, seq))

    counts = cnt[0, :N_EXPERTS]
    padded = ((counts + MOE_TM - 1) // MOE_TM) * MOE_TM
    ends = jnp.cumsum(padded)
    starts = ends - padded
    rows_padded = n * TOP_K + N_EXPERTS * MOE_TM
    n_tiles = rows_padded // MOE_TM
    tbl = jnp.concatenate([starts, ends, counts]).astype(jnp.int32)
    pos, tile_expert, tile_valid = _rows(tbl, mi, n_tiles)
    tiles_used = (ends[N_EXPERTS - 1] // MOE_TM).astype(jnp.int32).reshape(1)

    xs = _dispatch(h3, pos, rows_padded=rows_padded)
    y = _experts(tile_expert, tile_valid, tiles_used, xs, w_gate_up, b_gate_up, w_down, b_down)
    out = _combine(pos, y, x2, mw)
    return out.reshape(batch, seq, d)


def kernel(x, mem, mix_norm, w_in, b_forget, fox_q_gain, fox_k_gain, fox_out_gain, gdn_conv_w, gdn_a_log,
           gdn_dt_bias, gdn_out_gain, w_out, xattn_norm, mem_norm, w_xq, w_xkv, xq_gain, xk_gain, w_xo,
           moe_norm, w_router, b_router, w_gate_up, b_gate_up, w_down, b_down):
    depth = mix_norm.shape[0]
    for l in range(depth):
        x = _layer(x, mem, mix_norm[l], w_in[l], b_forget[l], fox_q_gain[l], fox_k_gain[l], fox_out_gain[l],
                   gdn_conv_w[l], gdn_a_log[l], gdn_dt_bias[l], gdn_out_gain[l], w_out[l], xattn_norm[l],
                   mem_norm[l], w_xq[l], w_xkv[l], xq_gain[l], xk_gain[l], w_xo[l], moe_norm[l], w_router[l],
                   b_router[l], w_gate_up[l], b_gate_up[l], w_down[l], b_down[l])
    return x
```

```python
import functools

import jax
import jax.numpy as jnp
from jax import lax
from jax.experimental import pallas as pl
from jax.experimental.pallas import tpu as pltpu

F32 = jnp.float32
BF16 = jnp.bfloat16

EPS = 1e-6
FOX_HEADS, FOX_HEAD_DIM = 8, 64
GDN_HEADS, GDN_HEAD_DIM = 4, 128
FOX_WIDTH = FOX_HEADS * FOX_HEAD_DIM
GDN_WIDTH = GDN_HEADS * GDN_HEAD_DIM
CONV_WIDTH = 4
CHUNK = 64
XA_HEADS = 4
N_EXPERTS = 32
TOP_K = 4
SWIGLU_LIMIT = 7.0
SWIGLU_ALPHA = 1.702
LOG2E = 1.4426950408889634

LANES = 128
SUBLANES = 8
VMEM_LIMIT = 52 * 1024 * 1024

SM_F0, SM_G0, SM_B0, SM_END = 0, 8, 12, 16

NT_DIMS = (((1,), (1,)), ((), ()))


def _cparams(sem):
    return pltpu.CompilerParams(dimension_semantics=sem, vmem_limit_bytes=VMEM_LIMIT)


def _rms(x, gain):
    return x * lax.rsqrt(jnp.mean(x * x, axis=-1, keepdims=True) + EPS) * gain


def _split_bf16(x):
    hi = x.astype(BF16)
    return hi, (x - hi.astype(F32)).astype(BF16)


def _dot_mask_left(mask, x):
    hi, lo = _split_bf16(x)
    mb = mask.astype(BF16)
    return jnp.dot(mb, hi, preferred_element_type=F32) + jnp.dot(mb, lo, preferred_element_type=F32)


def _dot_mask_right(x, mask):
    hi, lo = _split_bf16(x)
    mb = mask.astype(BF16)
    return jnp.dot(hi, mb, preferred_element_type=F32) + jnp.dot(lo, mb, preferred_element_type=F32)


def _softplus(t):
    return jnp.maximum(t, 0.0) + jnp.log1p(jnp.exp(-jnp.abs(t)))


def _sigmoid(t):
    return 1.0 / (1.0 + jnp.exp(-t))


def _small_act(v, bias, alog, idx):
    t = v + bias
    tail = jnp.log1p(jnp.exp(-jnp.abs(t)))
    log_f = jnp.minimum(t, 0.0) - tail
    g = -jnp.exp(alog) * (jnp.maximum(t, 0.0) + tail)
    beta = _sigmoid(v)
    return jnp.where(idx < SM_G0, log_f, jnp.where(idx < SM_B0, g, jnp.where(idx < SM_END, beta, 0.0)))


def _inproj_kernel(x_ref, gain_ref, w_ref, ws_ref, wst_ref, bd_ref, qg_ref, kg_ref, prow_ref, pcol_ref,
                   fq_ref, fk_ref, fv_ref, gqkv_ref, gz_ref, sm_ref, smt_ref, carry_r, carry_c,
                   *, tm, tiles_per_seq):
    i = pl.program_id(0)
    hb = _rms(x_ref[...], gain_ref[...]).astype(BF16)

    def proj(lo, hi):
        return jnp.dot(hb, w_ref[:, lo:hi], preferred_element_type=F32)

    def headnorm(p, g):
        ss = jnp.dot((p * p).astype(BF16), bd_ref[...], preferred_element_type=F32)
        return p * lax.rsqrt(ss * (1.0 / FOX_HEAD_DIM) + EPS) * g

    w0 = FOX_WIDTH
    fq_ref[...] = (headnorm(proj(0, w0), qg_ref[...]) * (FOX_HEAD_DIM ** -0.5 * LOG2E)).astype(BF16)
    fk_ref[...] = headnorm(proj(w0, 2 * w0), kg_ref[...]).astype(BF16)
    fv_ref[...] = proj(2 * w0, 3 * w0).astype(BF16)
    g0 = 3 * w0
    gqkv_ref[...] = proj(g0, g0 + 3 * GDN_WIDTH)
    gz_ref[...] = proj(g0 + 3 * GDN_WIDTH, g0 + 4 * GDN_WIDTH)

    @pl.when(i % tiles_per_seq == 0)
    def _():
        carry_r[...] = jnp.zeros_like(carry_r)
        carry_c[...] = jnp.zeros_like(carry_c)

    sm = jnp.dot(hb, ws_ref[...], preferred_element_type=F32)
    smt = lax.dot_general(wst_ref[...], hb, NT_DIMS, preferred_element_type=F32)
    lane = lax.broadcasted_iota(jnp.int32, (1, LANES), 1)
    srow = lax.broadcasted_iota(jnp.int32, (LANES, 1), 0)
    vals = _small_act(sm, prow_ref[0:1, :], prow_ref[1:2, :], lane)
    vals_t = _small_act(smt, pcol_ref[:, 0:1], pcol_ref[:, 1:2], srow)

    r = lax.broadcasted_iota(jnp.int32, (tm, tm), 0)
    c = lax.broadcasted_iota(jnp.int32, (tm, tm), 1)
    cum = _dot_mask_left(c <= r, vals) + carry_r[...]
    cum_t = _dot_mask_right(vals_t, r <= c) + carry_c[...]
    carry_r[...] = cum[tm - 1:tm, :]
    carry_c[...] = cum_t[:, tm - 1:tm]
    sm_ref[...] = jnp.where(lane < SM_G0, cum, vals)
    smt_ref[...] = jnp.where(srow < SM_G0, cum_t, vals_t)


def _inproj(x2d, gain, w_main, w_small, w_small_t, bd, qg, kg, prow, pcol, *, seq, tm):
    n, d = x2d.shape
    wm = w_main.shape[1]
    full = lambda shape: pl.BlockSpec(shape, lambda i: (0,) * len(shape))
    rows = lambda width: pl.BlockSpec((tm, width), lambda i: (i, 0))
    out_shape = (
        jax.ShapeDtypeStruct((n, FOX_WIDTH), BF16),
        jax.ShapeDtypeStruct((n, FOX_WIDTH), BF16),
        jax.ShapeDtypeStruct((n, FOX_WIDTH), BF16),
        jax.ShapeDtypeStruct((n, 3 * GDN_WIDTH), F32),
        jax.ShapeDtypeStruct((n, GDN_WIDTH), F32),
        jax.ShapeDtypeStruct((n, LANES), F32),
        jax.ShapeDtypeStruct((LANES, n), F32),
    )
    return pl.pallas_call(
        functools.partial(_inproj_kernel, tm=tm, tiles_per_seq=seq // tm),
        grid=(n // tm,),
        in_specs=[rows(d), full((1, d)), full((d, wm)), full((d, LANES)), full((LANES, d)),
                  full((FOX_WIDTH, FOX_WIDTH)), full((1, FOX_WIDTH)), full((1, FOX_WIDTH)),
                  full((SUBLANES, LANES)), full((LANES, 2))],
        out_specs=(rows(FOX_WIDTH), rows(FOX_WIDTH), rows(FOX_WIDTH), rows(3 * GDN_WIDTH), rows(GDN_WIDTH),
                   rows(LANES), pl.BlockSpec((LANES, tm), lambda i: (0, i))),
        out_shape=out_shape,
        scratch_shapes=[pltpu.VMEM((1, LANES), F32), pltpu.VMEM((LANES, 1), F32)],
        compiler_params=_cparams(("arbitrary",)),
        name="inproj",
    )(x2d, gain, w_main, w_small, w_small_t, bd, qg, kg, prow, pcol)


def _fox_kernel(q_ref, k_ref, v_ref, cum_ref, gain_ref, o_ref, m0_ref, m1_ref, acc0_ref, acc1_ref, *, tq, blk):
    qi = pl.program_id(2)
    head0 = 2 * pl.program_id(1)
    lane = lax.broadcasted_iota(jnp.int32, (1, LANES), 1)
    lo = lane < FOX_HEAD_DIM
    q = q_ref[...]
    zero = jnp.zeros_like(q)
    q_heads = (jnp.where(lo, q, zero), jnp.where(lo, zero, q))
    q0 = pl.multiple_of(qi * tq, tq)
    head_row = lax.broadcasted_iota(jnp.int32, (FOX_HEADS, 1), 0)

    def cum_row(hh, start, size):
        block = cum_ref[:, pl.ds(start, size)]
        return jnp.sum(jnp.where(head_row == head0 + hh, block, 0.0), axis=0, keepdims=True)

    c_ref = [cum_row(hh, q0, LANES)[:, 0:1] for hh in range(2)]
    head_lanes = (lo, jnp.logical_not(lo))
    sum_lane = (FOX_HEAD_DIM, 0)
    ones_col = [jnp.where(lane == sum_lane[hh], 1.0, 0.0).astype(BF16) for hh in range(2)]
    m_refs = (m0_ref, m1_ref)
    acc_refs = (acc0_ref, acc1_ref)
    for hh in range(2):
        m_refs[hh][...] = jnp.full_like(m_refs[hh], -1e30)
        acc_refs[hh][...] = jnp.zeros_like(acc_refs[hh])

    def step(k0, r0=None):
        rows = slice(0 if r0 is None else r0, tq)
        nr = rows.stop - rows.start
        kb = k_ref[pl.ds(k0, blk), :]
        vb = v_ref[pl.ds(k0, blk), :]
        scores = [lax.dot_general(q_heads[hh][rows], kb, NT_DIMS, preferred_element_type=F32) for hh in range(2)]
        for hh in range(2):
            s = scores[hh] + (c_ref[hh] - cum_row(hh, k0, blk)) * LOG2E
            if r0 is not None:
                r = lax.broadcasted_iota(jnp.int32, (nr, blk), 0)
                c = lax.broadcasted_iota(jnp.int32, (nr, blk), 1)
                s = jnp.where(c <= r, s, -jnp.inf)
            m_old = m_refs[hh][rows, :]
            m_new = jnp.maximum(m_old, jnp.max(s, axis=-1, keepdims=True))
            alpha = jnp.exp2(m_old - m_new)
            p = jnp.exp2(s - jnp.concatenate([m_new] * (blk // LANES), axis=1))
            v_h = jnp.where(head_lanes[hh], vb, ones_col[hh])
            pv = jnp.dot(p.astype(BF16), v_h, preferred_element_type=F32)
            acc_refs[hh][rows, :] = acc_refs[hh][rows, :] * alpha + pv
            m_refs[hh][rows, :] = m_new

    def body(kp, carry):
        step(pl.multiple_of(kp * (2 * blk), blk))
        step(pl.multiple_of(kp * (2 * blk) + blk, blk))
        return carry

    n_full = qi * (tq // blk)
    lax.fori_loop(0, n_full // 2, body, 0)

    @pl.when(n_full % 2 == 1)
    def _():
        step(pl.multiple_of((n_full - 1) * blk, blk))

    for j in range(tq // blk):
        step(pl.multiple_of(q0 + j * blk, blk), r0=j * blk)

    acc0, acc1 = acc0_ref[...], acc1_ref[...]
    l0 = acc0[:, sum_lane[0]:sum_lane[0] + 1]
    l1 = acc1[:, sum_lane[1]:sum_lane[1] + 1]
    o = jnp.where(lo, acc0 * (1.0 / l0), acc1 * (1.0 / l1))
    o2 = o * o
    ss_lo = jnp.sum(jnp.where(lo, o2, 0.0), axis=-1, keepdims=True)
    ss_hi = jnp.sum(jnp.where(lo, 0.0, o2), axis=-1, keepdims=True)
    ms = jnp.where(lo, ss_lo, ss_hi) * (1.0 / FOX_HEAD_DIM)
    o_ref[...] = (o * lax.rsqrt(ms + EPS) * gain_ref[...]).astype(o_ref.dtype)


def _fox_attention(fq, fk, fv, cum, gain_pairs, *, batch, seq, tq, blk):
    npairs = FOX_HEADS // 2
    return pl.pallas_call(
        functools.partial(_fox_kernel, tq=tq, blk=blk),
        grid=(batch, npairs, seq // tq),
        in_specs=[
            pl.BlockSpec((None, tq, LANES), lambda b, j, i: (b, i, j)),
            pl.BlockSpec((None, seq, LANES), lambda b, j, i: (b, 0, j)),
            pl.BlockSpec((None, seq, LANES), lambda b, j, i: (b, 0, j)),
            pl.BlockSpec((FOX_HEADS, seq), lambda b, j, i: (0, b)),
            pl.BlockSpec((None, 1, LANES), lambda b, j, i: (j, 0, 0)),
        ],
        out_specs=pl.BlockSpec((None, tq, LANES), lambda b, j, i: (b, i, j)),
        out_shape=jax.ShapeDtypeStruct((batch, seq, FOX_WIDTH), BF16),
        scratch_shapes=[pltpu.VMEM((tq, LANES), F32) for _ in range(4)],
        compiler_params=_cparams(("parallel", "parallel", "arbitrary")),
        name="fox_attention",
    )(fq, fk, fv, cum, gain_pairs)


GDN_BLK = 4 * CHUNK
QK_BLK = 2 * CHUNK


def _gdn_prep_kernel(x_ref, halo_ref, cw_ref, sm_ref, smt_ref,
                     u_ref, w_ref, qd_ref, qk_ref, kdt_ref, egl_ref, xpad_ref):
    i = pl.program_id(1)
    nb = GDN_BLK
    halo = halo_ref[...]
    xpad_ref[0:SUBLANES, :] = jnp.where(i > 0, halo, jnp.zeros_like(halo))
    xpad_ref[SUBLANES:, :] = x_ref[...]
    y = None
    for j in range(CONV_WIDTH):
        start = SUBLANES - (CONV_WIDTH - 1) + j
        term = cw_ref[j:j + 1, :] * xpad_ref[start:start + nb, :]
        y = term if y is None else y + term
    y = y * _sigmoid(y)

    r = lax.broadcasted_iota(jnp.int32, (nb, nb), 0)
    c = lax.broadcasted_iota(jnp.int32, (nb, nb), 1)
    chunk_shift = CHUNK.bit_length() - 1
    same = jnp.right_shift(r, chunk_shift) == jnp.right_shift(c, chunk_shift)
    incl = same & (c <= r)
    strict = same & (c < r)
    sm = sm_ref[...]
    g_cum = _dot_mask_left(incl, sm)
    g_tot = _dot_mask_left(same, sm)
    g_cum_t = _dot_mask_right(smt_ref[...], same & (r <= c))
    eye = (r == c).astype(F32)

    powers, t_invs, rhs = [], [], []
    for h in range(GDN_HEADS):
        sl = slice(h * GDN_HEAD_DIM, (h + 1) * GDN_HEAD_DIM)
        q = y[:, h * GDN_HEAD_DIM:(h + 1) * GDN_HEAD_DIM]
        k = y[:, GDN_WIDTH + h * GDN_HEAD_DIM:GDN_WIDTH + (h + 1) * GDN_HEAD_DIM]
        v = y[:, 2 * GDN_WIDTH + h * GDN_HEAD_DIM:2 * GDN_WIDTH + (h + 1) * GDN_HEAD_DIM]
        qn = q * lax.rsqrt(jnp.sum(q * q, axis=-1, keepdims=True) + EPS) * GDN_HEAD_DIM ** -0.5
        kn = k * lax.rsqrt(jnp.sum(k * k, axis=-1, keepdims=True) + EPS)
        gc = g_cum[:, SM_G0 + h:SM_G0 + h + 1]
        gl = g_tot[:, SM_G0 + h:SM_G0 + h + 1]
        gr = g_cum_t[SM_G0 + h:SM_G0 + h + 1, :]
        beta = sm[:, SM_B0 + h:SM_B0 + h + 1]
        decay = jnp.where(incl, jnp.exp(jnp.where(incl, gc - gr, 0.0)), 0.0)
        qb, kb = qn.astype(BF16), kn.astype(BF16)
        kk = lax.dot_general(kb, kb, NT_DIMS, preferred_element_type=F32)
        a = jnp.where(strict, beta * kk * decay, 0.0)
        eg = jnp.exp(gc)
        powers.append(a)
        t_invs.append(eye - a)
        rhs.append(jnp.concatenate([(v * beta).astype(BF16), (kn * (beta * eg)).astype(BF16)], axis=1))
        qk = lax.dot_general(qb, kb, NT_DIMS, preferred_element_type=F32)
        qk = jnp.where(incl, qk * decay, 0.0).astype(BF16)
        qk_ref[:, h * QK_BLK:(h + 1) * QK_BLK] = jnp.concatenate(
            [qk[j * QK_BLK:(j + 1) * QK_BLK, j * QK_BLK:(j + 1) * QK_BLK] for j in range(nb // QK_BLK)], axis=0)
        qd_ref[:, sl] = (qn * eg).astype(BF16)
        kdt_ref[sl, :] = (kn * jnp.exp(gl - gc)).T.astype(BF16)
        egl_ref[:, sl] = jnp.broadcast_to(jnp.exp(gl), (nb, GDN_HEAD_DIM))

    for _ in range(5):
        for h in range(GDN_HEADS):
            pb = powers[h].astype(BF16)
            powers[h] = jnp.dot(pb, pb, preferred_element_type=F32)
        for h in range(GDN_HEADS):
            t_invs[h] = t_invs[h] + jnp.dot(t_invs[h].astype(BF16), powers[h].astype(BF16),
                                            preferred_element_type=F32)
    for h in range(GDN_HEADS):
        sl = slice(h * GDN_HEAD_DIM, (h + 1) * GDN_HEAD_DIM)
        uw = jnp.dot(t_invs[h].astype(BF16), rhs[h], preferred_element_type=F32)
        u_ref[:, sl] = uw[:, :GDN_HEAD_DIM]
        w_ref[:, sl] = uw[:, GDN_HEAD_DIM:].astype(BF16)


def _gdn_prep(gqkv, conv_w_t, sm, smt, *, batch, seq):
    nb = GDN_BLK
    bps = seq // nb
    hps = nb // SUBLANES
    width = 3 * GDN_WIDTH
    row = lambda w: pl.BlockSpec((None, nb, w), lambda b, i: (b, i, 0))
    return pl.pallas_call(
        _gdn_prep_kernel,
        grid=(batch, bps),
        in_specs=[
            pl.BlockSpec((nb, width), lambda b, i: (b * bps + i, 0)),
            pl.BlockSpec((SUBLANES, width), lambda b, i: (jnp.maximum((b * bps + i) * hps - 1, 0), 0)),
            pl.BlockSpec((CONV_WIDTH, width), lambda b, i: (0, 0)),
            pl.BlockSpec((nb, LANES), lambda b, i: (b * bps + i, 0)),
            pl.BlockSpec((LANES, nb), lambda b, i: (0, b * bps + i)),
        ],
        out_specs=(row(GDN_WIDTH), row(GDN_WIDTH), row(GDN_WIDTH), row(GDN_HEADS * QK_BLK),
                   pl.BlockSpec((None, GDN_WIDTH, nb), lambda b, i: (b, 0, i)), row(GDN_WIDTH)),
        out_shape=(
            jax.ShapeDtypeStruct((batch, seq, GDN_WIDTH), F32),
            jax.ShapeDtypeStruct((batch, seq, GDN_WIDTH), BF16),
            jax.ShapeDtypeStruct((batch, seq, GDN_WIDTH), BF16),
            jax.ShapeDtypeStruct((batch, seq, GDN_HEADS * QK_BLK), BF16),
            jax.ShapeDtypeStruct((batch, GDN_WIDTH, seq), BF16),
            jax.ShapeDtypeStruct((batch, seq, GDN_WIDTH), F32),
        ),
        scratch_shapes=[pltpu.VMEM((nb + SUBLANES, width), F32)],
        compiler_params=_cparams(("parallel", "parallel")),
        name="gdn_prep",
    )(gqkv, gqkv, conv_w_t, sm, smt)


def _gdn_scan_kernel(u_ref, w_ref, qd_ref, qk_ref, kdt_ref, egl_ref, z_ref, gain_ref, o_ref, s_ref, vz_ref,
                     *, batch):
    pb = QK_BLK

    @pl.when(pl.program_id(0) == 0)
    def _():
        s_ref[...] = jnp.zeros_like(s_ref)

    vz_ref[...] = jnp.zeros_like(vz_ref)
    for cidx in range(GDN_BLK // CHUNK):
        rows = slice(cidx * CHUNK, (cidx + 1) * CHUNK)
        blk_rows = slice(cidx * CHUNK // pb * pb, (cidx * CHUNK // pb + 1) * pb)
        for b in range(batch):
            for h in range(GDN_HEADS):
                bh = b * GDN_HEADS + h
                sl = slice(h * GDN_HEAD_DIM, (h + 1) * GDN_HEAD_DIM)
                s_old = s_ref[bh]
                lhs1 = jnp.concatenate([w_ref[b, rows, sl], qd_ref[b, rows, sl]], axis=0)
                r1 = jnp.dot(lhs1, s_old.astype(BF16), preferred_element_type=F32)
                v_new = u_ref[b, rows, sl] - r1[:CHUNK]
                vz_ref[bh, rows, :] = v_new.astype(BF16)
                lhs2 = jnp.concatenate([qk_ref[b, rows, h * pb:(h + 1) * pb], kdt_ref[b, sl, blk_rows]], axis=0)
                r2 = jnp.dot(lhs2, vz_ref[bh, blk_rows, :], preferred_element_type=F32)
                vz_ref[bh, rows, :] = jnp.zeros((CHUNK, GDN_HEAD_DIM), BF16)
                last = egl_ref[b, (cidx + 1) * CHUNK - 1:(cidx + 1) * CHUNK, sl]
                s_ref[bh] = s_old * last + r2[CHUNK:]
                o = r1[CHUNK:] + r2[:CHUNK]
                z = z_ref[b, rows, sl]
                o_ref[b, rows, sl] = (_rms(o, gain_ref[...]) * (z * _sigmoid(z))).astype(o_ref.dtype)


def _gdn_scan(u, w, qd, qk, kdt, egl, z, gain, *, batch, seq):
    nb = GDN_BLK
    row = lambda width: pl.BlockSpec((batch, nb, width), lambda i: (0, i, 0))
    return pl.pallas_call(
        functools.partial(_gdn_scan_kernel, batch=batch),
        grid=(seq // nb,),
        in_specs=[row(GDN_WIDTH), row(GDN_WIDTH), row(GDN_WIDTH), row(GDN_HEADS * QK_BLK),
                  pl.BlockSpec((batch, GDN_WIDTH, nb), lambda i: (0, 0, i)), row(GDN_WIDTH), row(GDN_WIDTH),
                  pl.BlockSpec((1, GDN_HEAD_DIM), lambda i: (0, 0))],
        out_specs=row(GDN_WIDTH),
        out_shape=jax.ShapeDtypeStruct((batch, seq, GDN_WIDTH), BF16),
        scratch_shapes=[pltpu.VMEM((batch * GDN_HEADS, GDN_HEAD_DIM, GDN_HEAD_DIM), F32),
                        pltpu.VMEM((batch * GDN_HEADS, nb, GDN_HEAD_DIM), BF16)],
        compiler_params=_cparams(("arbitrary",)),
        name="gdn_scan",
    )(u, w, qd, qk, kdt, egl, z, gain)


def _memkv_kernel(m_ref, gain_ref, w_ref, kg_ref, k_ref, v_ref):
    d = m_ref.shape[-1]
    hd = d // XA_HEADS
    mb = _rms(m_ref[...], gain_ref[...]).astype(BF16)
    kv = jnp.dot(mb, w_ref[...], preferred_element_type=F32)
    for h in range(XA_HEADS):
        sl = slice(h * hd, (h + 1) * hd)
        k_ref[:, sl] = _rms(kv[:, sl], kg_ref[...]).astype(BF16)
    v_ref[...] = kv[:, d:].astype(BF16)


def _memkv(mem2d, gain, w_xkv, xk_gain, *, batch, mem_len):
    d = mem2d.shape[-1]
    full = lambda shape: pl.BlockSpec(shape, lambda b: (0,) * len(shape))
    row = pl.BlockSpec((mem_len, d), lambda b: (b, 0))
    return pl.pallas_call(
        _memkv_kernel,
        grid=(batch,),
        in_specs=[row, full((1, d)), full((d, 2 * d)), full((1, d // XA_HEADS))],
        out_specs=(row, row),
        out_shape=(jax.ShapeDtypeStruct(mem2d.shape, BF16), jax.ShapeDtypeStruct(mem2d.shape, BF16)),
        compiler_params=_cparams(("parallel",)),
        name="mem_kv",
    )(mem2d, gain, w_xkv, xk_gain)


def _mid_kernel(x_ref, of_ref, og_ref, wo_ref, xg_ref, wq_ref, qg_ref, k_ref, v_ref, wxo_ref, mg_ref,
                wr_ref, br_ref,
                x2_ref, h3_ref, mi_ref, mw_ref, cnt_ref, carry_ref, *, tm):
    i = pl.program_id(0)
    d = x_ref.shape[-1]
    hd = d // XA_HEADS
    x1 = (x_ref[...]
          + jnp.dot(of_ref[...], wo_ref[0:FOX_WIDTH, :], preferred_element_type=F32)
          + jnp.dot(og_ref[...], wo_ref[FOX_WIDTH:, :], preferred_element_type=F32))
    h2 = _rms(x1, xg_ref[...]).astype(BF16)
    q = jnp.dot(h2, wq_ref[...], preferred_element_type=F32)
    heads = []
    for h in range(XA_HEADS):
        sl = slice(h * hd, (h + 1) * hd)
        qn = (_rms(q[:, sl], qg_ref[...]) * hd ** -0.5).astype(BF16)
        s = lax.dot_general(qn, k_ref[:, sl], NT_DIMS, preferred_element_type=F32)
        p = jnp.exp(s - jnp.max(s, axis=-1, keepdims=True))
        p = p * (1.0 / jnp.sum(p, axis=-1, keepdims=True))
        heads.append(jnp.dot(p.astype(BF16), v_ref[:, sl], preferred_element_type=F32).astype(BF16))
    x2 = x1 + jnp.dot(jnp.concatenate(heads, axis=-1), wxo_ref[...], preferred_element_type=F32)
    x2_ref[...] = x2
    h3 = _rms(x2, mg_ref[...])
    for cc in range(SUBLANES):
        h3_ref[pl.ds(cc, tm, stride=SUBLANES), :] = h3[:, cc * LANES:(cc + 1) * LANES]
    h_hi, h_lo = _split_bf16(h3)
    logits = (jnp.dot(h_hi, wr_ref[0], preferred_element_type=F32)
              + jnp.dot(h_lo, wr_ref[0], preferred_element_type=F32)
              + jnp.dot(h_hi, wr_ref[1], preferred_element_type=F32)) + br_ref[...]
    lane = lax.broadcasted_iota(jnp.int32, (tm, LANES), 1)
    work = logits
    vals, idxs = [], []
    onehot = jnp.zeros((tm, LANES), F32)
    for _ in range(TOP_K):
        mx = jnp.max(work, axis=-1, keepdims=True)
        idx = jnp.min(jnp.where(work == mx, lane, LANES), axis=-1, keepdims=True)
        sel = lane == idx
        onehot = jnp.where(sel, 1.0, onehot)
        work = jnp.where(sel, -jnp.inf, work)
        vals.append(mx)
        idxs.append(idx)
    es = [jnp.exp(v - vals[0]) for v in vals]
    inv_denom = 1.0 / (es[0] + es[1] + es[2] + es[3])

    @pl.when(i == 0)
    def _():
        carry_ref[...] = jnp.zeros_like(carry_ref)

    r = lax.broadcasted_iota(jnp.int32, (tm, tm), 0)
    c = lax.broadcasted_iota(jnp.int32, (tm, tm), 1)
    before = jnp.dot((c < r).astype(BF16), onehot.astype(BF16), preferred_element_type=F32) + carry_ref[...]
    mi = jnp.zeros((tm, LANES), F32)
    mw = jnp.zeros((tm, LANES), F32)
    for kk in range(TOP_K):
        rank = jnp.sum(jnp.where(lane == idxs[kk], before, 0.0), axis=-1, keepdims=True)
        mi = jnp.where(lane == kk, idxs[kk].astype(F32), mi)
        mi = jnp.where(lane == TOP_K + kk, rank, mi)
        mw = jnp.where(lane == kk, es[kk] * inv_denom, mw)
    mi_ref[...] = mi.T[0:2 * TOP_K, :].astype(jnp.int32)
    mw_ref[...] = mw
    total = carry_ref[...] + jnp.sum(onehot, axis=0, keepdims=True)
    carry_ref[...] = total
    cnt_ref[...] = jnp.broadcast_to(total, cnt_ref.shape).astype(jnp.int32)


def _mid(x2d, o_fox, o_gdn, w_out, xg, w_xq, xq_gain, kx, vx, w_xo, mg, w_r, b_r, *, seq, mem_len, tm):
    n, d = x2d.shape
    full = lambda shape: pl.BlockSpec(shape, lambda i: (0,) * len(shape))
    rows = lambda width: pl.BlockSpec((tm, width), lambda i: (i, 0))
    mem = pl.BlockSpec((mem_len, d), lambda i: (i // (seq // tm), 0))
    return pl.pallas_call(
        functools.partial(_mid_kernel, tm=tm),
        grid=(n // tm,),
        in_specs=[rows(d), rows(FOX_WIDTH), rows(GDN_WIDTH), full((d, d)), full((1, d)), full((d, d)),
                  full((1, d // XA_HEADS)), mem, mem, full((d, d)), full((1, d)), full((2, d, LANES)),
                  full((1, LANES))],
        out_specs=(rows(d), pl.BlockSpec((tm * d // LANES, LANES), lambda i: (i, 0)),
                   pl.BlockSpec((None, 2 * TOP_K, tm), lambda i: (i, 0, 0)), rows(LANES),
                   full((SUBLANES, LANES))),
        out_shape=(jax.ShapeDtypeStruct((n, d), F32), jax.ShapeDtypeStruct((n * d // LANES, LANES), F32),
                   jax.ShapeDtypeStruct((n // tm, 2 * TOP_K, tm), jnp.int32),
                   jax.ShapeDtypeStruct((n, LANES), F32),
                   jax.ShapeDtypeStruct((SUBLANES, LANES), jnp.int32)),
        scratch_shapes=[pltpu.VMEM((1, LANES), F32)],
        compiler_params=_cparams(("arbitrary",)),
        name="outproj_xattn_router",
    )(x2d, o_fox, o_gdn, w_out, xg, w_xq, xq_gain, kx, vx, w_xo, mg, w_r, b_r)


MOE_TM = 512
ROUTE_TOKENS = 512


def _rows_kernel(tbl_ref, mi_ref, pos_ref, te_ref, nv_ref):
    tm = mi_ref.shape[-1]
    experts = mi_ref[:, 0:TOP_K, :]
    base = jnp.zeros_like(experts)
    tile_start = lax.broadcasted_iota(jnp.int32, te_ref.shape, 1) * MOE_TM
    tile_expert = jnp.zeros(te_ref.shape, jnp.int32)
    tile_limit = jnp.zeros(te_ref.shape, jnp.int32)
    last_used = jnp.int32(0)
    for e in range(N_EXPERTS):
        start, end, count = tbl_ref[e], tbl_ref[N_EXPERTS + e], tbl_ref[2 * N_EXPERTS + e]
        base = jnp.where(experts == e, start, base)
        inside = (tile_start >= start) & (tile_start < end)
        tile_expert = jnp.where(inside, e, tile_expert)
        tile_limit = jnp.where(inside, start + count, tile_limit)
        last_used = jnp.where(count > 0, e, last_used)
    rows = (base + mi_ref[:, TOP_K:2 * TOP_K, :]) * SUBLANES
    for kk in range(TOP_K):
        pos_ref[:, :, kk * tm:(kk + 1) * tm] = rows[:, kk:kk + 1, :]
    valid = jnp.clip(tile_limit - tile_start, 0, MOE_TM)
    nv_ref[...] = valid
    te_ref[...] = jnp.where(valid > 0, tile_expert, last_used)


def _rows(tbl, mi, n_tiles):
    steps, _, tm = mi.shape
    lanes = pl.cdiv(n_tiles, LANES) * LANES
    whole = lambda shape: pl.BlockSpec(shape, lambda i, t: (0,) * len(shape))
    grid_spec = pltpu.PrefetchScalarGridSpec(
        num_scalar_prefetch=1,
        grid=(1,),
        in_specs=[whole(mi.shape)],
        out_specs=(whole((steps, 1, TOP_K * tm)), whole((1, lanes)), whole((1, lanes))),
    )
    pos, te, nv = pl.pallas_call(
        _rows_kernel,
        grid_spec=grid_spec,
        out_shape=(jax.ShapeDtypeStruct((steps, 1, TOP_K * tm), jnp.int32),
                   jax.ShapeDtypeStruct((1, lanes), jnp.int32), jax.ShapeDtypeStruct((1, lanes), jnp.int32)),
        compiler_params=_cparams(("arbitrary",)),
        name="moe_rows",
    )(tbl, mi)
    return pos, te[0, :n_tiles], nv[0, :n_tiles]


def _dispatch_kernel(pos_ref, src_ref, dst_ref, sem):
    nt = src_ref.shape[0] // SUBLANES

    def copy(t, kk):
        src = src_ref.at[pl.ds(pl.multiple_of(t * SUBLANES, SUBLANES), SUBLANES), :]
        row = pl.multiple_of(pos_ref[0, kk * nt + t], SUBLANES)
        return pltpu.make_async_copy(src, dst_ref.at[pl.ds(row, SUBLANES), :], sem)

    def issue(t, carry):
        for kk in range(TOP_K):
            copy(t, kk).start(priority=kk % 2)
        return carry

    lax.fori_loop(0, nt, issue, 0, unroll=4)

    for kk in range(TOP_K):
        pltpu.make_async_copy(src_ref, dst_ref.at[pl.ds(0, nt * SUBLANES), :], sem).wait()


def _dispatch(h3_tiles, pos, *, rows_padded):
    n = h3_tiles.shape[0] // SUBLANES
    nt = pos.shape[-1] // TOP_K
    return pl.pallas_call(
        _dispatch_kernel,
        grid=(n // nt,),
        in_specs=[pl.BlockSpec((None, 1, nt * TOP_K), lambda i: (i, 0, 0), memory_space=pltpu.SMEM),
                  pl.BlockSpec((nt * SUBLANES, LANES), lambda i: (i, 0))],
        out_specs=pl.BlockSpec(memory_space=pl.ANY),
        out_shape=jax.ShapeDtypeStruct((rows_padded * SUBLANES, LANES), F32),
        scratch_shapes=[pltpu.SemaphoreType.DMA],
        compiler_params=pltpu.CompilerParams(dimension_semantics=("arbitrary",), has_side_effects=True),
        name="moe_dispatch",
    )(pos, h3_tiles)


def _expert_kernel(te_ref, nv_ref, nu_ref, xs_ref, wgu_ref, bgu_ref, wd_ref, bd_ref, y_ref, wgu_bf, wd_bf, acc_ref,
                   *, chunk):
    i = pl.program_id(0)
    tm = xs_ref.shape[0] // SUBLANES
    d = wd_bf.shape[1]
    f = wd_bf.shape[0]
    nvalid = nv_ref[i]
    first = jnp.logical_or(i == 0, te_ref[i] != te_ref[jnp.maximum(i - 1, 0)])

    @pl.when(jnp.logical_and(first, nvalid > 0))
    def _():
        wgu_bf[...] = wgu_ref[0].astype(BF16)
        wd_bf[...] = wd_ref[0].astype(BF16)

    @pl.when(nvalid > 0)
    def _():
        row = lax.broadcasted_iota(jnp.int32, (tm, 1), 0)
        x = jnp.concatenate([xs_ref[pl.ds(cc, tm, stride=SUBLANES), :] for cc in range(SUBLANES)], axis=-1)
        x = jnp.where(row < nvalid, x, 0.0).astype(BF16)
        for j in range(f // chunk):
            cs = slice(j * chunk, (j + 1) * chunk)
            us = slice(f + j * chunk, f + (j + 1) * chunk)
            g = jnp.dot(x, wgu_bf[:, cs], preferred_element_type=F32) + bgu_ref[0, :, cs]
            u = jnp.dot(x, wgu_bf[:, us], preferred_element_type=F32) + bgu_ref[0, :, us]
            gate = jnp.minimum(g, SWIGLU_LIMIT)
            up = jnp.clip(u, -SWIGLU_LIMIT, SWIGLU_LIMIT)
            act = ((up + 1.0) * (gate * _sigmoid(SWIGLU_ALPHA * gate))).astype(BF16)
            part = jnp.dot(act, wd_bf[cs, :], preferred_element_type=F32)
            if j == 0:
                acc_ref[...] = part + bd_ref[0]
            else:
                acc_ref[...] += part
        for cc in range(SUBLANES):
            y_ref[pl.ds(cc, tm, stride=SUBLANES), :] = acc_ref[:, cc * LANES:(cc + 1) * LANES]

    @pl.when(nvalid <= 0)
    def _():
        y_ref[...] = jnp.zeros_like(y_ref)


def _experts(tile_expert, tile_valid, tiles_used, xs, w_gate_up, b_gate_up, w_down, b_down):
    rows_padded = xs.shape[0] // SUBLANES
    e, d, f2 = w_gate_up.shape
    f = f2 // 2
    tm = MOE_TM
    grid_spec = pltpu.PrefetchScalarGridSpec(
        num_scalar_prefetch=3,
        grid=(rows_padded // tm,),
        in_specs=[
            pl.BlockSpec((tm * SUBLANES, LANES), lambda i, te, nv, nu: (jnp.minimum(i, nu[0]), 0)),
            pl.BlockSpec((1, d, f2), lambda i, te, nv, nu: (te[i], 0, 0)),
            pl.BlockSpec((1, 1, f2), lambda i, te, nv, nu: (te[i], 0, 0)),
            pl.BlockSpec((1, f, d), lambda i, te, nv, nu: (te[i], 0, 0)),
            pl.BlockSpec((1, 1, d), lambda i, te, nv, nu: (te[i], 0, 0)),
        ],
        out_specs=pl.BlockSpec((tm * SUBLANES, LANES), lambda i, te, nv, nu: (jnp.minimum(i, nu[0]), 0)),
        scratch_shapes=[pltpu.VMEM((d, f2), BF16), pltpu.VMEM((f, d), BF16), pltpu.VMEM((tm, d), F32)],
    )
    return pl.pallas_call(
        functools.partial(_expert_kernel, chunk=1024),
        grid_spec=grid_spec,
        out_shape=jax.ShapeDtypeStruct(xs.shape, F32),
        compiler_params=_cparams(("arbitrary",)),
        name="moe_experts",
    )(tile_expert, tile_valid, tiles_used, xs, w_gate_up, b_gate_up.reshape(e, 1, f2), w_down, b_down.reshape(e, 1, d))


def _combine_kernel(pos_ref, y_ref, x2_ref, mw_ref, o_ref, ybuf, sems):
    nt = x2_ref.shape[0]
    parts = sems.shape[0]
    half = nt // parts

    def copy(t, kk, hf):
        row = pl.multiple_of(pos_ref[0, kk * nt + t], SUBLANES)
        dst = ybuf.at[kk, pl.ds(pl.multiple_of(t * SUBLANES, SUBLANES), SUBLANES), :]
        return pltpu.make_async_copy(y_ref.at[pl.ds(row, SUBLANES), :], dst, sems.at[hf])

    def issue_half(hf):
        def issue(t, carry):
            for kk in range(TOP_K):
                copy(t, kk, hf).start(priority=kk % 2)
            return carry

        lax.fori_loop(hf * half, (hf + 1) * half, issue, 0, unroll=4)

    def drain_half(hf):
        for kk in range(TOP_K):
            rows = pl.ds(hf * half * SUBLANES, half * SUBLANES)
            pltpu.make_async_copy(y_ref.at[rows, :], ybuf.at[kk, rows, :], sems.at[hf]).wait()

    def combine_half(hf):
        rows = slice(hf * half, (hf + 1) * half)
        mw = mw_ref[rows, :]
        for cc in range(SUBLANES):
            cs = slice(cc * LANES, (cc + 1) * LANES)
            acc = x2_ref[rows, cs]
            for kk in range(TOP_K):
                ys = ybuf.at[kk][pl.ds(hf * half * SUBLANES + cc, half, stride=SUBLANES), :]
                acc = acc + mw[:, kk:kk + 1] * ys
            o_ref[rows, cs] = acc

    for hf in range(parts):
        issue_half(hf)
    for hf in range(parts):
        drain_half(hf)
        combine_half(hf)


def _combine(pos, y, x2, mw):
    n, d = x2.shape
    nt = pos.shape[-1] // TOP_K
    return pl.pallas_call(
        _combine_kernel,
        grid=(n // nt,),
        in_specs=[pl.BlockSpec((None, 1, nt * TOP_K), lambda i: (i, 0, 0), memory_space=pltpu.SMEM),
                  pl.BlockSpec(memory_space=pl.ANY),
                  pl.BlockSpec((nt, d), lambda i: (i, 0)),
                  pl.BlockSpec((nt, LANES), lambda i: (i, 0))],
        out_specs=pl.BlockSpec((nt, d), lambda i: (i, 0)),
        out_shape=jax.ShapeDtypeStruct((n, d), F32),
        scratch_shapes=[pltpu.VMEM((TOP_K, nt * SUBLANES, LANES), F32), pltpu.SemaphoreType.DMA((4,))],
        compiler_params=_cparams(("arbitrary",)),
        name="moe_combine",
    )(pos, y, x2, mw)


def _layer(x, mem, mix_norm, w_in, b_forget, fox_q_gain, fox_k_gain, fox_out_gain, gdn_conv_w, gdn_a_log,
           gdn_dt_bias, gdn_out_gain, w_out, xattn_norm, mem_norm, w_xq, w_xkv, xq_gain, xk_gain, w_xo,
           moe_norm, w_router, b_router, w_gate_up, b_gate_up, w_down, b_down):
    batch, seq, d = x.shape
    n = batch * seq
    mem_len = mem.shape[1]
    x2d = x.reshape(n, d)

    o_ff = 3 * FOX_WIDTH
    o_gq = o_ff + FOX_HEADS
    o_ga = o_gq + 3 * GDN_WIDTH
    o_gb = o_ga + GDN_HEADS
    o_gz = o_gb + GDN_HEADS
    w_main = jnp.concatenate([w_in[:, :o_ff], w_in[:, o_gq:o_ga], w_in[:, o_gz:]], axis=1).astype(BF16)
    w_small = jnp.concatenate([w_in[:, o_ff:o_gq], w_in[:, o_ga:o_gz],
                               jnp.zeros((d, LANES - SM_END), F32)], axis=1).astype(BF16)
    prow = jnp.zeros((SUBLANES, LANES), F32)
    prow = prow.at[0, SM_F0:SM_G0].set(b_forget).at[0, SM_G0:SM_B0].set(gdn_dt_bias)
    prow = prow.at[1, SM_G0:SM_B0].set(gdn_a_log)
    pcol = prow[0:2].T
    head_id = jnp.arange(FOX_WIDTH) // FOX_HEAD_DIM
    bd = (head_id[:, None] == head_id[None, :]).astype(BF16)
    row1 = lambda v: v.reshape(1, -1)

    tm = min(512, seq)
    fq, fk, fv, gqkv, gz, sm, smt = _inproj(
        x2d, row1(mix_norm), w_main, w_small, w_small.T, bd,
        row1(jnp.tile(fox_q_gain, FOX_HEADS)), row1(jnp.tile(fox_k_gain, FOX_HEADS)), prow, pcol,
        seq=seq, tm=tm)

    b3 = lambda a: a.reshape(batch, seq, a.shape[-1])
    o_fox = _fox_attention(b3(fq), b3(fk), b3(fv), smt, fox_out_gain.reshape(FOX_HEADS // 2, 1, LANES),
                           batch=batch, seq=seq, tq=min(2048, seq), blk=min(512, seq))

    u, w, qd, qk, kdt, egl = _gdn_prep(gqkv, gdn_conv_w.T, sm, smt, batch=batch, seq=seq)
    o_gdn = _gdn_scan(u, w, qd, qk, kdt, egl, b3(gz), row1(gdn_out_gain), batch=batch, seq=seq)

    kx, vx = _memkv(mem.reshape(batch * mem_len, d), row1(mem_norm), w_xkv.astype(BF16), row1(xk_gain),
                    batch=batch, mem_len=mem_len)
    w_r = jnp.concatenate([w_router, jnp.zeros((d, LANES - N_EXPERTS), F32)], axis=1)
    w_r = jnp.stack(_split_bf16(w_r))
    b_r = jnp.concatenate([b_router, jnp.full((LANES - N_EXPERTS,), -jnp.inf, F32)]).reshape(1, LANES)
    x2, h3, mi, mw, cnt = _mid(
        x2d, o_fox.reshape(n, FOX_WIDTH), o_gdn.reshape(n, GDN_WIDTH), w_out.astype(BF16), row1(xattn_norm),
        w_xq.astype(BF16), row1(xq_gain), kx, vx, w_xo.astype(BF16), row1(moe_norm), w_r, b_r,
        seq=seq, mem_len=mem_len, tm=min(ROUTE_TOKENS, seq))

    counts = cnt[0, :N_EXPERTS]
    padded = ((counts + MOE_TM - 1) // MOE_TM) * MOE_TM
    ends = jnp.cumsum(padded)
    starts = ends - padded
    rows_padded = n * TOP_K + N_EXPERTS * MOE_TM
    n_tiles = rows_padded // MOE_TM
    tbl = jnp.concatenate([starts, ends, counts]).astype(jnp.int32)
    pos, tile_expert, tile_valid = _rows(tbl, mi, n_tiles)
    tiles_used = (ends[N_EXPERTS - 1] // MOE_TM).astype(jnp.int32).reshape(1)

    xs = _dispatch(h3, pos, rows_padded=rows_padded)
    y = _experts(tile_expert, tile_valid, tiles_used, xs, w_gate_up, b_gate_up, w_down, b_down)
    out = _combine(pos, y, x2, mw)
    return out.reshape(batch, seq, d)


def kernel(x, mem, mix_norm, w_in, b_forget, fox_q_gain, fox_k_gain, fox_out_gain, gdn_conv_w, gdn_a_log,
           gdn_dt_bias, gdn_out_gain, w_out, xattn_norm, mem_norm, w_xq, w_xkv, xq_gain, xk_gain, w_xo,
           moe_norm, w_router, b_router, w_gate_up, b_gate_up, w_down, b_down):
    depth = mix_norm.shape[0]
    for l in range(depth):
        x = _layer(x, mem, mix_norm[l], w_in[l], b_forget[l], fox_q_gain[l], fox_k_gain[l], fox_out_gain[l],
                   gdn_conv_w[l], gdn_a_log[l], gdn_dt_bias[l], gdn_out_gain[l], w_out[l], xattn_norm[l],
                   mem_norm[l], w_xq[l], w_xkv[l], xq_gain[l], xk_gain[l], w_xo[l], moe_norm[l], w_router[l],
                   b_router[l], w_gate_up[l], b_gate_up[l], w_down[l], b_down[l])
    return x
```
